```python
import math
import jax
import jax.numpy as jnp
from jax import lax
import numpy as np

D_MODEL = 1024
BATCH = 8
SEQ = 4096
DEPTH = 4

HEAD_DIM = 64
D_FF = 2816
NORM_EPS = 1e-6
NUM_BUCKETS = 32
MAX_DISTANCE = 1024
BAND_BLOCK = 128
NSA_HEADS = 8
NSA_KV = 2
NSA_GROUP = NSA_HEADS // NSA_KV
CMP_LEN = 32
CMP_STRIDE = 16
CMP_HIDDEN = 128
SEL_LEN = 64
SEL_TOPK = 16
SEL_QCHUNK = 64
NSA_WINDOW = 512
SWA_HEADS = 8
SWA_KV = 2
SWA_GROUP = SWA_HEADS // SWA_KV
SWA_WINDOW = 128
MLA_HEADS = 8
MLA_Q_RANK = 384
MLA_KV_RANK = 256
MLA_NOPE = 64
MLA_ROPE = 32
MLA_V = 64
MLA_QBLOCK = 128
ROPE_BASE = 10000.0
N_BRANCH = 3
BRANCH_WIDTH = NSA_HEADS * HEAD_DIM
N_BIAS_HEADS = NSA_HEADS + SWA_HEADS
NEG_INF = -1e30
SEL_FORCE = 1e9
IN_SIZES = (
    NSA_HEADS * HEAD_DIM,
    NSA_KV * HEAD_DIM, NSA_KV * HEAD_DIM,
    NSA_KV * HEAD_DIM, NSA_KV * HEAD_DIM,
    NSA_KV * HEAD_DIM, NSA_KV * HEAD_DIM,
    3 * NSA_HEADS,
    SWA_HEADS * HEAD_DIM,
    SWA_KV * HEAD_DIM, SWA_KV * HEAD_DIM,
    MLA_Q_RANK,
    MLA_KV_RANK,
    MLA_ROPE,
    N_BRANCH * D_MODEL,
)
IN_WIDTH = sum(IN_SIZES)

kernel_name = "hybrid_gated_nsa_swa_mla_macaron"


def rmsnorm(x, g):
    xf = x.astype(jnp.float32)
    inv = lax.rsqrt(jnp.mean(xf * xf, axis=-1, keepdims=True) + NORM_EPS)
    return (xf * inv).astype(x.dtype) * g


def split_cols(z):
    outs = []
    off = 0
    for n in IN_SIZES:
        outs.append(z[..., off:off + n])
        off += n
    return outs


def t5_bucket(dist):
    max_exact = NUM_BUCKETS // 2
    d = jnp.maximum(dist, 0)
    df = jnp.maximum(d, 1).astype(jnp.float32)
    large = max_exact + (jnp.log(df / max_exact) / math.log(MAX_DISTANCE / max_exact)
                         * (NUM_BUCKETS - max_exact)).astype(jnp.int32)
    large = jnp.minimum(large, NUM_BUCKETS - 1)
    return jnp.where(d < max_exact, d, large)


def rel_bias(dist, table):
    return table[t5_bucket(dist)].astype(jnp.float32)


def rope(x, pos):
    half = x.shape[-1] // 2
    inv_freq = ROPE_BASE ** (-jnp.arange(half, dtype=jnp.float32) / half)
    ang = pos.astype(jnp.float32)[:, None] * inv_freq[None, :]
    cos = jnp.cos(ang)[None, :, None, :].astype(x.dtype)
    sin = jnp.sin(ang)[None, :, None, :].astype(x.dtype)
    x1, x2 = x[..., :half], x[..., half:]
    return jnp.concatenate([x1 * cos - x2 * sin, x1 * sin + x2 * cos], axis=-1)


def macaron_ffn(x, g_pre, w_gu, w_down, g_post):
    h = rmsnorm(x, g_pre)
    gu = h @ w_gu
    y = (jax.nn.silu(gu[..., :D_FF]) * gu[..., D_FF:]) @ w_down
    return x + 0.5 * rmsnorm(y, g_post)


def banded_attention(q, k, v, window, table, sinks=None):
    B, S, Hkv, G, d = q.shape
    blk = BAND_BLOCK
    nb = S // blk
    n_prev = window // blk
    span = (n_prev + 1) * blk
    pad = ((0, 0), (n_prev * blk, 0), (0, 0), (0, 0))
    kp = jnp.pad(k, pad)
    vp = jnp.pad(v, pad)
    qi = jnp.arange(blk)[:, None]
    kj = jnp.arange(span)[None, :]
    dist = qi + n_prev * blk - kj
    bias = rel_bias(dist, table).transpose(2, 0, 1).reshape(Hkv, G, blk, span)
    in_band = (dist >= 0) & (dist < window)
    scale = d ** -0.5

    def one_block(n):
        qb = lax.dynamic_slice_in_dim(q, n * blk, blk, axis=1)
        kb = lax.dynamic_slice_in_dim(kp, n * blk, span, axis=1)
        vb = lax.dynamic_slice_in_dim(vp, n * blk, span, axis=1)
        s = jnp.einsum('bqhgd,bkhd->bhgqk', qb, kb).astype(jnp.float32) * scale + bias
        valid = in_band & ((n - n_prev) * blk + kj >= 0)
        s = jnp.where(valid, s, NEG_INF)
        if sinks is None:
            p = jax.nn.softmax(s, axis=-1)
        else:
            sk = sinks.astype(jnp.float32).reshape(Hkv, G, 1, 1)
            m = jnp.maximum(jnp.max(s, axis=-1, keepdims=True), sk)
            e = jnp.exp(s - m)
            p = e / (jnp.sum(e, axis=-1, keepdims=True) + jnp.exp(sk - m))
        return jnp.einsum('bhgqk,bkhd->bqhgd', p.astype(v.dtype), vb)

    o = lax.map(one_block, jnp.arange(nb))
    return o.transpose(1, 0, 2, 3, 4, 5).reshape(B, S, Hkv, G, d)


def nsa_compress(kv, pe, w1, w2):
    B, S, Hkv, d = kv.shape
    nc = (S - CMP_LEN) // CMP_STRIDE + 1
    idx = jnp.arange(nc)[:, None] * CMP_STRIDE + jnp.arange(CMP_LEN)[None, :]
    blocks = kv[:, idx] + pe[:, None, :]
    flat = blocks.transpose(0, 1, 3, 2, 4).reshape(B, nc, Hkv, CMP_LEN * d)
    return jax.nn.gelu(flat @ w1) @ w2


def nsa_attention(q, k_c, v_c, k_s, v_s, k_w, v_w, gate, pos, table,
                  pe_k, w1_k, w2_k, pe_v, w1_v, w2_v):
    B, S, Hkv, G, d = q.shape
    scale = d ** -0.5
    kc = nsa_compress(k_c, pe_k, w1_k, w2_k)
    vc = nsa_compress(v_c, pe_v, w1_v, w2_v)
    nc = kc.shape[1]
    cmp_end = jnp.arange(nc) * CMP_STRIDE + CMP_LEN - 1
    dist_c = pos[:, None] - cmp_end[None, :]
    valid_c = dist_c >= 0
    bias_c = rel_bias(dist_c, table).transpose(2, 0, 1).reshape(Hkv, G, S, nc)
    s = jnp.einsum('bthgd,bchd->bhgtc', q, kc).astype(jnp.float32) * scale + bias_c
    s = jnp.where(valid_c, s, NEG_INF)
    e = jnp.where(valid_c, jnp.exp(s - jnp.max(s, axis=-1, keepdims=True)), 0.0)
    den = jnp.sum(e, axis=-1, keepdims=True)
    p_c = e / jnp.where(den > 0, den, 1.0)
    o_cmp = jnp.einsum('bhgtc,bchd->bthgd', p_c.astype(vc.dtype), vc)

    n_sel = S // SEL_LEN
    top_k = min(SEL_TOPK, n_sel)
    ci = jnp.arange(nc)[:, None] * CMP_STRIDE
    sj = jnp.arange(n_sel)[None, :] * SEL_LEN
    overlap = jnp.maximum(jnp.minimum(ci + CMP_LEN, sj + SEL_LEN) - jnp.maximum(ci, sj), 0)
    overlap = overlap.astype(jnp.float32) / CMP_LEN
    imp = jnp.einsum('bhgtc,cj->bhtj', p_c, overlap)
    cur = pos // SEL_LEN
    jj = jnp.arange(n_sel)[None, :]
    forced = (jj == 0) | (jj == cur[:, None]) | (jj == cur[:, None] - 1)
    causal_blk = jj <= cur[:, None]
    score = jnp.where(forced, SEL_FORCE, jnp.where(causal_blk, imp, -SEL_FORCE))
    _, sel_idx = lax.top_k(score, top_k)

    ks_blk = k_s.reshape(B, n_sel, SEL_LEN, Hkv, d).transpose(0, 3, 1, 2, 4)
    vs_blk = v_s.reshape(B, n_sel, SEL_LEN, Hkv, d).transpose(0, 3, 1, 2, 4)
    tb_h = table.reshape(NUM_BUCKETS, Hkv, G).transpose(1, 0, 2)
    bi = jnp.arange(B)[:, None, None, None]
    hi = jnp.arange(Hkv)[None, :, None, None]
    C = SEL_QCHUNK

    def one_chunk(c):
        qc = lax.dynamic_slice_in_dim(q, c * C, C, axis=1)
        ic = lax.dynamic_slice_in_dim(sel_idx, c * C, C, axis=2)
        pc = lax.dynamic_slice_in_dim(pos, c * C, C, axis=0)
        kg = ks_blk[bi, hi, ic]
        vg = vs_blk[bi, hi, ic]
        s = jnp.einsum('bchgd,bhcnpd->bhgcnp', qc, kg).astype(jnp.float32) * scale
        dist = pc[:, None, None] - (ic[..., None] * SEL_LEN + jnp.arange(SEL_LEN))
        bias = tb_h[hi[..., None], t5_bucket(dist)].astype(jnp.float32)
        s = s + jnp.moveaxis(bias, -1, 2)
        s = jnp.where((dist >= 0)[:, :, None], s, NEG_INF)
        p = jax.nn.softmax(s.reshape(B, Hkv, G, C, -1), axis=-1).reshape(s.shape)
        return jnp.einsum('bhgcnp,bhcnpd->bchgd', p.astype(vg.dtype), vg)

    o_sel = lax.map(one_chunk, jnp.arange(S // C))
    o_sel = o_sel.transpose(1, 0, 2, 3, 4, 5).reshape(B, S, Hkv, G, d)

    o_win = banded_attention(q, k_w, v_w, NSA_WINDOW, table)

    g = jax.nn.sigmoid(gate.astype(jnp.float32)).reshape(B, S, Hkv, G, 3).astype(q.dtype)
    o = g[..., 0:1] * o_cmp + g[..., 1:2] * o_sel + g[..., 2:3] * o_win
    return o.reshape(B, S, Hkv * G * d)


def mla_attention(q_c, kv_c, k_r, pos, q_norm, w_qb, kv_norm, w_kvb):
    B, S, _ = q_c.shape
    H = MLA_HEADS
    q = (rmsnorm(q_c, q_norm) @ w_qb).reshape(B, S, H, MLA_NOPE + MLA_ROPE)
    q_nope = q[..., :MLA_NOPE]
    q_rope = rope(q[..., MLA_NOPE:], pos)
    kv = (rmsnorm(kv_c, kv_norm) @ w_kvb).reshape(B, S, H, MLA_NOPE + MLA_V)
    k_nope, v = kv[..., :MLA_NOPE], kv[..., MLA_NOPE:]
    k_rope = rope(k_r[:, :, None, :], pos)[:, :, 0]
    scale = (MLA_NOPE + MLA_ROPE) ** -0.5
    QB = MLA_QBLOCK

    def one_block(n):
        qn = lax.dynamic_slice_in_dim(q_nope, n * QB, QB, axis=1)
        qr = lax.dynamic_slice_in_dim(q_rope, n * QB, QB, axis=1)
        pq = lax.dynamic_slice_in_dim(pos, n * QB, QB, axis=0)
        s = (jnp.einsum('bqhd,bkhd->bhqk', qn, k_nope)
             + jnp.einsum('bqhd,bkd->bhqk', qr, k_rope)).astype(jnp.float32) * scale
        s = jnp.where(pq[:, None] >= pos[None, :], s, NEG_INF)
        p = jax.nn.softmax(s, axis=-1)
        return jnp.einsum('bhqk,bkhd->bqhd', p.astype(v.dtype), v)

    o = lax.map(one_block, jnp.arange(S // QB))
    return o.transpose(1, 0, 2, 3, 4).reshape(B, S, H * MLA_V)


def token_mixing(h, pos, table, w_in, nsa_pe_k, nsa_w1_k, nsa_w2_k, nsa_pe_v, nsa_w1_v, nsa_w2_v,
                 swa_sinks, mla_q_norm, mla_w_qb, mla_kv_norm, mla_w_kvb, w_branch, w_out):
    B, S, _ = h.shape
    (nq, nkc, nvc, nks, nvs, nkw, nvw, ngate, sq, sk, sv,
     mq, mkv, mkr, mgate) = split_cols(h @ w_in)
    kvs_a = (B, S, NSA_KV, HEAD_DIM)
    o_a = nsa_attention(nq.reshape(B, S, NSA_KV, NSA_GROUP, HEAD_DIM),
                        nkc.reshape(kvs_a), nvc.reshape(kvs_a), nks.reshape(kvs_a),
                        nvs.reshape(kvs_a), nkw.reshape(kvs_a), nvw.reshape(kvs_a),
                        ngate, pos, table[:, :NSA_HEADS],
                        nsa_pe_k, nsa_w1_k, nsa_w2_k, nsa_pe_v, nsa_w1_v, nsa_w2_v)
    kvs_b = (B, S, SWA_KV, HEAD_DIM)
    o_b = banded_attention(sq.reshape(B, S, SWA_KV, SWA_GROUP, HEAD_DIM),
                           sk.reshape(kvs_b), sv.reshape(kvs_b), SWA_WINDOW,
                           table[:, NSA_HEADS:], swa_sinks).reshape(B, S, BRANCH_WIDTH)
    o_c = mla_attention(mq, mkv, mkr, pos, mla_q_norm, mla_w_qb, mla_kv_norm, mla_w_kvb)
    branches = jnp.stack([o_a, o_b, o_c], axis=2)
    y = jnp.einsum('bsnc,ncd->bsnd', branches, w_branch)
    gates = jax.nn.sigmoid(mgate.astype(jnp.float32)).reshape(B, S, N_BRANCH, D_MODEL).astype(y.dtype)
    return jnp.sum(gates * y, axis=2) @ w_out


def setup_inputs(seed: int = 0) -> dict:
    key = jax.random.key(seed)
    ks = jax.random.split(key, 26)
    f32 = jnp.float32
    L = DEPTH

    def dense(k, shape, fan_in):
        return jax.random.normal(k, shape, f32) * fan_in ** -0.5

    def gain(k, shape):
        return 1.0 + 0.05 * jax.random.normal(k, shape, f32)

    return {
        "x": jax.random.normal(ks[0], (BATCH, SEQ, D_MODEL), f32),
        "rel_bias_table": 0.2 * jax.random.normal(ks[1], (NUM_BUCKETS, N_BIAS_HEADS), f32),
        "ffn1_norm_pre": gain(ks[2], (L, D_MODEL)),
        "ffn1_w_gu": dense(ks[3], (L, D_MODEL, 2 * D_FF), D_MODEL),
        "ffn1_w_down": dense(ks[4], (L, D_FF, D_MODEL), D_FF),
        "ffn1_norm_post": gain(ks[5], (L, D_MODEL)),
        "mix_norm_pre": gain(ks[6], (L, D_MODEL)),
        "w_in": dense(ks[7], (L, D_MODEL, IN_WIDTH), D_MODEL),
        "nsa_pe_k": 0.1 * jax.random.normal(ks[8], (L, CMP_LEN, HEAD_DIM), f32),
        "nsa_w1_k": dense(ks[9], (L, CMP_LEN * HEAD_DIM, CMP_HIDDEN), CMP_LEN * HEAD_DIM),
        "nsa_w2_k": dense(ks[10], (L, CMP_HIDDEN, HEAD_DIM), CMP_HIDDEN),
        "nsa_pe_v": 0.1 * jax.random.normal(ks[11], (L, CMP_LEN, HEAD_DIM), f32),
        "nsa_w1_v": dense(ks[12], (L, CMP_LEN * HEAD_DIM, CMP_HIDDEN), CMP_LEN * HEAD_DIM),
        "nsa_w2_v": dense(ks[13], (L, CMP_HIDDEN, HEAD_DIM), CMP_HIDDEN),
        "swa_sinks": jax.random.normal(ks[14], (L, SWA_HEADS), f32),
        "mla_q_norm": gain(ks[15], (L, MLA_Q_RANK)),
        "mla_w_qb": dense(ks[16], (L, MLA_Q_RANK, MLA_HEADS * (MLA_NOPE + MLA_ROPE)), MLA_Q_RANK),
        "mla_kv_norm": gain(ks[17], (L, MLA_KV_RANK)),
        "mla_w_kvb": dense(ks[18], (L, MLA_KV_RANK, MLA_HEADS * (MLA_NOPE + MLA_V)), MLA_KV_RANK),
        "w_branch": dense(ks[19], (L, N_BRANCH, BRANCH_WIDTH, D_MODEL), BRANCH_WIDTH),
        "w_out": dense(ks[20], (L, D_MODEL, D_MODEL), D_MODEL),
        "mix_norm_post": gain(ks[21], (L, D_MODEL)),
        "ffn2_norm_pre": gain(ks[22], (L, D_MODEL)),
        "ffn2_w_gu": dense(ks[23], (L, D_MODEL, 2 * D_FF), D_MODEL),
        "ffn2_w_down": dense(ks[24], (L, D_FF, D_MODEL), D_FF),
        "ffn2_norm_post": gain(ks[25], (L, D_MODEL)),
    }


def reference(x, rel_bias_table, ffn1_norm_pre, ffn1_w_gu, ffn1_w_down, ffn1_norm_post,
              mix_norm_pre, w_in, nsa_pe_k, nsa_w1_k, nsa_w2_k, nsa_pe_v, nsa_w1_v, nsa_w2_v,
              swa_sinks, mla_q_norm, mla_w_qb, mla_kv_norm, mla_w_kvb, w_branch, w_out,
              mix_norm_post, ffn2_norm_pre, ffn2_w_gu, ffn2_w_down, ffn2_norm_post):
    pos = jnp.arange(x.shape[1], dtype=jnp.int32)
    for l in range(DEPTH):
        x = macaron_ffn(x, ffn1_norm_pre[l], ffn1_w_gu[l], ffn1_w_down[l], ffn1_norm_post[l])
        h = rmsnorm(x, mix_norm_pre[l])
        m = token_mixing(h, pos, rel_bias_table, w_in[l],
                         nsa_pe_k[l], nsa_w1_k[l], nsa_w2_k[l], nsa_pe_v[l], nsa_w1_v[l], nsa_w2_v[l],
                         swa_sinks[l], mla_q_norm[l], mla_w_qb[l], mla_kv_norm[l], mla_w_kvb[l],
                         w_branch[l], w_out[l])
        x = x + rmsnorm(m, mix_norm_post[l])
        x = macaron_ffn(x, ffn2_norm_pre[l], ffn2_w_gu[l], ffn2_w_down[l], ffn2_norm_post[l])
    return x
```

```python
import functools
import math

import jax
import jax.numpy as jnp
import numpy as np
from jax import lax
from jax.experimental import pallas as pl
from jax.experimental.pallas import tpu as pltpu

F32 = jnp.float32
BF16 = jnp.bfloat16

D_MODEL = 1024
D_FF = 2816
HEAD_DIM = 64
NORM_EPS = 1e-6
NUM_BUCKETS = 32
MAX_DISTANCE = 1024
N_HEADS = 8
N_KV = 2
GROUP = N_HEADS // N_KV
CMP_LEN = 32
CMP_STRIDE = 16
CMP_HIDDEN = 128
SEL_LEN = 64
SEL_SHIFT = 6
SEL_TOPK = 16
NSA_WINDOW = 512
SWA_WINDOW = 128
MLA_Q_RANK = 384
MLA_KV_RANK = 256
MLA_NOPE = 64
MLA_ROPE = 32
MLA_V = 64
MLA_QK = MLA_NOPE + MLA_ROPE
ROPE_BASE = 10000.0
BRANCH_WIDTH = N_HEADS * HEAD_DIM
NEG_INF = -1e30
SEL_FORCE = 1e9

LANE = 128
VMEM_LIMIT = 56 * 1024 * 1024

ZB_MQ = 0
ZB_NGATE = 3
ZB_MKV = 4
ZB_KR1 = 6
ZB_KR2 = 7
ZB_NQ = 8
ZB_NKV = 12
ZB_SQ = 18
ZB_SK = 22
ZB_SV = 23
ZB_MGATE = 24
Z_BLOCKS = 48
Z_WIDTH = Z_BLOCKS * LANE


def _cparams(sem):
    return pltpu.CompilerParams(dimension_semantics=sem, vmem_limit_bytes=VMEM_LIMIT)


def _rms(x, g):
    inv = lax.rsqrt(jnp.mean(x * x, axis=-1, keepdims=True) + NORM_EPS)
    return (x * inv) * g


def _dot(a, b):
    return jnp.dot(a, b, preferred_element_type=F32)


def _dot_nt(a, b, precision=None):
    return lax.dot_general(a, b, (((1,), (1,)), ((), ())), precision=precision,
                           preferred_element_type=F32)


def _ffn_kernel(x_ref, gpre_ref, wg_ref, wu_ref, wd_ref, gpost_ref, o_ref, h_sc, acc_sc):
    j = pl.program_id(1)

    @pl.when(j == 0)
    def _():
        h_sc[...] = _rms(x_ref[...], gpre_ref[...]).astype(BF16)
        acc_sc[...] = jnp.zeros_like(acc_sc)

    h = h_sc[...]
    g = _dot(h, wg_ref[...])
    u = _dot(h, wu_ref[...])
    act = (g * jax.nn.sigmoid(g)) * u
    acc_sc[...] += _dot(act.astype(BF16), wd_ref[...])

    @pl.when(j == pl.num_programs(1) - 1)
    def _():
        o_ref[...] = x_ref[...] + 0.5 * _rms(acc_sc[...], gpost_ref[...])


def _ffn(x, g_pre, w_gu, w_down, g_post, *, tm, nf):
    m, d = x.shape
    tf = D_FF // nf
    return pl.pallas_call(
        _ffn_kernel,
        grid=(m // tm, nf),
        in_specs=[
            pl.BlockSpec((tm, d), lambda i, j: (i, 0)),
            pl.BlockSpec((1, d), lambda i, j: (0, 0)),
            pl.BlockSpec((d, tf), lambda i, j: (0, j)),
            pl.BlockSpec((d, tf), lambda i, j: (0, j + nf)),
            pl.BlockSpec((tf, d), lambda i, j: (j, 0)),
            pl.BlockSpec((1, d), lambda i, j: (0, 0)),
        ],
        out_specs=pl.BlockSpec((tm, d), lambda i, j: (i, 0)),
        out_shape=jax.ShapeDtypeStruct((m, d), F32),
        scratch_shapes=[pltpu.VMEM((tm, d), BF16), pltpu.VMEM((tm, d), F32)],
        compiler_params=_cparams(("parallel", "arbitrary")),
        name="ffn",
    )(x, g_pre, w_gu, w_gu, w_down, g_post)


def _inproj_kernel(x_ref, g_ref, w_ref, o_ref, h_sc):
    @pl.when(pl.program_id(1) == 0)
    def _():
        h_sc[...] = _rms(x_ref[...], g_ref[...]).astype(BF16)

    o_ref[...] = _dot(h_sc[...], w_ref[...]).astype(BF16)


def _inproj(x, g, w, *, tm, tn):
    m, d = x.shape
    n = w.shape[1]
    return pl.pallas_call(
        _inproj_kernel,
        grid=(m // tm, n // tn),
        in_specs=[
            pl.BlockSpec((tm, d), lambda i, j: (i, 0)),
            pl.BlockSpec((1, d), lambda i, j: (0, 0)),
            pl.BlockSpec((d, tn), lambda i, j: (0, j)),
        ],
        out_specs=pl.BlockSpec((tm, tn), lambda i, j: (i, j)),
        out_shape=jax.ShapeDtypeStruct((m, n), BF16),
        scratch_shapes=[pltpu.VMEM((tm, d), BF16)],
        compiler_params=_cparams(("parallel", "arbitrary")),
        name="inproj",
    )(x, g, w)


def _cmp_mlp_kernel(k_ref, v_ref, pek_ref, w1k_ref, w2k_ref, pev_ref, w1v_ref, w2v_ref,
                    kc_ref, vc_ref):
    half = CMP_STRIDE * HEAD_DIM

    def one(x_ref, pe_ref, w1_ref, w2_ref, o_ref):
        x = x_ref[0, 0].astype(F32)
        ncp = x.shape[0]
        xa = (x + pe_ref[0:1, :]).astype(BF16)
        xb = (x + pe_ref[1:2, :]).astype(BF16)
        pa = _dot(xa, w1_ref[0:half, :])
        pb = _dot(xb, w1_ref[half:2 * half, :])
        hid = pa + pltpu.roll(pb, ncp - 1, 0)
        out = _dot(jax.nn.gelu(hid).astype(BF16), w2_ref[...])
        row = lax.broadcasted_iota(jnp.int32, out.shape, 0)
        o_ref[0, 0] = jnp.where(row < ncp - 1, out, 0.0).astype(BF16)

    one(k_ref, pek_ref, w1k_ref, w2k_ref, kc_ref)
    one(v_ref, pev_ref, w1v_ref, w2v_ref, vc_ref)


def _cmp_mlp(k4, v4, pek, w1k, w2k, pev, w1v, w2v):
    b, hkv, ncp, width = k4.shape
    blk = pl.BlockSpec((1, 1, ncp, width), lambda i, j: (i, j, 0, 0))
    oblk = pl.BlockSpec((1, 1, ncp, HEAD_DIM), lambda i, j: (i, j, 0, 0))

    def full(a):
        return pl.BlockSpec(a.shape, lambda i, j: (0,) * a.ndim)

    out = jax.ShapeDtypeStruct((b, hkv, ncp, HEAD_DIM), BF16)
    return pl.pallas_call(
        _cmp_mlp_kernel,
        grid=(b, hkv),
        in_specs=[blk, blk, full(pek), full(w1k), full(w2k), full(pev), full(w1v), full(w2v)],
        out_specs=[oblk, oblk],
        out_shape=[out, out],
        compiler_params=_cparams(("parallel", "parallel")),
        name="cmp_mlp",
    )(k4, v4, pek, w1k, w2k, pev, w1v, w2v)


def _cmp_attn_kernel(q_ref, kc_ref, vc_ref, bias_ref, ovl_ref, o_ref, sel_ref, *, tq, ncp, nsel):
    qi = pl.program_id(1)
    kc = kc_ref[0, 0]
    vc = vc_ref[0, 0]
    row = lax.broadcasted_iota(jnp.int32, (tq, ncp), 0) + qi * tq
    col = lax.broadcasted_iota(jnp.int32, (tq, ncp), 1)
    valid = (col * CMP_STRIDE + (CMP_LEN - 1)) <= row
    p_sum = jnp.zeros((tq, ncp), F32)
    for g in range(GROUP):
        q = q_ref[0, 0, g] * (HEAD_DIM ** -0.5)
        s = _dot_nt(q, kc) + bias_ref[0, g]
        s = jnp.where(valid, s, NEG_INF)
        e = jnp.where(valid, jnp.exp(s - jnp.max(s, axis=-1, keepdims=True)), 0.0)
        den = jnp.sum(e, axis=-1, keepdims=True)
        p = e / jnp.where(den > 0, den, 1.0)
        o_ref[0, 0, g] = _dot(p.astype(BF16), vc).astype(BF16)
        p_sum = p_sum + p

    imp = _dot_nt(ovl_ref[...], p_sum, precision=lax.Precision.HIGHEST)
    jj = lax.broadcasted_iota(jnp.int32, (nsel, tq), 0)
    cur = (lax.broadcasted_iota(jnp.int32, (nsel, tq), 1) + qi * tq) >> SEL_SHIFT
    forced = (jj == 0) | (jj == cur) | (jj == cur - 1)
    score = jnp.where(forced, SEL_FORCE, jnp.where(jj <= cur, imp, -SEL_FORCE))
    rank = jnp.zeros((nsel, tq), F32)
    for i in range(nsel):
        si = score[i:i + 1, :]
        ge = jnp.where(si >= score, 1.0, 0.0)
        gt = jnp.where(si > score, 1.0, 0.0)
        rank = rank + jnp.where(jj > i, ge, gt)
    chosen = jnp.where(rank < min(SEL_TOPK, nsel), 1.0, 0.0).astype(BF16)
    eye = (lax.broadcasted_iota(jnp.int32, (tq, tq), 0)
           == lax.broadcasted_iota(jnp.int32, (tq, tq), 1))
    sel_ref[0, 0] = _dot_nt(jnp.where(eye, 1.0, 0.0).astype(BF16), chosen).astype(BF16)


def _cmp_attn(qn, kc, vc, bias_c, ovl_t, *, tq):
    b, hkv, g, s, d = qn.shape
    ncp = kc.shape[2]
    nsel = ovl_t.shape[0]
    kern = functools.partial(_cmp_attn_kernel, tq=tq, ncp=ncp, nsel=nsel)
    return pl.pallas_call(
        kern,
        grid=(hkv, s // tq, b),
        in_specs=[
            pl.BlockSpec((1, 1, g, tq, d), lambda h, i, bb: (bb, h, 0, i, 0)),
            pl.BlockSpec((1, 1, ncp, d), lambda h, i, bb: (bb, h, 0, 0)),
            pl.BlockSpec((1, 1, ncp, d), lambda h, i, bb: (bb, h, 0, 0)),
            pl.BlockSpec((1, g, tq, ncp), lambda h, i, bb: (h, 0, i, 0)),
            pl.BlockSpec((nsel, ncp), lambda h, i, bb: (0, 0)),
        ],
        out_specs=[
            pl.BlockSpec((1, 1, g, tq, d), lambda h, i, bb: (bb, h, 0, i, 0)),
            pl.BlockSpec((1, 1, tq, nsel), lambda h, i, bb: (bb, h, i, 0)),
        ],
        out_shape=[
            jax.ShapeDtypeStruct((b, hkv, g, s, d), BF16),
            jax.ShapeDtypeStruct((b, hkv, s, nsel), BF16),
        ],
        compiler_params=_cparams(("parallel", "parallel", "parallel")),
        name="cmp_attn",
    )(qn, kc, vc, bias_c, ovl_t)


def _attn_kernel(*refs, g, tq, tk, n_bias, use_sel, window, has_sink, q_scale):
    it = iter(refs)
    q_ref, k_ref, v_ref = next(it), next(it), next(it)
    bias_ref = next(it) if n_bias else None
    sel_ref = next(it) if use_sel else None
    sink_ref = next(it) if has_sink else None
    o_ref = next(it)
    m_sc, l_sc, acc_sc = next(it), next(it), next(it)

    qi = pl.program_id(2)
    for h in range(g):
        if has_sink:
            m_sc[h] = jnp.broadcast_to(sink_ref[0, h], (tq, 1))
            l_sc[h] = jnp.ones((tq, 1), F32)
        else:
            m_sc[h] = jnp.full((tq, 1), NEG_INF, F32)
            l_sc[h] = jnp.zeros((tq, 1), F32)
        acc_sc[h] = jnp.zeros(acc_sc.shape[1:], F32)

    hi = (qi + 1) * (tq // tk)
    if window is None:
        lo = 0
    else:
        lo = jnp.maximum(qi * tq - (window - 1), 0) // tk
    row = lax.broadcasted_iota(jnp.int32, (tq, tk), 0)
    col = lax.broadcasted_iota(jnp.int32, (tq, tk), 1)
    rel = row - col

    def body(step, carry):
        kj = hi - 1 - step
        start = pl.multiple_of(kj * tk, tk)
        k = k_ref[0, 0, pl.ds(start, tk), :]
        v = v_ref[0, 0, pl.ds(start, tk), :]
        dist = rel + (qi * tq - kj * tk)
        valid = dist >= 0
        if window is not None:
            valid = valid & (dist < window)
        if use_sel:
            nsel = sel_ref.shape[-1]
            jb = lax.broadcasted_iota(jnp.int32, (nsel, tk), 0)
            cb = (lax.broadcasted_iota(jnp.int32, (nsel, tk), 1) >> SEL_SHIFT) + kj * (tk // SEL_LEN)
            expand = jnp.where(jb == cb, 1.0, 0.0).astype(BF16)
            valid = valid & (_dot(sel_ref[0, 0], expand) > 0.5)
        for h in range(g):
            q = q_ref[0, 0, h]
            if q_scale is not None:
                q = q * q_scale
            s = _dot_nt(q, k)
            if n_bias:
                s = s + bias_ref[0, h, jnp.minimum(step, n_bias - 1)]
            s = jnp.where(valid, s, NEG_INF)
            m_old = m_sc[h]
            m_new = jnp.maximum(m_old, jnp.max(s, axis=-1, keepdims=True))
            p = jnp.exp(s - m_new)
            alpha = jnp.exp(m_old - m_new)
            l_sc[h] = alpha * l_sc[h] + jnp.sum(p, axis=-1, keepdims=True)
            acc_sc[h] = alpha * acc_sc[h] + _dot(p.astype(BF16), v)
            m_sc[h] = m_new
        return carry

    lax.fori_loop(0, hi - lo, body, 0)
    for h in range(g):
        o_ref[0, 0, h] = (acc_sc[h] / l_sc[h]).astype(BF16)


def _attention(q, k, v, *, tq, tk, bias=None, sel=None, sinks=None, window=None, q_scale=None):
    b, hk, g, s, dq = q.shape
    dv = v.shape[-1]
    n_bias = 0 if bias is None else bias.shape[2]
    if n_bias:
        assert tq == tk
    kern = functools.partial(_attn_kernel, g=g, tq=tq, tk=tk, n_bias=n_bias,
                             use_sel=sel is not None, window=window,
                             has_sink=sinks is not None, q_scale=q_scale)
    in_specs = [
        pl.BlockSpec((1, 1, g, tq, dq), lambda h, bb, i: (bb, h, 0, i, 0)),
        pl.BlockSpec((1, 1, s, dq), lambda h, bb, i: (bb, h, 0, 0)),
        pl.BlockSpec((1, 1, s, dv), lambda h, bb, i: (bb, h, 0, 0)),
    ]
    args = [q, k, v]
    if n_bias:
        in_specs.append(pl.BlockSpec((1, g, n_bias, tq, tk), lambda h, bb, i: (h, 0, 0, 0, 0)))
        args.append(bias)
    if sel is not None:
        in_specs.append(pl.BlockSpec((1, 1, tq, sel.shape[-1]), lambda h, bb, i: (bb, h, i, 0)))
        args.append(sel)
    if sinks is not None:
        in_specs.append(pl.BlockSpec((1, g, 1, 1), lambda h, bb, i: (h, 0, 0, 0)))
        args.append(sinks)
    return pl.pallas_call(
        kern,
        grid=(hk, b, s // tq),
        in_specs=in_specs,
        out_specs=pl.BlockSpec((1, 1, g, tq, dv), lambda h, bb, i: (bb, h, 0, i, 0)),
        out_shape=jax.ShapeDtypeStruct((b, hk, g, s, dv), BF16),
        scratch_shapes=[pltpu.VMEM((g, tq, 1), F32), pltpu.VMEM((g, tq, 1), F32),
                        pltpu.VMEM((g, tq, dv), F32)],
        compiler_params=_cparams(("parallel", "parallel", "arbitrary")),
        name="attn",
    )(*args)


def _mla_proj_kernel(mq_ref, mkv_ref, kr1_ref, kr2_ref, cos_ref, sin_ref, qn_ref, wq_ref,
                     kvn_ref, wkv_ref, oq_ref, oqr_ref, okv_ref, okr_ref):
    cos = cos_ref[...]
    sin = sin_ref[...]
    nope = N_HEADS * MLA_NOPE
    ql = _rms(mq_ref[...].astype(F32), qn_ref[...]).astype(BF16)
    q = _dot(ql, wq_ref[...]) * (MLA_QK ** -0.5)
    oq_ref[...] = q[:, :nope].astype(BF16)
    r1 = q[:, nope:nope + LANE]
    r2 = q[:, nope + LANE:nope + 2 * LANE]
    oqr_ref[:, 0:LANE] = (r1 * cos - r2 * sin).astype(BF16)
    oqr_ref[:, LANE:2 * LANE] = (r1 * sin + r2 * cos).astype(BF16)
    kvl = _rms(mkv_ref[...].astype(F32), kvn_ref[...]).astype(BF16)
    okv_ref[...] = _dot(kvl, wkv_ref[...]).astype(BF16)
    k1 = kr1_ref[...].astype(F32)
    k2 = kr2_ref[...].astype(F32)
    okr_ref[:, 0:LANE] = (k1 * cos - k2 * sin).astype(BF16)
    okr_ref[:, LANE:2 * LANE] = (k1 * sin + k2 * cos).astype(BF16)


def _mla_proj(z, cos, sin, q_norm, w_qb, kv_norm, w_kvb, *, tm, seq):
    m = z.shape[0]
    nt = seq // tm
    nope = N_HEADS * MLA_NOPE

    def full(a):
        return pl.BlockSpec(a.shape, lambda i: (0,) * a.ndim)

    return pl.pallas_call(
        _mla_proj_kernel,
        grid=(m // tm,),
        in_specs=[
            pl.BlockSpec((tm, MLA_Q_RANK), lambda i: (i, ZB_MQ * LANE // MLA_Q_RANK)),
            pl.BlockSpec((tm, MLA_KV_RANK), lambda i: (i, ZB_MKV * LANE // MLA_KV_RANK)),
            pl.BlockSpec((tm, LANE), lambda i: (i, ZB_KR1)),
            pl.BlockSpec((tm, LANE), lambda i: (i, ZB_KR2)),
            pl.BlockSpec((tm, LANE), lambda i: (i % nt, 0)),
            pl.BlockSpec((tm, LANE), lambda i: (i % nt, 0)),
            full(q_norm), full(w_qb), full(kv_norm), full(w_kvb),
        ],
        out_specs=[
            pl.BlockSpec((tm, nope), lambda i: (i, 0)),
            pl.BlockSpec((tm, 2 * LANE), lambda i: (i, 0)),
            pl.BlockSpec((tm, 2 * nope), lambda i: (i, 0)),
            pl.BlockSpec((tm, 2 * LANE), lambda i: (i, 0)),
        ],
        out_shape=[
            jax.ShapeDtypeStruct((m, nope), BF16),
            jax.ShapeDtypeStruct((m, 2 * LANE), BF16),
            jax.ShapeDtypeStruct((m, 2 * nope), BF16),
            jax.ShapeDtypeStruct((m, 2 * LANE), BF16),
        ],
        compiler_params=_cparams(("parallel",)),
        name="mla_proj",
    )(z, z, z, z, cos, sin, q_norm, w_qb, kv_norm, w_kvb)


def _merge_kernel(ocmp_ref, osel_ref, owin_ref, ob_ref, oc_ref, ng_ref, mg_ref, x_ref,
                  gexp_ref, wb_ref, wo_ref, gpost_ref, o_ref):
    ng = jax.nn.sigmoid(ng_ref[...].astype(F32))
    o_a = jnp.zeros(ocmp_ref.shape, F32)
    for n, ref in enumerate((ocmp_ref, osel_ref, owin_ref)):
        gate = jnp.dot(ng, gexp_ref[n], precision=lax.Precision.HIGHEST,
                       preferred_element_type=F32)
        o_a = o_a + gate * ref[...].astype(F32)
    mixed = jnp.zeros(x_ref.shape, F32)
    for n, br in enumerate((o_a.astype(BF16), ob_ref[...], oc_ref[...])):
        y = _dot(br, wb_ref[n])
        gate = jax.nn.sigmoid(mg_ref[:, n * D_MODEL:(n + 1) * D_MODEL].astype(F32))
        mixed = mixed + gate * y
    out = _dot(mixed.astype(BF16), wo_ref[...])
    o_ref[...] = x_ref[...] + _rms(out, gpost_ref[...])


def _merge(o_cmp, o_sel, o_win, o_b, o_c, z, x, gexp, w_branch, w_out, g_post, *, tm):
    m, d = x.shape
    bw = o_cmp.shape[1]
    br = pl.BlockSpec((tm, bw), lambda i: (i, 0))

    def full(a):
        return pl.BlockSpec(a.shape, lambda i: (0,) * a.ndim)

    return pl.pallas_call(
        _merge_kernel,
        grid=(m // tm,),
        in_specs=[
            br, br, br, br, br,
            pl.BlockSpec((tm, LANE), lambda i: (i, ZB_NGATE)),
            pl.BlockSpec((tm, 3 * d), lambda i: (i, ZB_MGATE * LANE // (3 * d))),
            pl.BlockSpec((tm, d), lambda i: (i, 0)),
            full(gexp), full(w_branch), full(w_out), full(g_post),
        ],
        out_specs=pl.BlockSpec((tm, d), lambda i: (i, 0)),
        out_shape=jax.ShapeDtypeStruct((m, d), F32),
        compiler_params=_cparams(("parallel",)),
        name="merge",
    )(o_cmp, o_sel, o_win, o_b, o_c, z, z, x, gexp, w_branch, w_out, g_post)


def _t5_bucket(dist):
    max_exact = NUM_BUCKETS // 2
    d = jnp.maximum(dist, 0)
    df = jnp.maximum(d, 1).astype(F32)
    large = max_exact + (jnp.log(df / max_exact) / math.log(MAX_DISTANCE / max_exact)
                         * (NUM_BUCKETS - max_exact)).astype(jnp.int32)
    large = jnp.minimum(large, NUM_BUCKETS - 1)
    return jnp.where(d < max_exact, d, large)


def _bias_tiles(table, n_tiles, t):
    i = jnp.arange(t)[:, None]
    j = jnp.arange(t)[None, :]
    dist = jnp.arange(n_tiles)[:, None, None] * t + (i - j)[None]
    return jnp.moveaxis(table[_t5_bucket(dist)].astype(F32), -1, 0)


def _in_proj_weight(w_in):
    sizes = (512, 128, 128, 128, 128, 128, 128, 24, 512, 128, 128, MLA_Q_RANK, MLA_KV_RANK,
             MLA_ROPE, 3 * D_MODEL)
    offs = np.concatenate([[0], np.cumsum(sizes)])
    seg = [w_in[:, offs[i]:offs[i + 1]] for i in range(len(sizes))]
    (nq, nkc, nvc, nks, nvs, nkw, nvw, ngate, sq, sk, sv, mq, mkv, mkr, mgate) = seg
    d = w_in.shape[0]

    def pad(a):
        return jnp.pad(a, ((0, 0), (0, LANE - a.shape[1])))

    half = MLA_ROPE // 2
    cols = [mq, pad(ngate), mkv, pad(mkr[:, :half]), pad(mkr[:, half:]), nq,
            nkc, nvc, nks, nvs, nkw, nvw, sq, sk, sv, mgate]
    w = jnp.concatenate(cols, axis=1)
    assert w.shape == (d, Z_WIDTH)
    return w.astype(BF16)


def _mla_weights(w_qb, w_kvb):
    half = MLA_ROPE // 2
    wq = w_qb.reshape(MLA_Q_RANK, N_HEADS, MLA_QK)
    wq = jnp.concatenate([
        wq[:, :, :MLA_NOPE].reshape(MLA_Q_RANK, -1),
        wq[:, :, MLA_NOPE:MLA_NOPE + half].reshape(MLA_Q_RANK, -1),
        wq[:, :, MLA_NOPE + half:].reshape(MLA_Q_RANK, -1)], axis=1)
    wkv = w_kvb.reshape(MLA_KV_RANK, N_HEADS, MLA_NOPE + MLA_V)
    wkv = jnp.concatenate([wkv[:, :, :MLA_NOPE].reshape(MLA_KV_RANK, -1),
                           wkv[:, :, MLA_NOPE:].reshape(MLA_KV_RANK, -1)], axis=1)
    return wq.astype(BF16), wkv.astype(BF16)


def _gate_expand():
    e = np.zeros((3, LANE, BRANCH_WIDTH), np.float32)
    for n in range(3):
        for h in range(N_HEADS):
            e[n, h * 3 + n, h * HEAD_DIM:(h + 1) * HEAD_DIM] = 1.0
    return jnp.asarray(e)


def _heads(a, b, s, nh):
    return a.reshape(b, s, nh, -1).transpose(0, 2, 1, 3)


def _unheads(a):
    b, hk, g, s, d = a.shape
    return a.transpose(0, 3, 1, 2, 4).reshape(b * s, hk * g * d)


def kernel(x, rel_bias_table, ffn1_norm_pre, ffn1_w_gu, ffn1_w_down, ffn1_norm_post,
           mix_norm_pre, w_in, nsa_pe_k, nsa_w1_k, nsa_w2_k, nsa_pe_v, nsa_w1_v, nsa_w2_v,
           swa_sinks, mla_q_norm, mla_w_qb, mla_kv_norm, mla_w_kvb, w_branch, w_out,
           mix_norm_post, ffn2_norm_pre, ffn2_w_gu, ffn2_w_down, ffn2_norm_post):
    b, s, d = x.shape
    depth = w_in.shape[0]
    m = b * s
    tq = 256
    tq_mla = min(512, s)
    ncp = s // CMP_STRIDE
    nsel = s // SEL_LEN
    tm = min(512, m)
    tm_big = min(1024, s)

    pos = jnp.arange(s, dtype=jnp.int32)
    tab_a = rel_bias_table[:, :N_HEADS]
    tab_b = rel_bias_table[:, N_HEADS:]
    far = -(-790 // tq) + 1
    n_sel_tiles = min(far + 1, s // tq)
    bias_sel = _bias_tiles(tab_a, n_sel_tiles, tq).reshape(N_KV, GROUP, n_sel_tiles, tq, tq)
    n_win_tiles = min((NSA_WINDOW - 1 + tq - 1) // tq + 1, s // tq)
    bias_win = bias_sel[:, :, :n_win_tiles]
    n_swa_tiles = min((SWA_WINDOW - 1 + tq - 1) // tq + 1, s // tq)
    bias_swa = _bias_tiles(tab_b, n_swa_tiles, tq).reshape(N_KV, GROUP, n_swa_tiles, tq, tq)
    cmp_end = jnp.arange(ncp) * CMP_STRIDE + CMP_LEN - 1
    bias_c = jnp.moveaxis(tab_a[_t5_bucket(pos[:, None] - cmp_end[None, :])].astype(F32), -1, 0)
    bias_c = bias_c.reshape(N_KV, GROUP, s, ncp)
    ci = jnp.arange(ncp)[None, :] * CMP_STRIDE
    sj = jnp.arange(nsel)[:, None] * SEL_LEN
    ovl_t = (jnp.maximum(jnp.minimum(ci + CMP_LEN, sj + SEL_LEN) - jnp.maximum(ci, sj), 0)
             .astype(F32) / CMP_LEN)
    half = MLA_ROPE // 2
    inv_freq = ROPE_BASE ** (-jnp.arange(half, dtype=F32) / half)
    ang = pos.astype(F32)[:, None] * inv_freq[None, :]
    cos = jnp.tile(jnp.cos(ang), (1, LANE // half))
    sin = jnp.tile(jnp.sin(ang), (1, LANE // half))
    gexp = _gate_expand()

    xf = x.reshape(m, d)
    for l in range(depth):
        xf = _ffn(xf, ffn1_norm_pre[l][None], ffn1_w_gu[l].astype(BF16),
                  ffn1_w_down[l].astype(BF16), ffn1_norm_post[l][None], tm=tm, nf=2)

        z = _inproj(xf, mix_norm_pre[l][None], _in_proj_weight(w_in[l]), tm=tm_big, tn=Z_WIDTH // 4)

        def zcols(blk, n):
            return z[:, blk * LANE:(blk + n) * LANE]

        qn = _heads(zcols(ZB_NQ, 4), b, s, N_HEADS).reshape(b, N_KV, GROUP, s, HEAD_DIM)
        nkv = [_heads(zcols(ZB_NKV + i, 1), b, s, N_KV) for i in range(6)]
        k_c, v_c, k_s, v_s, k_w, v_w = nkv
        chunk = CMP_STRIDE * HEAD_DIM
        kc, vc = _cmp_mlp(k_c.reshape(b, N_KV, ncp, chunk), v_c.reshape(b, N_KV, ncp, chunk),
                          nsa_pe_k[l].reshape(2, chunk), nsa_w1_k[l].astype(BF16),
                          nsa_w2_k[l].astype(BF16),
                          nsa_pe_v[l].reshape(2, chunk), nsa_w1_v[l].astype(BF16),
                          nsa_w2_v[l].astype(BF16))
        o_cmp, sel = _cmp_attn(qn, kc, vc, bias_c, ovl_t, tq=tq)
        o_sel = _attention(qn, k_s, v_s, tq=tq, tk=tq, bias=bias_sel, sel=sel,
                           q_scale=HEAD_DIM ** -0.5)
        o_win = _attention(qn, k_w, v_w, tq=tq, tk=tq, bias=bias_win, window=NSA_WINDOW,
                           q_scale=HEAD_DIM ** -0.5)
        sq = _heads(zcols(ZB_SQ, 4), b, s, N_HEADS).reshape(b, N_KV, GROUP, s, HEAD_DIM)
        sk = _heads(zcols(ZB_SK, 1), b, s, N_KV)
        sv = _heads(zcols(ZB_SV, 1), b, s, N_KV)
        o_b = _attention(sq, sk, sv, tq=tq, tk=tq, bias=bias_swa, window=SWA_WINDOW,
                         sinks=swa_sinks[l].reshape(N_KV, GROUP, 1, 1), q_scale=HEAD_DIM ** -0.5)
        wq, wkv = _mla_weights(mla_w_qb[l], mla_w_kvb[l])
        mq_n, mq_r, mkv_u, mk_r = _mla_proj(z, cos, sin, mla_q_norm[l][None], wq,
                                            mla_kv_norm[l][None], wkv, tm=tm_big, seq=s)
        nope = N_HEADS * MLA_NOPE
        q96 = jnp.concatenate([
            mq_n.reshape(b, s, N_HEADS, MLA_NOPE),
            mq_r[:, :LANE].reshape(b, s, N_HEADS, half),
            mq_r[:, LANE:].reshape(b, s, N_HEADS, half)], axis=-1).transpose(0, 2, 1, 3)
        kr = jnp.concatenate([mk_r[:, :half], mk_r[:, LANE:LANE + half]], axis=-1)
        k96 = jnp.concatenate([
            mkv_u[:, :nope].reshape(b, s, N_HEADS, MLA_NOPE),
            jnp.broadcast_to(kr.reshape(b, s, 1, MLA_ROPE), (b, s, N_HEADS, MLA_ROPE))],
            axis=-1).transpose(0, 2, 1, 3)
        v_m = _heads(mkv_u[:, nope:], b, s, N_HEADS)
        o_c = _attention(q96[:, :, None], k96, v_m, tq=tq_mla, tk=tq_mla)

        xf = _merge(_unheads(o_cmp), _unheads(o_sel), _unheads(o_win), _unheads(o_b),
                    _unheads(o_c), z, xf, gexp, w_branch[l].astype(BF16), w_out[l].astype(BF16),
                    mix_norm_post[l][None], tm=tm)

        xf = _ffn(xf, ffn2_norm_pre[l][None], ffn2_w_gu[l].astype(BF16),
                  ffn2_w_down[l].astype(BF16), ffn2_norm_post[l][None], tm=tm, nf=2)
    return xf.reshape(b, s, d)
```

```python
import functools
import math

import jax
import jax.numpy as jnp
import numpy as np
from jax import lax
from jax.experimental import pallas as pl
from jax.experimental.pallas import tpu as pltpu

F32 = jnp.float32
BF16 = jnp.bfloat16

D_MODEL = 1024
D_FF = 2816
HEAD_DIM = 64
NORM_EPS = 1e-6
NUM_BUCKETS = 32
MAX_DISTANCE = 1024
LAST_BUCKET_DIST = 1 + math.ceil(
    (NUM_BUCKETS // 2) * (MAX_DISTANCE / (NUM_BUCKETS // 2))
    ** ((NUM_BUCKETS // 2 - 1) / (NUM_BUCKETS - NUM_BUCKETS // 2)))
N_HEADS = 8
N_KV = 2
GROUP = N_HEADS // N_KV
CMP_LEN = 32
CMP_STRIDE = 16
CMP_HIDDEN = 128
SEL_LEN = 64
SEL_SHIFT = 6
SEL_TOPK = 16
NSA_WINDOW = 512
SWA_WINDOW = 128
MLA_Q_RANK = 384
MLA_KV_RANK = 256
MLA_NOPE = 64
MLA_ROPE = 32
MLA_V = 64
MLA_QK = MLA_NOPE + MLA_ROPE
ROPE_BASE = 10000.0
BRANCH_WIDTH = N_HEADS * HEAD_DIM
NEG_INF = -1e30
SEL_FORCE = 1e9

LANE = 128
VMEM_LIMIT = 56 * 1024 * 1024

ZB_MQ = 0
ZB_NGATE = 3
ZB_MKV = 4
ZB_KR1 = 6
ZB_KR2 = 7
ZB_NQ = 8
ZB_NKV = 12
ZB_SQ = 18
ZB_SK = 22
ZB_SV = 23
ZB_MGATE = 24
Z_BLOCKS = 48
Z_WIDTH = Z_BLOCKS * LANE


def _cparams(sem):
    return pltpu.CompilerParams(dimension_semantics=sem, vmem_limit_bytes=VMEM_LIMIT)


def _rms(x, g):
    inv = lax.rsqrt(jnp.mean(x * x, axis=-1, keepdims=True) + NORM_EPS)
    return (x * inv) * g


def _dot(a, b):
    return jnp.dot(a, b, preferred_element_type=F32)


def _dot_nt(a, b, precision=None):
    return lax.dot_general(a, b, (((1,), (1,)), ((), ())), precision=precision,
                           preferred_element_type=F32)


def _ffn_kernel(x_ref, gpre_ref, wg_ref, wu_ref, wd_ref, gpost_ref, o_ref, h_sc, acc_sc):
    j = pl.program_id(1)

    @pl.when(j == 0)
    def _():
        h_sc[...] = _rms(x_ref[...], gpre_ref[...]).astype(BF16)
        acc_sc[...] = jnp.zeros_like(acc_sc)

    h = h_sc[...]
    g = _dot(h, wg_ref[...])
    u = _dot(h, wu_ref[...])
    act = (g * jax.nn.sigmoid(g)) * u
    acc_sc[...] += _dot(act.astype(BF16), wd_ref[...])

    @pl.when(j == pl.num_programs(1) - 1)
    def _():
        o_ref[...] = x_ref[...] + 0.5 * _rms(acc_sc[...], gpost_ref[...])


def _ffn(x, g_pre, w_gu, w_down, g_post, *, tm, nf):
    m, d = x.shape
    tf = D_FF // nf
    return pl.pallas_call(
        _ffn_kernel,
        grid=(m // tm, nf),
        in_specs=[
            pl.BlockSpec((tm, d), lambda i, j: (i, 0)),
            pl.BlockSpec((1, d), lambda i, j: (0, 0)),
            pl.BlockSpec((d, tf), lambda i, j: (0, j)),
            pl.BlockSpec((d, tf), lambda i, j: (0, j + nf)),
            pl.BlockSpec((tf, d), lambda i, j: (j, 0)),
            pl.BlockSpec((1, d), lambda i, j: (0, 0)),
        ],
        out_specs=pl.BlockSpec((tm, d), lambda i, j: (i, 0)),
        out_shape=jax.ShapeDtypeStruct((m, d), F32),
        scratch_shapes=[pltpu.VMEM((tm, d), BF16), pltpu.VMEM((tm, d), F32)],
        compiler_params=_cparams(("parallel", "arbitrary")),
        name="ffn",
    )(x, g_pre, w_gu, w_gu, w_down, g_post)


def _inproj_kernel(x_ref, g_ref, w_ref, o_ref, h_sc):
    @pl.when(pl.program_id(1) == 0)
    def _():
        h_sc[...] = _rms(x_ref[...], g_ref[...]).astype(BF16)

    o_ref[...] = _dot(h_sc[...], w_ref[...]).astype(BF16)


def _inproj(x, g, w, *, tm, tn):
    m, d = x.shape
    n = w.shape[1]
    return pl.pallas_call(
        _inproj_kernel,
        grid=(m // tm, n // tn),
        in_specs=[
            pl.BlockSpec((tm, d), lambda i, j: (i, 0)),
            pl.BlockSpec((1, d), lambda i, j: (0, 0)),
            pl.BlockSpec((d, tn), lambda i, j: (0, j)),
        ],
        out_specs=pl.BlockSpec((tm, tn), lambda i, j: (i, j)),
        out_shape=jax.ShapeDtypeStruct((m, n), BF16),
        scratch_shapes=[pltpu.VMEM((tm, d), BF16)],
        compiler_params=_cparams(("parallel", "arbitrary")),
        name="inproj",
    )(x, g, w)


def _cmp_mlp_kernel(k_ref, v_ref, pek_ref, w1k_ref, w2k_ref, pev_ref, w1v_ref, w2v_ref,
                    kc_ref, vc_ref):
    half = CMP_STRIDE * HEAD_DIM

    def one(x_ref, pe_ref, w1_ref, w2_ref, o_ref):
        x = x_ref[0, 0].astype(F32)
        ncp = x.shape[0]
        xa = (x + pe_ref[0:1, :]).astype(BF16)
        xb = (x + pe_ref[1:2, :]).astype(BF16)
        pa = _dot(xa, w1_ref[0:half, :])
        pb = _dot(xb, w1_ref[half:2 * half, :])
        hid = pa + pltpu.roll(pb, ncp - 1, 0)
        out = _dot(jax.nn.gelu(hid).astype(BF16), w2_ref[...])
        row = lax.broadcasted_iota(jnp.int32, out.shape, 0)
        o_ref[0, 0] = jnp.where(row < ncp - 1, out, 0.0).astype(BF16)

    one(k_ref, pek_ref, w1k_ref, w2k_ref, kc_ref)
    one(v_ref, pev_ref, w1v_ref, w2v_ref, vc_ref)


def _cmp_mlp(k4, v4, pek, w1k, w2k, pev, w1v, w2v):
    b, hkv, ncp, width = k4.shape
    blk = pl.BlockSpec((1, 1, ncp, width), lambda i, j: (i, j, 0, 0))
    oblk = pl.BlockSpec((1, 1, ncp, HEAD_DIM), lambda i, j: (i, j, 0, 0))

    def full(a):
        return pl.BlockSpec(a.shape, lambda i, j: (0,) * a.ndim)

    out = jax.ShapeDtypeStruct((b, hkv, ncp, HEAD_DIM), BF16)
    return pl.pallas_call(
        _cmp_mlp_kernel,
        grid=(b, hkv),
        in_specs=[blk, blk, full(pek), full(w1k), full(w2k), full(pev), full(w1v), full(w2v)],
        out_specs=[oblk, oblk],
        out_shape=[out, out],
        compiler_params=_cparams(("parallel", "parallel")),
        name="cmp_mlp",
    )(k4, v4, pek, w1k, w2k, pev, w1v, w2v)


def _cmp_attn_kernel(q_ref, kc_ref, vc_ref, bias_ref, ovl_ref, o_ref, sel_ref, *, tq, ncp, nsel):
    qi = pl.program_id(1)
    kc = kc_ref[0, 0]
    vc = vc_ref[0, 0]
    row = lax.broadcasted_iota(jnp.int32, (tq, ncp), 0) + qi * tq
    col = lax.broadcasted_iota(jnp.int32, (tq, ncp), 1)
    valid = (col * CMP_STRIDE + (CMP_LEN - 1)) <= row
    p_sum = jnp.zeros((tq, ncp), F32)
    for g in range(GROUP):
        q = q_ref[0, 0, g] * (HEAD_DIM ** -0.5)
        s = _dot_nt(q, kc) + bias_ref[0, g]
        s = jnp.where(valid, s, NEG_INF)
        e = jnp.where(valid, jnp.exp(s - jnp.max(s, axis=-1, keepdims=True)), 0.0)
        den = jnp.sum(e, axis=-1, keepdims=True)
        p = e / jnp.where(den > 0, den, 1.0)
        o_ref[0, 0, g] = _dot(p.astype(BF16), vc).astype(BF16)
        p_sum = p_sum + p

    imp = _dot_nt(ovl_ref[...], p_sum, precision=lax.Precision.HIGHEST)
    jj = lax.broadcasted_iota(jnp.int32, (nsel, tq), 0)
    cur = (lax.broadcasted_iota(jnp.int32, (nsel, tq), 1) + qi * tq) >> SEL_SHIFT
    forced = (jj == 0) | (jj == cur) | (jj == cur - 1)
    score = jnp.where(forced, SEL_FORCE, jnp.where(jj <= cur, imp, -SEL_FORCE))
    rank = jnp.zeros((nsel, tq), F32)
    for i in range(nsel):
        si = score[i:i + 1, :]
        ge = jnp.where(si >= score, 1.0, 0.0)
        gt = jnp.where(si > score, 1.0, 0.0)
        rank = rank + jnp.where(jj > i, ge, gt)
    dropped = jnp.where(rank < min(SEL_TOPK, nsel), 0.0, 1.0).astype(BF16)
    eye = (lax.broadcasted_iota(jnp.int32, (tq, tq), 0)
           == lax.broadcasted_iota(jnp.int32, (tq, tq), 1))
    sel_ref[0, 0] = _dot_nt(jnp.where(eye, 1.0, 0.0).astype(BF16), dropped).astype(BF16)


def _cmp_attn(qn, kc, vc, bias_c, ovl_t, *, tq):
    b, hkv, g, s, d = qn.shape
    ncp = kc.shape[2]
    nsel = ovl_t.shape[0]
    kern = functools.partial(_cmp_attn_kernel, tq=tq, ncp=ncp, nsel=nsel)
    return pl.pallas_call(
        kern,
        grid=(hkv, s // tq, b),
        in_specs=[
            pl.BlockSpec((1, 1, g, tq, d), lambda h, i, bb: (bb, h, 0, i, 0)),
            pl.BlockSpec((1, 1, ncp, d), lambda h, i, bb: (bb, h, 0, 0)),
            pl.BlockSpec((1, 1, ncp, d), lambda h, i, bb: (bb, h, 0, 0)),
            pl.BlockSpec((1, g, tq, ncp), lambda h, i, bb: (h, 0, i, 0)),
            pl.BlockSpec((nsel, ncp), lambda h, i, bb: (0, 0)),
        ],
        out_specs=[
            pl.BlockSpec((1, 1, g, tq, d), lambda h, i, bb: (bb, h, 0, i, 0)),
            pl.BlockSpec((1, 1, tq, nsel), lambda h, i, bb: (bb, h, i, 0)),
        ],
        out_shape=[
            jax.ShapeDtypeStruct((b, hkv, g, s, d), BF16),
            jax.ShapeDtypeStruct((b, hkv, s, nsel), BF16),
        ],
        compiler_params=_cparams(("parallel", "parallel", "parallel")),
        name="cmp_attn",
    )(qn, kc, vc, bias_c, ovl_t)


def _finish(acc, dv, extra_den=None):
    den = acc[:, dv:2 * dv]
    if extra_den is not None:
        den = den + extra_den
    return acc[:, :dv] / den


def _band_attn_kernel(*refs, g, tq, w, back, n_off, has_sink, q_scale, dv):
    if has_sink:
        q_ref, k_ref, v_ref, bias_ref, sink_ref, o_ref = refs
    else:
        q_ref, k_ref, v_ref, bias_ref, o_ref = refs
    qi = pl.program_id(2)
    r = g * tq
    off = jnp.minimum(qi * tq, back)
    start = pl.multiple_of(qi * tq - off, LANE)
    k = k_ref[0, 0, pl.ds(start, w), :]
    v = v_ref[0, 0, pl.ds(start, w), :]
    q = q_ref[0, 0].reshape(r, q_ref.shape[-1]) * q_scale
    s = _dot_nt(q, k).reshape(g, tq, w) + bias_ref[0, jnp.minimum(qi, n_off - 1)]
    m = jnp.max(s, axis=-1, keepdims=True)
    if has_sink:
        sink = sink_ref[0]
        m = jnp.maximum(m, sink)
    p = jnp.exp(s - m).reshape(r, w)
    acc = _dot(p.astype(BF16), v)
    extra = jnp.exp(sink - m).reshape(r, 1) if has_sink else None
    o_ref[0, 0] = _finish(acc, dv, extra).reshape(g, tq, dv).astype(BF16)


def _causal_attn_kernel(*refs, g, tq, tk, n_bias, q_scale, dv):
    if n_bias:
        q_ref, k_ref, v_ref, bias_ref, o_ref, m_sc, acc_sc = refs
    else:
        q_ref, k_ref, v_ref, o_ref, m_sc, acc_sc = refs
    qi = pl.program_id(2)
    r = g * tq
    q = q_ref[0, 0].reshape(r, q_ref.shape[-1])
    if q_scale is not None:
        q = q * q_scale
    hi = ((qi + 1) * tq + tk - 1) // tk
    m_sc[...] = jnp.full(m_sc.shape, NEG_INF, F32)
    acc_sc[...] = jnp.zeros(acc_sc.shape, F32)

    def step(kj, mask):
        start = pl.multiple_of(kj * tk, tk)
        k = k_ref[0, 0, pl.ds(start, tk), :]
        v = v_ref[0, 0, pl.ds(start, tk), :]
        s = _dot_nt(q, k)
        if n_bias:
            u = jnp.minimum((qi * tq - kj * tk) // tq, n_bias - 1)
            s = (s.reshape(g, tq, tk) + bias_ref[0, u]).reshape(r, tk)
        if mask is not None:
            s = s + mask
        m_old = m_sc[...]
        m_new = jnp.maximum(m_old, jnp.max(s, axis=-1, keepdims=True))
        alpha = jnp.exp(m_old - m_new)
        p = jnp.exp(s - jnp.concatenate([m_new] * (tk // LANE), axis=1))
        acc_sc[...] = alpha * acc_sc[...] + _dot(p.astype(BF16), v)
        m_sc[...] = m_new

    if n_bias:
        first = 0
    else:
        row = lax.broadcasted_iota(jnp.int32, (r, tk), 0)
        col = lax.broadcasted_iota(jnp.int32, (r, tk), 1)
        step(hi - 1, jnp.where(col <= row, 0.0, NEG_INF))
        first = 1

    def body(i, carry):
        step(hi - 1 - i, None)
        return carry

    lax.fori_loop(first, hi, body, 0)
    o_ref[0, 0] = _finish(acc_sc[...], dv).reshape(g, tq, dv).astype(BF16)


def _qkv_specs(q, k, v, tq):
    g, dq = q.shape[2], q.shape[4]
    s = k.shape[2]
    return [
        pl.BlockSpec((1, 1, g, tq, dq), lambda h, bb, i: (bb, h, 0, i, 0)),
        pl.BlockSpec((1, 1, s, k.shape[3]), lambda h, bb, i: (bb, h, 0, 0)),
        pl.BlockSpec((1, 1, s, v.shape[3]), lambda h, bb, i: (bb, h, 0, 0)),
    ]


def _resident(a):
    return pl.BlockSpec((1,) + a.shape[1:], lambda h, bb, i: (h,) + (0,) * (a.ndim - 1),
                        pipeline_mode=pl.Buffered(1))


def _band_attention(q, k, v1, bias, *, tq, back, name, sinks=None, q_scale):
    b, hk, g, s, _ = q.shape
    dv = v1.shape[-1] // 2
    n_off, w = bias.shape[1], bias.shape[4]
    assert w == back + tq and w <= s and back % LANE == 0
    kern = functools.partial(_band_attn_kernel, g=g, tq=tq, w=w, back=back, n_off=n_off,
                             has_sink=sinks is not None, q_scale=q_scale, dv=dv)
    in_specs = _qkv_specs(q, k, v1, tq) + [_resident(bias)]
    args = [q, k, v1, bias]
    if sinks is not None:
        in_specs.append(pl.BlockSpec((1, g, 1, 1), lambda h, bb, i: (h, 0, 0, 0)))
        args.append(sinks)
    return pl.pallas_call(
        kern,
        grid=(hk, b, s // tq),
        in_specs=in_specs,
        out_specs=pl.BlockSpec((1, 1, g, tq, dv), lambda h, bb, i: (bb, h, 0, i, 0)),
        out_shape=jax.ShapeDtypeStruct((b, hk, g, s, dv), BF16),
        compiler_params=_cparams(("parallel", "parallel", "parallel")),
        name=name,
    )(*args)


def _causal_attention(q, k, v1, *, tq, tk, name, bias=None, q_scale=None):
    b, hk, g, s, _ = q.shape
    dv = v1.shape[-1] // 2
    assert 2 * dv == LANE
    n_bias = 0 if bias is None else bias.shape[1]
    assert n_bias or tq == tk
    kern = functools.partial(_causal_attn_kernel, g=g, tq=tq, tk=tk, n_bias=n_bias,
                             q_scale=q_scale, dv=dv)
    in_specs = _qkv_specs(q, k, v1, tq)
    args = [q, k, v1]
    if n_bias:
        in_specs.append(_resident(bias))
        args.append(bias)
    return pl.pallas_call(
        kern,
        grid=(hk, b, s // tq),
        in_specs=in_specs,
        out_specs=pl.BlockSpec((1, 1, g, tq, dv), lambda h, bb, i: (bb, h, 0, i, 0)),
        out_shape=jax.ShapeDtypeStruct((b, hk, g, s, dv), BF16),
        scratch_shapes=[pltpu.VMEM((g * tq, LANE), F32), pltpu.VMEM((g * tq, 2 * dv), F32)],
        compiler_params=_cparams(("parallel", "parallel", "arbitrary")),
        name=name,
    )(*args)


def _mla_proj_kernel(mq_ref, mkv_ref, kr1_ref, kr2_ref, cos_ref, sin_ref, qn_ref, wq_ref,
                     kvn_ref, wkv_ref, oq_ref, oqr_ref, okv_ref, okr_ref):
    cos = cos_ref[...]
    sin = sin_ref[...]
    nope = N_HEADS * MLA_NOPE
    ql = _rms(mq_ref[...].astype(F32), qn_ref[...]).astype(BF16)
    q = _dot(ql, wq_ref[...]) * (MLA_QK ** -0.5)
    oq_ref[...] = q[:, :nope].astype(BF16)
    r1 = q[:, nope:nope + LANE]
    r2 = q[:, nope + LANE:nope + 2 * LANE]
    oqr_ref[:, 0:LANE] = (r1 * cos - r2 * sin).astype(BF16)
    oqr_ref[:, LANE:2 * LANE] = (r1 * sin + r2 * cos).astype(BF16)
    kvl = _rms(mkv_ref[...].astype(F32), kvn_ref[...]).astype(BF16)
    okv_ref[...] = _dot(kvl, wkv_ref[...]).astype(BF16)
    k1 = kr1_ref[...].astype(F32)
    k2 = kr2_ref[...].astype(F32)
    okr_ref[:, 0:LANE] = (k1 * cos - k2 * sin).astype(BF16)
    okr_ref[:, LANE:2 * LANE] = (k1 * sin + k2 * cos).astype(BF16)


def _mla_proj(z, cos, sin, q_norm, w_qb, kv_norm, w_kvb, *, tm, seq):
    m = z.shape[0]
    nt = seq // tm
    nope = N_HEADS * MLA_NOPE

    def full(a):
        return pl.BlockSpec(a.shape, lambda i: (0,) * a.ndim)

    return pl.pallas_call(
        _mla_proj_kernel,
        grid=(m // tm,),
        in_specs=[
            pl.BlockSpec((tm, MLA_Q_RANK), lambda i: (i, ZB_MQ * LANE // MLA_Q_RANK)),
            pl.BlockSpec((tm, MLA_KV_RANK), lambda i: (i, ZB_MKV * LANE // MLA_KV_RANK)),
            pl.BlockSpec((tm, LANE), lambda i: (i, ZB_KR1)),
            pl.BlockSpec((tm, LANE), lambda i: (i, ZB_KR2)),
            pl.BlockSpec((tm, LANE), lambda i: (i % nt, 0)),
            pl.BlockSpec((tm, LANE), lambda i: (i % nt, 0)),
            full(q_norm), full(w_qb), full(kv_norm), full(w_kvb),
        ],
        out_specs=[
            pl.BlockSpec((tm, nope), lambda i: (i, 0)),
            pl.BlockSpec((tm, 2 * LANE), lambda i: (i, 0)),
            pl.BlockSpec((tm, 2 * nope), lambda i: (i, 0)),
            pl.BlockSpec((tm, 2 * LANE), lambda i: (i, 0)),
        ],
        out_shape=[
            jax.ShapeDtypeStruct((m, nope), BF16),
            jax.ShapeDtypeStruct((m, 2 * LANE), BF16),
            jax.ShapeDtypeStruct((m, 2 * nope), BF16),
            jax.ShapeDtypeStruct((m, 2 * LANE), BF16),
        ],
        compiler_params=_cparams(("parallel",)),
        name="mla_proj",
    )(z, z, z, z, cos, sin, q_norm, w_qb, kv_norm, w_kvb)


def _merge_kernel(ocmp_ref, osel_ref, owin_ref, ob_ref, oc_ref, ng_ref, mg_ref, x_ref,
                  gexp_ref, wb_ref, wo_ref, gpost_ref, o_ref):
    ng = jax.nn.sigmoid(ng_ref[...].astype(F32))
    o_a = jnp.zeros(ocmp_ref.shape, F32)
    for n, ref in enumerate((ocmp_ref, osel_ref, owin_ref)):
        gate = jnp.dot(ng, gexp_ref[n], precision=lax.Precision.HIGHEST,
                       preferred_element_type=F32)
        o_a = o_a + gate * ref[...].astype(F32)
    mixed = jnp.zeros(x_ref.shape, F32)
    for n, br in enumerate((o_a.astype(BF16), ob_ref[...], oc_ref[...])):
        y = _dot(br, wb_ref[n])
        gate = jax.nn.sigmoid(mg_ref[:, n * D_MODEL:(n + 1) * D_MODEL].astype(F32))
        mixed = mixed + gate * y
    out = _dot(mixed.astype(BF16), wo_ref[...])
    o_ref[...] = x_ref[...] + _rms(out, gpost_ref[...])


def _merge(o_cmp, o_sel, o_win, o_b, o_c, z, x, gexp, w_branch, w_out, g_post, *, tm):
    m, d = x.shape
    bw = o_cmp.shape[1]
    br = pl.BlockSpec((tm, bw), lambda i: (i, 0))

    def full(a):
        return pl.BlockSpec(a.shape, lambda i: (0,) * a.ndim)

    return pl.pallas_call(
        _merge_kernel,
        grid=(m // tm,),
        in_specs=[
            br, br, br, br, br,
            pl.BlockSpec((tm, LANE), lambda i: (i, ZB_NGATE)),
            pl.BlockSpec((tm, 3 * d), lambda i: (i, ZB_MGATE * LANE // (3 * d))),
            pl.BlockSpec((tm, d), lambda i: (i, 0)),
            full(gexp), full(w_branch), full(w_out), full(g_post),
        ],
        out_specs=pl.BlockSpec((tm, d), lambda i: (i, 0)),
        out_shape=jax.ShapeDtypeStruct((m, d), F32),
        compiler_params=_cparams(("parallel",)),
        name="merge",
    )(o_cmp, o_sel, o_win, o_b, o_c, z, z, x, gexp, w_branch, w_out, g_post)


def _t5_bucket(dist):
    max_exact = NUM_BUCKETS // 2
    d = jnp.maximum(dist, 0)
    df = jnp.maximum(d, 1).astype(F32)
    large = max_exact + (jnp.log(df / max_exact) / math.log(MAX_DISTANCE / max_exact)
                         * (NUM_BUCKETS - max_exact)).astype(jnp.int32)
    large = jnp.minimum(large, NUM_BUCKETS - 1)
    return jnp.where(d < max_exact, d, large)


def _toeplitz(fn, offs, tq, tk):
    ln = tq + tk - 1
    i = jnp.arange(ln + 1)
    shift = jnp.where(i < tk, -i, ln + 1 - i)
    ext = jnp.moveaxis(fn(jnp.asarray(offs)[:, None] + shift[None, :]), -1, 0)
    flat = jnp.tile(ext, (1, 1, tq))[:, :, :tq * ln]
    return flat.reshape(ext.shape[0], len(offs), tq, ln)[:, :, :, :tk]


def _bias_fn(table, window):
    def fn(dist):
        ok = dist >= 0
        if window is not None:
            ok = ok & (dist < window)
        return jnp.where(ok[..., None], table[_t5_bucket(dist)].astype(F32), NEG_INF)
    return fn


def _head_tiles(t):
    return t.reshape(N_KV, GROUP, *t.shape[1:]).transpose(0, 2, 1, 3, 4)


def _in_proj_weight(w_in):
    sizes = (512, 128, 128, 128, 128, 128, 128, 24, 512, 128, 128, MLA_Q_RANK, MLA_KV_RANK,
             MLA_ROPE, 3 * D_MODEL)
    offs = np.concatenate([[0], np.cumsum(sizes)])
    seg = [w_in[:, offs[i]:offs[i + 1]] for i in range(len(sizes))]
    (nq, nkc, nvc, nks, nvs, nkw, nvw, ngate, sq, sk, sv, mq, mkv, mkr, mgate) = seg
    d = w_in.shape[0]

    def pad(a):
        return jnp.pad(a, ((0, 0), (0, LANE - a.shape[1])))

    half = MLA_ROPE // 2
    cols = [mq, pad(ngate), mkv, pad(mkr[:, :half]), pad(mkr[:, half:]), nq,
            nkc, nvc, nks, nvs, nkw, nvw, sq, sk, sv, mgate]
    w = jnp.concatenate(cols, axis=1)
    assert w.shape == (d, Z_WIDTH)
    return w.astype(BF16)


def _mla_weights(w_qb, w_kvb):
    half = MLA_ROPE // 2
    wq = w_qb.reshape(MLA_Q_RANK, N_HEADS, MLA_QK)
    wq = jnp.concatenate([
        wq[:, :, :MLA_NOPE].reshape(MLA_Q_RANK, -1),
        wq[:, :, MLA_NOPE:MLA_NOPE + half].reshape(MLA_Q_RANK, -1),
        wq[:, :, MLA_NOPE + half:].reshape(MLA_Q_RANK, -1)], axis=1)
    wkv = w_kvb.reshape(MLA_KV_RANK, N_HEADS, MLA_NOPE + MLA_V)
    wkv = jnp.concatenate([wkv[:, :, :MLA_NOPE].reshape(MLA_KV_RANK, -1),
                           wkv[:, :, MLA_NOPE:].reshape(MLA_KV_RANK, -1)], axis=1)
    return wq.astype(BF16), wkv.astype(BF16)


def _gate_expand():
    e = np.zeros((3, LANE, BRANCH_WIDTH), np.float32)
    for n in range(3):
        for h in range(N_HEADS):
            e[n, h * 3 + n, h * HEAD_DIM:(h + 1) * HEAD_DIM] = 1.0
    return jnp.asarray(e)


def _heads(a, b, s, nh):
    return a.reshape(b, s, nh, -1).transpose(0, 2, 1, 3)


def _unheads(a):
    b, hk, g, s, d = a.shape
    return a.transpose(0, 3, 1, 2, 4).reshape(b * s, hk * g * d)


def kernel(x, rel_bias_table, ffn1_norm_pre, ffn1_w_gu, ffn1_w_down, ffn1_norm_post,
           mix_norm_pre, w_in, nsa_pe_k, nsa_w1_k, nsa_w2_k, nsa_pe_v, nsa_w1_v, nsa_w2_v,
           swa_sinks, mla_q_norm, mla_w_qb, mla_kv_norm, mla_w_kvb, w_branch, w_out,
           mix_norm_post, ffn2_norm_pre, ffn2_w_gu, ffn2_w_down, ffn2_norm_post):
    b, s, d = x.shape
    depth = w_in.shape[0]
    m = b * s
    tq = 256
    tq_mla = min(512, s)
    ncp = s // CMP_STRIDE
    nsel = s // SEL_LEN
    tm = min(512, m)
    tm_big = min(1024, s)

    pos = jnp.arange(s, dtype=jnp.int32)
    tab_a = rel_bias_table[:, :N_HEADS]
    tab_b = rel_bias_table[:, N_HEADS:]
    tk_sel = min(512, s)
    n_far = -(-(LAST_BUCKET_DIST + tk_sel - 1) // tq)
    n_sel_tiles = min(n_far + 1, s // tq)
    bias_sel = _head_tiles(_toeplitz(_bias_fn(tab_a, None), [u * tq for u in range(n_sel_tiles)],
                                     tq, tk_sel))
    win_offs = sorted({min(i * tq, NSA_WINDOW) for i in range(-(-NSA_WINDOW // tq) + 1)})
    bias_win = _head_tiles(_toeplitz(_bias_fn(tab_a, NSA_WINDOW), win_offs, tq, NSA_WINDOW + tq))
    swa_offs = sorted({min(i * tq, SWA_WINDOW) for i in range(-(-SWA_WINDOW // tq) + 1)})
    bias_swa = _head_tiles(_toeplitz(_bias_fn(tab_b, SWA_WINDOW), swa_offs, tq, SWA_WINDOW + tq))

    def cmp_bias_fn(e):
        r = jnp.arange(CMP_STRIDE)
        dist = e[..., None] * CMP_STRIDE + r - (CMP_LEN - 1)
        return tab_a[_t5_bucket(dist)].astype(F32).reshape(*e.shape, CMP_STRIDE * N_HEADS)

    bias_c = _toeplitz(cmp_bias_fn, [0], ncp, ncp).reshape(CMP_STRIDE, N_HEADS, ncp, ncp)
    bias_c = bias_c.transpose(1, 2, 0, 3).reshape(N_KV, GROUP, s, ncp)
    sel_cols = jnp.where((pos[:, None] >> SEL_SHIFT) == jnp.arange(nsel)[None, :],
                         NEG_INF / HEAD_DIM ** -0.5, 0.0).astype(BF16)
    ones_v = jnp.ones((b, N_KV, s, HEAD_DIM), BF16)
    ci = jnp.arange(ncp)[None, :] * CMP_STRIDE
    sj = jnp.arange(nsel)[:, None] * SEL_LEN
    ovl_t = (jnp.maximum(jnp.minimum(ci + CMP_LEN, sj + SEL_LEN) - jnp.maximum(ci, sj), 0)
             .astype(F32) / CMP_LEN)
    half = MLA_ROPE // 2
    inv_freq = ROPE_BASE ** (-jnp.arange(half, dtype=F32) / half)
    ang = pos.astype(F32)[:, None] * inv_freq[None, :]
    cos = jnp.tile(jnp.cos(ang), (1, LANE // half))
    sin = jnp.tile(jnp.sin(ang), (1, LANE // half))
    gexp = _gate_expand()

    xf = x.reshape(m, d)
    for l in range(depth):
        xf = _ffn(xf, ffn1_norm_pre[l][None], ffn1_w_gu[l].astype(BF16),
                  ffn1_w_down[l].astype(BF16), ffn1_norm_post[l][None], tm=tm, nf=2)

        z = _inproj(xf, mix_norm_pre[l][None], _in_proj_weight(w_in[l]), tm=tm_big, tn=Z_WIDTH // 4)

        def zcols(blk, n):
            return z[:, blk * LANE:(blk + n) * LANE]

        qn = _heads(zcols(ZB_NQ, 4), b, s, N_HEADS).reshape(b, N_KV, GROUP, s, HEAD_DIM)
        nkv = [_heads(zcols(ZB_NKV + i, 1), b, s, N_KV) for i in range(6)]
        k_c, v_c, k_s, v_s, k_w, v_w = nkv
        chunk = CMP_STRIDE * HEAD_DIM
        kc, vc = _cmp_mlp(k_c.reshape(b, N_KV, ncp, chunk), v_c.reshape(b, N_KV, ncp, chunk),
                          nsa_pe_k[l].reshape(2, chunk), nsa_w1_k[l].astype(BF16),
                          nsa_w2_k[l].astype(BF16),
                          nsa_pe_v[l].reshape(2, chunk), nsa_w1_v[l].astype(BF16),
                          nsa_w2_v[l].astype(BF16))
        o_cmp, dropped = _cmp_attn(qn, kc, vc, bias_c, ovl_t, tq=tq)
        q_sel = jnp.concatenate(
            [qn, jnp.broadcast_to(dropped[:, :, None], (b, N_KV, GROUP, s, nsel))], axis=-1)
        k_sel = jnp.concatenate(
            [k_s, jnp.broadcast_to(sel_cols[None, None], (b, N_KV, s, nsel))], axis=-1)
        o_sel = _causal_attention(q_sel, k_sel, jnp.concatenate([v_s, ones_v], axis=-1),
                                  tq=tq, tk=tk_sel, name="attn_sel", bias=bias_sel,
                                  q_scale=HEAD_DIM ** -0.5)
        o_win = _band_attention(qn, k_w, jnp.concatenate([v_w, ones_v], axis=-1), bias_win,
                                tq=tq, back=NSA_WINDOW, name="attn_win", q_scale=HEAD_DIM ** -0.5)
        sq = _heads(zcols(ZB_SQ, 4), b, s, N_HEADS).reshape(b, N_KV, GROUP, s, HEAD_DIM)
        sk = _heads(zcols(ZB_SK, 1), b, s, N_KV)
        sv = _heads(zcols(ZB_SV, 1), b, s, N_KV)
        o_b = _band_attention(sq, sk, jnp.concatenate([sv, ones_v], axis=-1), bias_swa,
                              tq=tq, back=SWA_WINDOW, name="attn_swa",
                              sinks=swa_sinks[l].reshape(N_KV, GROUP, 1, 1),
                              q_scale=HEAD_DIM ** -0.5)
        wq, wkv = _mla_weights(mla_w_qb[l], mla_w_kvb[l])
        mq_n, mq_r, mkv_u, mk_r = _mla_proj(z, cos, sin, mla_q_norm[l][None], wq,
                                            mla_kv_norm[l][None], wkv, tm=tm_big, seq=s)
        nope = N_HEADS * MLA_NOPE
        q96 = jnp.concatenate([
            mq_n.reshape(b, s, N_HEADS, MLA_NOPE),
            mq_r[:, :LANE].reshape(b, s, N_HEADS, half),
            mq_r[:, LANE:].reshape(b, s, N_HEADS, half)], axis=-1).transpose(0, 2, 1, 3)
        kr = jnp.concatenate([mk_r[:, :half], mk_r[:, LANE:LANE + half]], axis=-1)
        k96 = jnp.concatenate([
            mkv_u[:, :nope].reshape(b, s, N_HEADS, MLA_NOPE),
            jnp.broadcast_to(kr.reshape(b, s, 1, MLA_ROPE), (b, s, N_HEADS, MLA_ROPE))],
            axis=-1).transpose(0, 2, 1, 3)
        v_m = _heads(mkv_u[:, nope:], b, s, N_HEADS)
        v_m1 = jnp.concatenate([v_m, jnp.ones_like(v_m)], axis=-1)
        o_c = _causal_attention(q96[:, :, None], k96, v_m1, tq=tq_mla, tk=tq_mla, name="attn_mla")

        xf = _merge(_unheads(o_cmp), _unheads(o_sel), _unheads(o_win), _unheads(o_b),
                    _unheads(o_c), z, xf, gexp, w_branch[l].astype(BF16), w_out[l].astype(BF16),
                    mix_norm_post[l][None], tm=tm)

        xf = _ffn(xf, ffn2_norm_pre[l][None], ffn2_w_gu[l].astype(BF16),
                  ffn2_w_down[l].astype(BF16), ffn2_norm_post[l][None], tm=tm, nf=2)
    return xf.reshape(b, s, d)
```

```python
import functools
import math

import jax
import jax.numpy as jnp
import numpy as np
from jax import lax
from jax.experimental import pallas as pl
from jax.experimental.pallas import tpu as pltpu

F32 = jnp.float32
BF16 = jnp.bfloat16

D_MODEL = 1024
D_FF = 2816
HEAD_DIM = 64
NORM_EPS = 1e-6
NUM_BUCKETS = 32
MAX_DISTANCE = 1024
LAST_BUCKET_DIST = 1 + math.ceil(
    (NUM_BUCKETS // 2) * (MAX_DISTANCE / (NUM_BUCKETS // 2))
    ** ((NUM_BUCKETS // 2 - 1) / (NUM_BUCKETS - NUM_BUCKETS // 2)))
N_HEADS = 8
N_KV = 2
GROUP = N_HEADS // N_KV
CMP_LEN = 32
CMP_STRIDE = 16
CMP_HIDDEN = 128
SEL_LEN = 64
SEL_SHIFT = 6
SEL_TOPK = 16
NSA_WINDOW = 512
SWA_WINDOW = 128
MLA_Q_RANK = 384
MLA_KV_RANK = 256
MLA_NOPE = 64
MLA_ROPE = 32
MLA_V = 64
MLA_QK = MLA_NOPE + MLA_ROPE
ROPE_BASE = 10000.0
BRANCH_WIDTH = N_HEADS * HEAD_DIM
NEG_INF = -1e30
SEL_FORCE = 1e9

LANE = 128
VMEM_LIMIT = 56 * 1024 * 1024
HEAD_SHIFT = 6
GROUP_WIDTH = GROUP * HEAD_DIM
MLA_CHAINS = 4

ZB_MQ = 0
ZB_NGATE = 3
ZB_MKV = 4
ZB_KR = 6
ZB_NQ = 8
ZB_CMP_K = 12
ZB_CMP_V = 13
ZB_SEL_KV = 14
ZB_WIN_KV = 16
ZB_SQ = 18
ZB_SWA_KV = 22
ZB_MGATE = 24
Z_BLOCKS = 48
Z_WIDTH = Z_BLOCKS * LANE


def _cparams(sem):
    return pltpu.CompilerParams(dimension_semantics=sem, vmem_limit_bytes=VMEM_LIMIT)


def _rms(x, g):
    inv = lax.rsqrt(jnp.mean(x * x, axis=-1, keepdims=True) + NORM_EPS)
    return (x * inv) * g


def _dot(a, b):
    return jnp.dot(a, b, preferred_element_type=F32)


def _dot_nt(a, b, precision=None):
    return lax.dot_general(a, b, (((1,), (1,)), ((), ())), precision=precision,
                           preferred_element_type=F32)


def _ffn_kernel(x_ref, gpre_ref, wg_ref, wu_ref, wd_ref, gpost_ref, o_ref, h_sc, acc_sc):
    j = pl.program_id(1)

    @pl.when(j == 0)
    def _():
        h_sc[...] = _rms(x_ref[...], gpre_ref[...]).astype(BF16)
        acc_sc[...] = jnp.zeros_like(acc_sc)

    h = h_sc[...]
    g = _dot(h, wg_ref[...])
    u = _dot(h, wu_ref[...])
    act = (g * jax.nn.sigmoid(g)) * u
    acc_sc[...] += _dot(act.astype(BF16), wd_ref[...])

    @pl.when(j == pl.num_programs(1) - 1)
    def _():
        o_ref[...] = x_ref[...] + 0.5 * _rms(acc_sc[...], gpost_ref[...])


def _ffn(x, g_pre, w_gu, w_down, g_post, *, tm, nf):
    m, d = x.shape
    tf = D_FF // nf
    return pl.pallas_call(
        _ffn_kernel,
        grid=(m // tm, nf),
        in_specs=[
            pl.BlockSpec((tm, d), lambda i, j: (i, 0)),
            pl.BlockSpec((1, d), lambda i, j: (0, 0)),
            pl.BlockSpec((d, tf), lambda i, j: (0, j)),
            pl.BlockSpec((d, tf), lambda i, j: (0, j + nf)),
            pl.BlockSpec((tf, d), lambda i, j: (j, 0)),
            pl.BlockSpec((1, d), lambda i, j: (0, 0)),
        ],
        out_specs=pl.BlockSpec((tm, d), lambda i, j: (i, 0)),
        out_shape=jax.ShapeDtypeStruct((m, d), F32),
        scratch_shapes=[pltpu.VMEM((tm, d), BF16), pltpu.VMEM((tm, d), F32)],
        compiler_params=_cparams(("parallel", "arbitrary")),
        name="ffn",
    )(x, g_pre, w_gu, w_gu, w_down, g_post)


def _inproj_kernel(x_ref, g_ref, w_ref, o_ref, h_sc):
    @pl.when(pl.program_id(1) == 0)
    def _():
        h_sc[...] = _rms(x_ref[...], g_ref[...]).astype(BF16)

    o_ref[...] = _dot(h_sc[...], w_ref[...]).astype(BF16)


def _inproj(x, g, w, *, tm, tn):
    m, d = x.shape
    n = w.shape[1]
    return pl.pallas_call(
        _inproj_kernel,
        grid=(m // tm, n // tn),
        in_specs=[
            pl.BlockSpec((tm, d), lambda i, j: (i, 0)),
            pl.BlockSpec((1, d), lambda i, j: (0, 0)),
            pl.BlockSpec((d, tn), lambda i, j: (0, j)),
        ],
        out_specs=pl.BlockSpec((tm, tn), lambda i, j: (i, j)),
        out_shape=jax.ShapeDtypeStruct((m, n), BF16),
        scratch_shapes=[pltpu.VMEM((tm, d), BF16)],
        compiler_params=_cparams(("parallel", "arbitrary")),
        name="inproj",
    )(x, g, w)


def _cmp_mlp_kernel(k_ref, v_ref, pek_ref, w1k_ref, w2k_ref, pev_ref, w1v_ref, w2v_ref,
                    kc_ref, vc_ref):
    half = CMP_STRIDE * HEAD_DIM

    def one(x_ref, pe_ref, w1_ref, w2_ref, o_ref):
        x = x_ref[0, 0].astype(F32)
        ncp = x.shape[0]
        xa = (x + pe_ref[0:1, :]).astype(BF16)
        xb = (x + pe_ref[1:2, :]).astype(BF16)
        pa = _dot(xa, w1_ref[0:half, :])
        pb = _dot(xb, w1_ref[half:2 * half, :])
        hid = pa + pltpu.roll(pb, ncp - 1, 0)
        out = _dot(jax.nn.gelu(hid).astype(BF16), w2_ref[...])
        row = lax.broadcasted_iota(jnp.int32, out.shape, 0)
        o_ref[0, 0] = jnp.where(row < ncp - 1, out, 0.0).astype(BF16)

    one(k_ref, pek_ref, w1k_ref, w2k_ref, kc_ref)
    one(v_ref, pev_ref, w1v_ref, w2v_ref, vc_ref)


def _cmp_mlp(k4, v4, pek, w1k, w2k, pev, w1v, w2v):
    b, hkv, ncp, width = k4.shape
    blk = pl.BlockSpec((1, 1, ncp, width), lambda i, j: (i, j, 0, 0))
    oblk = pl.BlockSpec((1, 1, ncp, HEAD_DIM), lambda i, j: (i, j, 0, 0))

    def full(a):
        return pl.BlockSpec(a.shape, lambda i, j: (0,) * a.ndim)

    out = jax.ShapeDtypeStruct((b, hkv, ncp, HEAD_DIM), BF16)
    return pl.pallas_call(
        _cmp_mlp_kernel,
        grid=(b, hkv),
        in_specs=[blk, blk, full(pek), full(w1k), full(w2k), full(pev), full(w1v), full(w2v)],
        out_specs=[oblk, oblk],
        out_shape=[out, out],
        compiler_params=_cparams(("parallel", "parallel")),
        name="cmp_mlp",
    )(k4, v4, pek, w1k, w2k, pev, w1v, w2v)


def _head_rows(q_all, scale):
    lane_head = lax.broadcasted_iota(jnp.int32, (1, q_all.shape[1]), 1) >> HEAD_SHIFT
    return [q_all * jnp.where(lane_head == h, scale, 0.0).astype(q_all.dtype)
            for h in range(GROUP)]


def _split_kv(kv_ref, krep_sc, v2_sc):
    x = kv_ref[...].astype(F32)
    rot = pltpu.roll(x, HEAD_DIM, 1)
    lane = lax.broadcasted_iota(jnp.int32, x.shape, 1)
    krep_sc[...] = jnp.where(lane < HEAD_DIM, x, rot).astype(BF16)
    v2_sc[:, 0:LANE] = jnp.where(lane < HEAD_DIM, rot, 1.0).astype(BF16)
    v2_sc[:, LANE:2 * LANE] = jnp.where(lane < HEAD_DIM, 1.0, x).astype(BF16)


def _finish_pair(acc_even, acc_odd):
    lane = lax.broadcasted_iota(jnp.int32, (acc_even.shape[0], LANE), 1)
    even = acc_even[:, :LANE] / acc_even[:, LANE:]
    odd = acc_odd[:, LANE:] / acc_odd[:, :LANE]
    return jnp.where(lane < HEAD_DIM, even, odd)


def _finish_group(accs):
    return jnp.concatenate([_finish_pair(accs[0], accs[1]), _finish_pair(accs[2], accs[3])], axis=1)


def _online_update(s, v, m_ref, acc_ref, idx):
    m_old = m_ref[idx]
    m_new = jnp.maximum(m_old, jnp.max(s, axis=-1, keepdims=True))
    alpha = jnp.exp(m_old - m_new)
    p = jnp.exp(s - jnp.concatenate([m_new] * (s.shape[1] // LANE), axis=1))
    acc = acc_ref[idx]
    acc_ref[idx] = (jnp.concatenate([alpha] * (acc.shape[1] // LANE), axis=1) * acc
                    + _dot(p.astype(BF16), v))
    m_ref[idx] = m_new


def _cmp_attn_kernel(q_ref, kc_ref, vc_ref, bias_ref, ovl_ref, o_ref, sel_ref, *, tq, ncp, nsel):
    qi = pl.program_id(1)
    kc = kc_ref[0, 0]
    vc = vc_ref[0, 0]
    lane = lax.broadcasted_iota(jnp.int32, (tq, LANE), 1)
    row = lax.broadcasted_iota(jnp.int32, (tq, ncp), 0) + qi * tq
    col = lax.broadcasted_iota(jnp.int32, (tq, ncp), 1)
    valid = (col * CMP_STRIDE + (CMP_LEN - 1)) <= row
    p_sum = jnp.zeros((tq, ncp), F32)
    outs = []
    for g, q in enumerate(_head_rows(q_ref[...], HEAD_DIM ** -0.5)):
        s = _dot_nt(q, kc) + bias_ref[0, g]
        s = jnp.where(valid, s, NEG_INF)
        e = jnp.where(valid, jnp.exp(s - jnp.max(s, axis=-1, keepdims=True)), 0.0)
        den = jnp.sum(e, axis=-1, keepdims=True)
        p = e / jnp.where(den > 0, den, 1.0)
        outs.append(_dot(p.astype(BF16), vc))
        p_sum = p_sum + p
    o_ref[...] = jnp.concatenate([jnp.where(lane < HEAD_DIM, outs[0], outs[1]),
                                  jnp.where(lane < HEAD_DIM, outs[2], outs[3])],
                                 axis=1).astype(BF16)

    imp = _dot_nt(ovl_ref[...], p_sum, precision=lax.Precision.HIGHEST)
    jj = lax.broadcasted_iota(jnp.int32, (nsel, tq), 0)
    cur = (lax.broadcasted_iota(jnp.int32, (nsel, tq), 1) + qi * tq) >> SEL_SHIFT
    forced = (jj == 0) | (jj == cur) | (jj == cur - 1)
    score = jnp.where(forced, SEL_FORCE, jnp.where(jj <= cur, imp, -SEL_FORCE))
    rank = jnp.zeros((nsel, tq), F32)
    for i in range(nsel):
        si = score[i:i + 1, :]
        ge = jnp.where(si >= score, 1.0, 0.0)
        gt = jnp.where(si > score, 1.0, 0.0)
        rank = rank + jnp.where(jj > i, ge, gt)
    dropped = jnp.where(rank < min(SEL_TOPK, nsel), 0.0, 1.0).astype(BF16)
    eye = (lax.broadcasted_iota(jnp.int32, (tq, tq), 0)
           == lax.broadcasted_iota(jnp.int32, (tq, tq), 1))
    sel_ref[0, 0] = _dot_nt(jnp.where(eye, 1.0, 0.0).astype(BF16), dropped).astype(BF16)


def _cmp_attn(z, kc4, vc, bias_c, ovl_t, *, tq, seq):
    b, hkv, ncp, _ = kc4.shape
    m = z.shape[0]
    nq = seq // tq
    nsel = ovl_t.shape[0]
    kern = functools.partial(_cmp_attn_kernel, tq=tq, ncp=ncp, nsel=nsel)
    qblk = ZB_NQ * LANE // GROUP_WIDTH
    return pl.pallas_call(
        kern,
        grid=(hkv, nq, b),
        in_specs=[
            pl.BlockSpec((tq, GROUP_WIDTH), lambda h, i, bb: (bb * nq + i, qblk + h)),
            pl.BlockSpec((1, 1, ncp, GROUP_WIDTH), lambda h, i, bb: (bb, h, 0, 0)),
            pl.BlockSpec((1, 1, ncp, LANE), lambda h, i, bb: (bb, h, 0, 0)),
            pl.BlockSpec((1, GROUP, tq, ncp), lambda h, i, bb: (h, 0, i, 0)),
            pl.BlockSpec((nsel, ncp), lambda h, i, bb: (0, 0)),
        ],
        out_specs=[
            pl.BlockSpec((tq, GROUP_WIDTH), lambda h, i, bb: (bb * nq + i, h)),
            pl.BlockSpec((1, 1, tq, nsel), lambda h, i, bb: (bb, h, i, 0)),
        ],
        out_shape=[
            jax.ShapeDtypeStruct((m, BRANCH_WIDTH), BF16),
            jax.ShapeDtypeStruct((b, hkv, seq, nsel), BF16),
        ],
        compiler_params=_cparams(("parallel", "parallel", "parallel")),
        name="cmp_attn",
    )(z, kc4, vc, bias_c, ovl_t)


def _band_attn_kernel(*refs, tq, w, back, n_off, has_sink):
    if has_sink:
        q_ref, kv_ref, bias_ref, sink_ref, o_ref, krep_sc, v2_sc = refs
    else:
        q_ref, kv_ref, bias_ref, o_ref, krep_sc, v2_sc = refs
    qi = pl.program_id(2)

    @pl.when(qi == 0)
    def _():
        _split_kv(kv_ref, krep_sc, v2_sc)

    off = jnp.minimum(qi * tq, back)
    start = pl.multiple_of(qi * tq - off, LANE)
    k2 = krep_sc[pl.ds(start, w), :]
    k4 = jnp.concatenate([k2, k2], axis=1)
    v2 = v2_sc[pl.ds(start, w), :]
    tile = jnp.minimum(qi, n_off - 1)
    lane_q = lax.broadcasted_iota(jnp.int32, (tq, 2 * LANE), 1) >> HEAD_SHIFT
    sum_lane = (lane_q == 1) | (lane_q == 2)
    accs = []
    for h, q in enumerate(_head_rows(q_ref[...], HEAD_DIM ** -0.5)):
        s = _dot_nt(q, k4) + bias_ref[0, tile, h]
        m = jnp.max(s, axis=-1, keepdims=True)
        if has_sink:
            sink = sink_ref[0, h]
            m = jnp.maximum(m, sink)
        acc = _dot(jnp.exp(s - m).astype(BF16), v2)
        if has_sink:
            acc = acc + jnp.where(sum_lane, jnp.exp(sink - m), 0.0)
        accs.append(acc)
    o_ref[...] = _finish_group(accs).astype(BF16)


def _sel_attn_kernel(q_ref, kv_ref, drop_ref, bias_ref, o_ref, krep_sc, v2_sc, m_sc, acc_sc,
                     *, tq, tk, n_bias):
    qi = pl.program_id(2)

    @pl.when(qi == 0)
    def _():
        _split_kv(kv_ref, krep_sc, v2_sc)

    qs = _head_rows(q_ref[...], HEAD_DIM ** -0.5)
    dropped = drop_ref[0, 0]
    nsel = dropped.shape[1]
    hi = ((qi + 1) * tq + tk - 1) // tk
    m_sc[...] = jnp.full(m_sc.shape, NEG_INF, F32)
    acc_sc[...] = jnp.zeros(acc_sc.shape, F32)
    jb = lax.broadcasted_iota(jnp.int32, (nsel, tk), 0)
    cb = lax.broadcasted_iota(jnp.int32, (nsel, tk), 1) >> SEL_SHIFT

    def body(i, carry):
        kj = hi - 1 - i
        start = pl.multiple_of(kj * tk, tk)
        k2 = krep_sc[pl.ds(start, tk), :]
        k4 = jnp.concatenate([k2, k2], axis=1)
        v2 = v2_sc[pl.ds(start, tk), :]
        u = jnp.minimum((qi * tq - kj * tk) // tq, n_bias - 1)
        block_neg = jnp.where(jb == cb + kj * (tk // SEL_LEN), NEG_INF, 0.0).astype(BF16)
        drop_add = _dot(dropped, block_neg)
        for h in range(GROUP):
            s = _dot_nt(qs[h], k4) + bias_ref[0, u, h] + drop_add
            _online_update(s, v2, m_sc, acc_sc, h)
        return carry

    lax.fori_loop(0, hi, body, 0)
    o_ref[...] = _finish_group([acc_sc[h] for h in range(GROUP)]).astype(BF16)


def _gqa_specs(z, q_block, kv_block, tq, seq):
    nq = seq // tq
    qblk = q_block * LANE // GROUP_WIDTH
    return [
        pl.BlockSpec((tq, GROUP_WIDTH), lambda h, bb, i: (bb * nq + i, qblk + h)),
        pl.BlockSpec((seq, LANE), lambda h, bb, i: (bb, kv_block + h)),
    ]


def _resident(a):
    return pl.BlockSpec((1,) + a.shape[1:], lambda h, bb, i: (h,) + (0,) * (a.ndim - 1),
                        pipeline_mode=pl.Buffered(1))


def _gqa_out(m, tq, seq):
    nq = seq // tq
    return (pl.BlockSpec((tq, GROUP_WIDTH), lambda h, bb, i: (bb * nq + i, h)),
            jax.ShapeDtypeStruct((m, BRANCH_WIDTH), BF16))


def _kv_scratch(seq):
    return [pltpu.VMEM((seq, LANE), BF16), pltpu.VMEM((seq, 2 * LANE), BF16)]


def _band_attention(z, bias, *, q_block, kv_block, tq, seq, back, name, sinks=None):
    m = z.shape[0]
    b = m // seq
    n_off, w = bias.shape[1], bias.shape[4]
    assert w == back + tq and w <= seq and back % LANE == 0
    kern = functools.partial(_band_attn_kernel, tq=tq, w=w, back=back, n_off=n_off,
                             has_sink=sinks is not None)
    in_specs = _gqa_specs(z, q_block, kv_block, tq, seq) + [_resident(bias)]
    args = [z, z, bias]
    if sinks is not None:
        in_specs.append(pl.BlockSpec((1, GROUP, 1, 1), lambda h, bb, i: (h, 0, 0, 0)))
        args.append(sinks)
    out_spec, out_shape = _gqa_out(m, tq, seq)
    return pl.pallas_call(
        kern,
        grid=(N_KV, b, seq // tq),
        in_specs=in_specs,
        out_specs=out_spec,
        out_shape=out_shape,
        scratch_shapes=_kv_scratch(seq),
        compiler_params=_cparams(("parallel", "parallel", "arbitrary")),
        name=name,
    )(*args)


def _sel_attention(z, dropped, bias, *, tq, tk, seq):
    m = z.shape[0]
    b = m // seq
    nsel = dropped.shape[-1]
    kern = functools.partial(_sel_attn_kernel, tq=tq, tk=tk, n_bias=bias.shape[1])
    in_specs = _gqa_specs(z, ZB_NQ, ZB_SEL_KV, tq, seq) + [
        pl.BlockSpec((1, 1, tq, nsel), lambda h, bb, i: (bb, h, i, 0)),
        _resident(bias),
    ]
    out_spec, out_shape = _gqa_out(m, tq, seq)
    return pl.pallas_call(
        kern,
        grid=(N_KV, b, seq // tq),
        in_specs=in_specs,
        out_specs=out_spec,
        out_shape=out_shape,
        scratch_shapes=_kv_scratch(seq) + [pltpu.VMEM((GROUP, tq, LANE), F32),
                                           pltpu.VMEM((GROUP, tq, 2 * LANE), F32)],
        compiler_params=_cparams(("parallel", "parallel", "arbitrary")),
        name="attn_sel",
    )(z, z, dropped, bias)


def _rope_lanes(x, cos_t, sin_a, sin_b):
    half = MLA_ROPE // 2
    return x * cos_t + pltpu.roll(x, LANE - half, 1) * sin_a + pltpu.roll(x, half, 1) * sin_b


def _mla_proj_kernel(mq_ref, mkv_ref, kr_ref, cos_ref, sina_ref, sinb_ref, qn_ref, wq_ref,
                     kvn_ref, wkv_ref, oq_ref, ok_ref, ov_ref):
    cos_t, sin_a, sin_b = cos_ref[...], sina_ref[...], sinb_ref[...]
    width = N_HEADS * LANE
    ql = _rms(mq_ref[...].astype(F32), qn_ref[...]).astype(BF16)
    q = _dot(ql, wq_ref[...]) * (MLA_QK ** -0.5)
    kvl = _rms(mkv_ref[...].astype(F32), kvn_ref[...]).astype(BF16)
    kv = _dot(kvl, wkv_ref[...])
    kr = _rope_lanes(kr_ref[...].astype(F32), cos_t, sin_a, sin_b)
    lane = lax.broadcasted_iota(jnp.int32, kr.shape, 1)
    for h in range(N_HEADS):
        blk = slice(h * LANE, (h + 1) * LANE)
        oq_ref[:, blk] = _rope_lanes(q[:, blk], cos_t, sin_a, sin_b).astype(BF16)
        ok_ref[:, blk] = (kv[:, blk] + kr).astype(BF16)
        v = kv[:, width + h * LANE:width + (h + 1) * LANE]
        ov_ref[:, blk] = jnp.where(lane < MLA_V, v, 1.0).astype(BF16)


def _mla_proj(z, cos_t, sin_a, sin_b, q_norm, w_qb, kv_norm, w_kvb, *, tm, seq):
    m = z.shape[0]
    nt = seq // tm
    width = N_HEADS * LANE

    def full(a):
        return pl.BlockSpec(a.shape, lambda i: (0,) * a.ndim)

    tab = pl.BlockSpec((tm, LANE), lambda i: (i % nt, 0))
    out = jax.ShapeDtypeStruct((m, width), BF16)
    oblk = pl.BlockSpec((tm, width), lambda i: (i, 0))
    return pl.pallas_call(
        _mla_proj_kernel,
        grid=(m // tm,),
        in_specs=[
            pl.BlockSpec((tm, MLA_Q_RANK), lambda i: (i, ZB_MQ * LANE // MLA_Q_RANK)),
            pl.BlockSpec((tm, MLA_KV_RANK), lambda i: (i, ZB_MKV * LANE // MLA_KV_RANK)),
            pl.BlockSpec((tm, LANE), lambda i: (i, ZB_KR)),
            tab, tab, tab,
            full(q_norm), full(w_qb), full(kv_norm), full(w_kvb),
        ],
        out_specs=[oblk, oblk, oblk],
        out_shape=[out, out, out],
        compiler_params=_cparams(("parallel",)),
        name="mla_proj",
    )(z, z, z, cos_t, sin_a, sin_b, q_norm, w_qb, kv_norm, w_kvb)


def _mla_attn_kernel(q_ref, k_ref, v_ref, o_ref, m_sc, acc_sc, *, tq):
    qi = pl.program_id(2)
    n_chain = m_sc.shape[0]
    qs = [q_ref[:, c * LANE:(c + 1) * LANE] for c in range(n_chain)]
    m_sc[...] = jnp.full(m_sc.shape, NEG_INF, F32)
    acc_sc[...] = jnp.zeros(acc_sc.shape, F32)

    def step(kj, diagonal):
        start = pl.multiple_of(kj * tq, tq)
        for c in range(n_chain):
            blk = slice(c * LANE, (c + 1) * LANE)
            s = _dot_nt(qs[c], k_ref[pl.ds(start, tq), blk])
            if diagonal:
                row = lax.broadcasted_iota(jnp.int32, (tq, tq), 0)
                col = lax.broadcasted_iota(jnp.int32, (tq, tq), 1)
                s = jnp.where(col <= row, s, NEG_INF)
            _online_update(s, v_ref[pl.ds(start, tq), blk], m_sc, acc_sc, c)

    step(qi, True)

    def body(i, carry):
        step(qi - i, False)
        return carry

    lax.fori_loop(1, qi + 1, body, 0)
    for c in range(n_chain):
        acc = acc_sc[c]
        lane = lax.broadcasted_iota(jnp.int32, acc.shape, 1)
        den = jnp.where(lane < MLA_V, pltpu.roll(acc, MLA_V, 1), acc)
        o_ref[:, c * LANE:(c + 1) * LANE] = (acc / den).astype(BF16)


def _mla_attention(q, k, v1, *, tq, seq):
    m = q.shape[0]
    b = m // seq
    nq = seq // tq
    width = MLA_CHAINS * LANE
    slab = pl.BlockSpec((seq, width), lambda h, bb, i: (bb, h))
    tile = pl.BlockSpec((tq, width), lambda h, bb, i: (bb * nq + i, h))
    return pl.pallas_call(
        functools.partial(_mla_attn_kernel, tq=tq),
        grid=(N_HEADS // MLA_CHAINS, b, nq),
        in_specs=[tile, slab, slab],
        out_specs=tile,
        out_shape=jax.ShapeDtypeStruct(q.shape, BF16),
        scratch_shapes=[pltpu.VMEM((MLA_CHAINS, tq, LANE), F32),
                        pltpu.VMEM((MLA_CHAINS, tq, LANE), F32)],
        compiler_params=_cparams(("parallel", "parallel", "arbitrary")),
        name="attn_mla",
    )(q, k, v1)


def _merge_kernel(ocmp_ref, osel_ref, owin_ref, ob_ref, oc_ref, ng_ref, mg_ref, x_ref,
                  gexp_ref, wab_ref, wc_ref, wo_ref, gpost_ref, o_ref):
    ng = jax.nn.sigmoid(ng_ref[...].astype(F32))
    o_a = jnp.zeros(ocmp_ref.shape, F32)
    for n, ref in enumerate((ocmp_ref, osel_ref, owin_ref)):
        gate = jnp.dot(ng, gexp_ref[n], precision=lax.Precision.HIGHEST,
                       preferred_element_type=F32)
        o_a = o_a + gate * ref[...].astype(F32)
    ys = (_dot(o_a.astype(BF16), wab_ref[0]), _dot(ob_ref[...], wab_ref[1]),
          _dot(oc_ref[...], wc_ref[...]))
    mixed = jnp.zeros(x_ref.shape, F32)
    for n, y in enumerate(ys):
        gate = jax.nn.sigmoid(mg_ref[:, n * D_MODEL:(n + 1) * D_MODEL].astype(F32))
        mixed = mixed + gate * y
    out = _dot(mixed.astype(BF16), wo_ref[...])
    o_ref[...] = x_ref[...] + _rms(out, gpost_ref[...])


def _merge(o_cmp, o_sel, o_win, o_b, o_c, z, x, gexp, w_ab, w_c, w_out, g_post, *, tm):
    m, d = x.shape
    br = pl.BlockSpec((tm, o_cmp.shape[1]), lambda i: (i, 0))

    def full(a):
        return pl.BlockSpec(a.shape, lambda i: (0,) * a.ndim)

    return pl.pallas_call(
        _merge_kernel,
        grid=(m // tm,),
        in_specs=[
            br, br, br, br,
            pl.BlockSpec((tm, o_c.shape[1]), lambda i: (i, 0)),
            pl.BlockSpec((tm, LANE), lambda i: (i, ZB_NGATE)),
            pl.BlockSpec((tm, 3 * d), lambda i: (i, ZB_MGATE * LANE // (3 * d))),
            pl.BlockSpec((tm, d), lambda i: (i, 0)),
            full(gexp), full(w_ab), full(w_c), full(w_out), full(g_post),
        ],
        out_specs=pl.BlockSpec((tm, d), lambda i: (i, 0)),
        out_shape=jax.ShapeDtypeStruct((m, d), F32),
        compiler_params=_cparams(("parallel",)),
        name="merge",
    )(o_cmp, o_sel, o_win, o_b, o_c, z, z, x, gexp, w_ab, w_c, w_out, g_post)


def _t5_bucket(dist):
    max_exact = NUM_BUCKETS // 2
    d = jnp.maximum(dist, 0)
    df = jnp.maximum(d, 1).astype(F32)
    large = max_exact + (jnp.log(df / max_exact) / math.log(MAX_DISTANCE / max_exact)
                         * (NUM_BUCKETS - max_exact)).astype(jnp.int32)
    large = jnp.minimum(large, NUM_BUCKETS - 1)
    return jnp.where(d < max_exact, d, large)


def _toeplitz(fn, offs, tq, tk):
    ln = tq + tk - 1
    i = jnp.arange(ln + 1)
    shift = jnp.where(i < tk, -i, ln + 1 - i)
    ext = jnp.moveaxis(fn(jnp.asarray(offs)[:, None] + shift[None, :]), -1, 0)
    flat = jnp.tile(ext, (1, 1, tq))[:, :, :tq * ln]
    return flat.reshape(ext.shape[0], len(offs), tq, ln)[:, :, :, :tk]


def _bias_fn(table, window):
    def fn(dist):
        ok = dist >= 0
        if window is not None:
            ok = ok & (dist < window)
        return jnp.where(ok[..., None], table[_t5_bucket(dist)].astype(F32), NEG_INF)
    return fn


def _head_tiles(t):
    return t.reshape(N_KV, GROUP, *t.shape[1:]).transpose(0, 2, 1, 3, 4)


def _in_proj_weight(w_in):
    sizes = (512, 128, 128, 128, 128, 128, 128, 24, 512, 128, 128, MLA_Q_RANK, MLA_KV_RANK,
             MLA_ROPE, 3 * D_MODEL)
    offs = np.concatenate([[0], np.cumsum(sizes)])
    seg = [w_in[:, offs[i]:offs[i + 1]] for i in range(len(sizes))]
    (nq, nkc, nvc, nks, nvs, nkw, nvw, ngate, sq, sk, sv, mq, mkv, mkr, mgate) = seg
    d = w_in.shape[0]
    hd = HEAD_DIM

    def pad(a, left=0):
        return jnp.pad(a, ((0, 0), (left, LANE - left - a.shape[1])))

    def kv_blocks(k, v):
        return [jnp.concatenate([k[:, h * hd:(h + 1) * hd], v[:, h * hd:(h + 1) * hd]], axis=1)
                for h in range(N_KV)]

    cols = ([mq, pad(ngate), mkv, pad(mkr, MLA_NOPE), jnp.zeros((d, LANE), w_in.dtype), nq,
             nkc, nvc] + kv_blocks(nks, nvs) + kv_blocks(nkw, nvw) + [sq] + kv_blocks(sk, sv)
            + [mgate])
    w = jnp.concatenate(cols, axis=1)
    assert w.shape == (d, Z_WIDTH)
    return w.astype(BF16)


def _mla_weights(w_qb, w_kvb, w_c):
    def blocks(a):
        return jnp.pad(a, ((0, 0), (0, 0), (0, LANE - a.shape[2]))).reshape(a.shape[0], -1)

    wq = blocks(w_qb.reshape(MLA_Q_RANK, N_HEADS, MLA_QK))
    wkv = w_kvb.reshape(MLA_KV_RANK, N_HEADS, MLA_NOPE + MLA_V)
    wkv = jnp.concatenate([blocks(wkv[:, :, :MLA_NOPE]), blocks(wkv[:, :, MLA_NOPE:])], axis=1)
    wc = jnp.pad(w_c.reshape(N_HEADS, MLA_V, -1), ((0, 0), (0, LANE - MLA_V), (0, 0)))
    return wq.astype(BF16), wkv.astype(BF16), wc.reshape(N_HEADS * LANE, -1).astype(BF16)


def _gate_expand():
    e = np.zeros((3, LANE, BRANCH_WIDTH), np.float32)
    for n in range(3):
        for h in range(N_HEADS):
            e[n, h * 3 + n, h * HEAD_DIM:(h + 1) * HEAD_DIM] = 1.0
    return jnp.asarray(e)


def _rope_tables(pos):
    half = MLA_ROPE // 2
    inv_freq = ROPE_BASE ** (-jnp.arange(half, dtype=F32) / half)
    ang = pos.astype(F32)[:, None] * inv_freq[None, :]
    cos, sin = jnp.cos(ang), jnp.sin(ang)
    n = pos.shape[0]
    ones = jnp.ones((n, MLA_NOPE), F32)
    z16 = jnp.zeros((n, half), F32)
    tail = jnp.zeros((n, LANE - MLA_QK), F32)
    cos_t = jnp.concatenate([ones, cos, cos, tail], axis=1)
    sin_a = jnp.concatenate([0 * ones, -sin, z16, tail], axis=1)
    sin_b = jnp.concatenate([0 * ones, z16, sin, tail], axis=1)
    return cos_t, sin_a, sin_b


def kernel(x, rel_bias_table, ffn1_norm_pre, ffn1_w_gu, ffn1_w_down, ffn1_norm_post,
           mix_norm_pre, w_in, nsa_pe_k, nsa_w1_k, nsa_w2_k, nsa_pe_v, nsa_w1_v, nsa_w2_v,
           swa_sinks, mla_q_norm, mla_w_qb, mla_kv_norm, mla_w_kvb, w_branch, w_out,
           mix_norm_post, ffn2_norm_pre, ffn2_w_gu, ffn2_w_down, ffn2_norm_post):
    b, s, d = x.shape
    depth = w_in.shape[0]
    m = b * s
    tq = 256
    tk_sel = min(512, s)
    tq_mla = min(512, s)
    ncp = s // CMP_STRIDE
    nsel = s // SEL_LEN
    tm = min(512, m)
    tm_big = min(1024, s)

    pos = jnp.arange(s, dtype=jnp.int32)
    tab_a = rel_bias_table[:, :N_HEADS]
    tab_b = rel_bias_table[:, N_HEADS:]
    n_far = -(-(LAST_BUCKET_DIST + tk_sel - 1) // tq)
    n_sel_tiles = min(n_far + 1, s // tq)
    bias_sel = _head_tiles(_toeplitz(_bias_fn(tab_a, None), [u * tq for u in range(n_sel_tiles)],
                                     tq, tk_sel))
    win_offs = sorted({min(i * tq, NSA_WINDOW) for i in range(-(-NSA_WINDOW // tq) + 1)})
    bias_win = _head_tiles(_toeplitz(_bias_fn(tab_a, NSA_WINDOW), win_offs, tq, NSA_WINDOW + tq))
    swa_offs = sorted({min(i * tq, SWA_WINDOW) for i in range(-(-SWA_WINDOW // tq) + 1)})
    bias_swa = _head_tiles(_toeplitz(_bias_fn(tab_b, SWA_WINDOW), swa_offs, tq, SWA_WINDOW + tq))

    def cmp_bias_fn(e):
        r = jnp.arange(CMP_STRIDE)
        dist = e[..., None] * CMP_STRIDE + r - (CMP_LEN - 1)
        return tab_a[_t5_bucket(dist)].astype(F32).reshape(*e.shape, CMP_STRIDE * N_HEADS)

    bias_c = _toeplitz(cmp_bias_fn, [0], ncp, ncp).reshape(CMP_STRIDE, N_HEADS, ncp, ncp)
    bias_c = bias_c.transpose(1, 2, 0, 3).reshape(N_KV, GROUP, s, ncp)
    ci = jnp.arange(ncp)[None, :] * CMP_STRIDE
    sj = jnp.arange(nsel)[:, None] * SEL_LEN
    ovl_t = (jnp.maximum(jnp.minimum(ci + CMP_LEN, sj + SEL_LEN) - jnp.maximum(ci, sj), 0)
             .astype(F32) / CMP_LEN)
    cos_t, sin_a, sin_b = _rope_tables(pos)
    gexp = _gate_expand()
    chunk = CMP_STRIDE * HEAD_DIM

    xf = x.reshape(m, d)
    for l in range(depth):
        xf = _ffn(xf, ffn1_norm_pre[l][None], ffn1_w_gu[l].astype(BF16),
                  ffn1_w_down[l].astype(BF16), ffn1_norm_post[l][None], tm=tm, nf=2)

        z = _inproj(xf, mix_norm_pre[l][None], _in_proj_weight(w_in[l]), tm=tm_big, tn=Z_WIDTH // 4)

        def chunks(blk):
            a = z[:, blk * LANE:(blk + 1) * LANE].reshape(b, s, N_KV, HEAD_DIM)
            return a.transpose(0, 2, 1, 3).reshape(b, N_KV, ncp, chunk)

        kc, vc = _cmp_mlp(chunks(ZB_CMP_K), chunks(ZB_CMP_V),
                          nsa_pe_k[l].reshape(2, chunk), nsa_w1_k[l].astype(BF16),
                          nsa_w2_k[l].astype(BF16),
                          nsa_pe_v[l].reshape(2, chunk), nsa_w1_v[l].astype(BF16),
                          nsa_w2_v[l].astype(BF16))
        o_cmp, dropped = _cmp_attn(z, jnp.tile(kc, (1, 1, 1, GROUP)), jnp.tile(vc, (1, 1, 1, 2)),
                                   bias_c, ovl_t, tq=tq, seq=s)
        o_sel = _sel_attention(z, dropped, bias_sel, tq=tq, tk=tk_sel, seq=s)
        o_win = _band_attention(z, bias_win, q_block=ZB_NQ, kv_block=ZB_WIN_KV, tq=tq, seq=s,
                                back=NSA_WINDOW, name="attn_win")
        o_b = _band_attention(z, bias_swa, q_block=ZB_SQ, kv_block=ZB_SWA_KV, tq=tq, seq=s,
                              back=SWA_WINDOW, name="attn_swa",
                              sinks=swa_sinks[l].reshape(N_KV, GROUP, 1, 1))
        wq, wkv, wc = _mla_weights(mla_w_qb[l], mla_w_kvb[l], w_branch[l, 2])
        mq, mk, mv1 = _mla_proj(z, cos_t, sin_a, sin_b, mla_q_norm[l][None], wq,
                                mla_kv_norm[l][None], wkv, tm=tm_big, seq=s)
        o_c = _mla_attention(mq, mk, mv1, tq=tq_mla, seq=s)

        xf = _merge(o_cmp, o_sel, o_win, o_b, o_c, z, xf, gexp, w_branch[l, :2].astype(BF16), wc,
                    w_out[l].astype(BF16), mix_norm_post[l][None], tm=tm)

        xf = _ffn(xf, ffn2_norm_pre[l][None], ffn2_w_gu[l].astype(BF16),
                  ffn2_w_down[l].astype(BF16), ffn2_norm_post[l][None], tm=tm, nf=2)
    return xf.reshape(b, s, d)
```

```python
import functools
import math

import jax
import jax.numpy as jnp
import numpy as np
from jax import lax
from jax.experimental import pallas as pl
from jax.experimental.pallas import tpu as pltpu

F32 = jnp.float32
BF16 = jnp.bfloat16

D_MODEL = 1024
D_FF = 2816
HEAD_DIM = 64
NORM_EPS = 1e-6
NUM_BUCKETS = 32
MAX_DISTANCE = 1024
LAST_BUCKET_DIST = 1 + math.ceil(
    (NUM_BUCKETS // 2) * (MAX_DISTANCE / (NUM_BUCKETS // 2))
    ** ((NUM_BUCKETS // 2 - 1) / (NUM_BUCKETS - NUM_BUCKETS // 2)))
N_HEADS = 8
N_KV = 2
GROUP = N_HEADS // N_KV
CMP_LEN = 32
CMP_STRIDE = 16
CMP_HIDDEN = 128
SEL_LEN = 64
SEL_SHIFT = 6
SEL_TOPK = 16
NSA_WINDOW = 512
SWA_WINDOW = 128
MLA_Q_RANK = 384
MLA_KV_RANK = 256
MLA_NOPE = 64
MLA_ROPE = 32
MLA_V = 64
MLA_QK = MLA_NOPE + MLA_ROPE
ROPE_BASE = 10000.0
BRANCH_WIDTH = N_HEADS * HEAD_DIM
NEG_INF = -1e30
SEL_FORCE = 1e9
LOG2E = math.log2(math.e)
QK_SCALE = HEAD_DIM ** -0.5 * LOG2E

LANE = 128
VMEM_LIMIT = 56 * 1024 * 1024
HEAD_SHIFT = 6
GROUP_WIDTH = GROUP * HEAD_DIM
KV_SPLIT_CHUNK = 512
MLA_CHAINS = 4

ZB_MQ = 0
ZB_NGATE = 3
ZB_MKV = 4
ZB_KR = 6
ZB_NQ = 8
ZB_CMP_K = 12
ZB_CMP_V = 13
ZB_SEL_KV = 14
ZB_WIN_KV = 16
ZB_SQ = 18
ZB_SWA_KV = 22
ZB_MGATE = 24
Z_BLOCKS = 48
Z_WIDTH = Z_BLOCKS * LANE


def _cparams(sem):
    return pltpu.CompilerParams(dimension_semantics=sem, vmem_limit_bytes=VMEM_LIMIT)


def _rms(x, g):
    inv = lax.rsqrt(jnp.mean(x * x, axis=-1, keepdims=True) + NORM_EPS)
    return (x * inv) * g


def _dot(a, b):
    return jnp.dot(a, b, preferred_element_type=F32)


def _dot_nt(a, b, precision=None):
    return lax.dot_general(a, b, (((1,), (1,)), ((), ())), precision=precision,
                           preferred_element_type=F32)


def _ffn_kernel(x_ref, gpre_ref, wg_ref, wu_ref, wd_ref, gpost_ref, o_ref):
    x = x_ref[...]
    h = _rms(x, gpre_ref[...]).astype(BF16)
    g = _dot(h, wg_ref[...])
    u = _dot(h, wu_ref[...])
    act = (g * jax.nn.sigmoid(g)) * u
    y = _dot(act.astype(BF16), wd_ref[...])
    o_ref[...] = x + 0.5 * _rms(y, gpost_ref[...])


def _ffn(x, g_pre, w_gu, w_down, g_post, *, tm):
    m, d = x.shape
    once = pl.Buffered(1)
    return pl.pallas_call(
        _ffn_kernel,
        grid=(m // tm,),
        in_specs=[
            pl.BlockSpec((tm, d), lambda i: (i, 0)),
            pl.BlockSpec((1, d), lambda i: (0, 0)),
            pl.BlockSpec((d, D_FF), lambda i: (0, 0), pipeline_mode=once),
            pl.BlockSpec((d, D_FF), lambda i: (0, 1), pipeline_mode=once),
            pl.BlockSpec((D_FF, d), lambda i: (0, 0), pipeline_mode=once),
            pl.BlockSpec((1, d), lambda i: (0, 0)),
        ],
        out_specs=pl.BlockSpec((tm, d), lambda i: (i, 0)),
        out_shape=jax.ShapeDtypeStruct((m, d), F32),
        compiler_params=_cparams(("parallel",)),
        name="ffn",
    )(x, g_pre, w_gu, w_gu, w_down, g_post)


def _inproj_kernel(x_ref, g_ref, w_ref, o_ref, h_sc):
    @pl.when(pl.program_id(1) == 0)
    def _():
        h_sc[...] = _rms(x_ref[...], g_ref[...]).astype(BF16)

    o_ref[...] = _dot(h_sc[...], w_ref[...]).astype(BF16)


def _inproj(x, g, w, *, tm, tn):
    m, d = x.shape
    n = w.shape[1]
    return pl.pallas_call(
        _inproj_kernel,
        grid=(m // tm, n // tn),
        in_specs=[
            pl.BlockSpec((tm, d), lambda i, j: (i, 0)),
            pl.BlockSpec((1, d), lambda i, j: (0, 0)),
            pl.BlockSpec((d, tn), lambda i, j: (0, j)),
        ],
        out_specs=pl.BlockSpec((tm, tn), lambda i, j: (i, j)),
        out_shape=jax.ShapeDtypeStruct((m, n), BF16),
        scratch_shapes=[pltpu.VMEM((tm, d), BF16)],
        compiler_params=_cparams(("parallel", "arbitrary")),
        name="inproj",
    )(x, g, w)


def _cmp_mlp_kernel(k_ref, v_ref, pek_ref, w1k_ref, w2k_ref, pev_ref, w1v_ref, w2v_ref,
                    kc_ref, vc_ref):
    half = CMP_STRIDE * HEAD_DIM

    def one(x_ref, pe_ref, w1_ref, w2_ref, o_ref):
        x = x_ref[0, 0].astype(F32)
        ncp = x.shape[0]
        xa = (x + pe_ref[0:1, :]).astype(BF16)
        xb = (x + pe_ref[1:2, :]).astype(BF16)
        pa = _dot(xa, w1_ref[0:half, :])
        pb = _dot(xb, w1_ref[half:2 * half, :])
        hid = pa + pltpu.roll(pb, ncp - 1, 0)
        out = _dot(jax.nn.gelu(hid).astype(BF16), w2_ref[...])
        row = lax.broadcasted_iota(jnp.int32, out.shape, 0)
        o_ref[0, 0] = jnp.where(row < ncp - 1, out, 0.0).astype(BF16)

    one(k_ref, pek_ref, w1k_ref, w2k_ref, kc_ref)
    one(v_ref, pev_ref, w1v_ref, w2v_ref, vc_ref)


def _cmp_mlp(k4, v4, pek, w1k, w2k, pev, w1v, w2v):
    b, hkv, ncp, width = k4.shape
    blk = pl.BlockSpec((1, 1, ncp, width), lambda i, j: (i, j, 0, 0))
    oblk = pl.BlockSpec((1, 1, ncp, HEAD_DIM), lambda i, j: (i, j, 0, 0))

    def full(a):
        return pl.BlockSpec(a.shape, lambda i, j: (0,) * a.ndim)

    out = jax.ShapeDtypeStruct((b, hkv, ncp, HEAD_DIM), BF16)
    return pl.pallas_call(
        _cmp_mlp_kernel,
        grid=(b, hkv),
        in_specs=[blk, blk, full(pek), full(w1k), full(w2k), full(pev), full(w1v), full(w2v)],
        out_specs=[oblk, oblk],
        out_shape=[out, out],
        compiler_params=_cparams(("parallel", "parallel")),
        name="cmp_mlp",
    )(k4, v4, pek, w1k, w2k, pev, w1v, w2v)


def _keep_lanes(q, keep):
    group = lax.broadcasted_iota(jnp.int32, (1, q.shape[1]), 1) >> HEAD_SHIFT
    mask = functools.reduce(jnp.logical_or, [group == g for g, k in enumerate(keep) if k])
    return q * jnp.where(mask, 1.0, 0.0).astype(q.dtype)


def _head_rows(q_all):
    return [_keep_lanes(q_all, [g == h for g in range(GROUP)]) for h in range(GROUP)]


def _stack_heads(q_all, upper):
    q32 = q_all.astype(F32)
    lane = lax.broadcasted_iota(jnp.int32, upper.shape, 1)
    rows = []
    for pair in range(GROUP // 2):
        half = q32[:, pair * LANE:(pair + 1) * LANE]
        rows.append(jnp.where(lane < HEAD_DIM, half, upper))
        rows.append(jnp.where(lane < HEAD_DIM, pltpu.roll(half, HEAD_DIM, 1), upper))
    return jnp.concatenate(rows, axis=0).astype(BF16)


def _split_kv(kv_ref, kt_sc, v_sc, key_rows=None):
    seq = kv_ref.shape[0]
    chunk = min(KV_SPLIT_CHUNK, seq)
    copies = kt_sc.shape[0] // HEAD_DIM - (key_rows is not None)
    lane = lax.broadcasted_iota(jnp.int32, (chunk, LANE), 1)
    for c in range(seq // chunk):
        rows = slice(c * chunk, (c + 1) * chunk)
        x = kv_ref[rows, :].astype(F32)
        kt = x.T[0:HEAD_DIM, :].astype(BF16)
        for r in range(copies):
            kt_sc[r * HEAD_DIM:(r + 1) * HEAD_DIM, rows] = kt
        v_sc[rows, :] = jnp.where(lane < HEAD_DIM, pltpu.roll(x, HEAD_DIM, 1), 1.0).astype(BF16)
    if key_rows is not None:
        kt_sc[copies * HEAD_DIM:(copies + 1) * HEAD_DIM, :] = key_rows


def _finish_pair_rolled(acc_even, acc_odd):
    lane = lax.broadcasted_iota(jnp.int32, acc_even.shape, 1)
    even = acc_even / pltpu.roll(acc_even, HEAD_DIM, 1)
    odd = pltpu.roll(acc_odd, HEAD_DIM, 1) / acc_odd
    return jnp.where(lane < HEAD_DIM, even, odd)


def _finish_group(accs):
    return jnp.concatenate([_finish_pair_rolled(accs[0], accs[1]),
                            _finish_pair_rolled(accs[2], accs[3])], axis=1)


def _online_update(s, v, m_ref, acc_ref, idx):
    m_old = m_ref[idx]
    m_new = jnp.maximum(m_old, jnp.max(s, axis=-1, keepdims=True))
    alpha = jnp.exp2(m_old - m_new)
    p = jnp.exp2(s - jnp.concatenate([m_new] * (s.shape[1] // LANE), axis=1))
    acc = acc_ref[idx]
    acc_ref[idx] = (jnp.concatenate([alpha] * (acc.shape[1] // LANE), axis=1) * acc
                    + _dot(p.astype(BF16), v))
    m_ref[idx] = m_new


def _cmp_attn_kernel(q_ref, kc_ref, vc_ref, bias_ref, ovl_ref, o_ref, sel_ref, *, tq, ncp, nsel):
    qi = pl.program_id(1)
    kc = kc_ref[0, 0]
    vc = vc_ref[0, 0]
    lane = lax.broadcasted_iota(jnp.int32, (tq, LANE), 1)
    row = lax.broadcasted_iota(jnp.int32, (tq, ncp), 0) + qi * tq
    col = lax.broadcasted_iota(jnp.int32, (tq, ncp), 1)
    valid = (col * CMP_STRIDE + (CMP_LEN - 1)) <= row
    p_sum = jnp.zeros((tq, ncp), F32)
    outs = []
    for g, q in enumerate(_head_rows(q_ref[...])):
        s = _dot_nt(q, kc) + bias_ref[0, g]
        s = jnp.where(valid, s, NEG_INF)
        e = jnp.where(valid, jnp.exp2(s - jnp.max(s, axis=-1, keepdims=True)), 0.0)
        den = jnp.sum(e, axis=-1, keepdims=True)
        p = e / jnp.where(den > 0, den, 1.0)
        outs.append(_dot(p.astype(BF16), vc))
        p_sum = p_sum + p
    o_ref[...] = jnp.concatenate([jnp.where(lane < HEAD_DIM, outs[0], outs[1]),
                                  jnp.where(lane < HEAD_DIM, outs[2], outs[3])],
                                 axis=1).astype(BF16)

    imp = _dot_nt(ovl_ref[...], p_sum, precision=lax.Precision.HIGHEST)
    jj = lax.broadcasted_iota(jnp.int32, (nsel, tq), 0)
    cur = (lax.broadcasted_iota(jnp.int32, (nsel, tq), 1) + qi * tq) >> SEL_SHIFT
    forced = (jj == 0) | (jj == cur) | (jj == cur - 1)
    score = jnp.where(forced, SEL_FORCE, jnp.where(jj <= cur, imp, -SEL_FORCE))
    rank = jnp.zeros((nsel, tq), F32)
    for i in range(nsel):
        si = score[i:i + 1, :]
        ge = jnp.where(si >= score, 1.0, 0.0)
        gt = jnp.where(si > score, 1.0, 0.0)
        rank = rank + jnp.where(jj > i, ge, gt)
    dropped = jnp.where(rank < min(SEL_TOPK, nsel), 0.0, 1.0).astype(BF16)
    pieces = [jnp.zeros((HEAD_DIM, tq), BF16), dropped]
    if nsel < LANE - HEAD_DIM:
        pieces.append(jnp.zeros((LANE - HEAD_DIM - nsel, tq), BF16))
    dropped = jnp.concatenate(pieces, axis=0)
    eye = (lax.broadcasted_iota(jnp.int32, (tq, tq), 0)
           == lax.broadcasted_iota(jnp.int32, (tq, tq), 1))
    sel_ref[0, 0] = _dot_nt(jnp.where(eye, 1.0, 0.0).astype(BF16), dropped).astype(BF16)


def _cmp_attn(z, kc4, vc, bias_c, ovl_t, *, tq, seq):
    b, hkv, ncp, _ = kc4.shape
    m = z.shape[0]
    nq = seq // tq
    nsel = ovl_t.shape[0]
    kern = functools.partial(_cmp_attn_kernel, tq=tq, ncp=ncp, nsel=nsel)
    qblk = ZB_NQ * LANE // GROUP_WIDTH
    return pl.pallas_call(
        kern,
        grid=(hkv, nq, b),
        in_specs=[
            pl.BlockSpec((tq, GROUP_WIDTH), lambda h, i, bb: (bb * nq + i, qblk + h)),
            pl.BlockSpec((1, 1, ncp, GROUP_WIDTH), lambda h, i, bb: (bb, h, 0, 0)),
            pl.BlockSpec((1, 1, ncp, LANE), lambda h, i, bb: (bb, h, 0, 0)),
            pl.BlockSpec((1, GROUP, tq, ncp), lambda h, i, bb: (h, 0, i, 0)),
            pl.BlockSpec((nsel, ncp), lambda h, i, bb: (0, 0)),
        ],
        out_specs=[
            pl.BlockSpec((tq, GROUP_WIDTH), lambda h, i, bb: (bb * nq + i, h)),
            pl.BlockSpec((1, 1, tq, LANE), lambda h, i, bb: (bb, h, i, 0)),
        ],
        out_shape=[
            jax.ShapeDtypeStruct((m, BRANCH_WIDTH), BF16),
            jax.ShapeDtypeStruct((b, hkv, seq, LANE), BF16),
        ],
        compiler_params=_cparams(("parallel", "parallel", "parallel")),
        name="cmp_attn",
    )(z, kc4, vc, bias_c, ovl_t)


def _band_attn_kernel(*refs, tq, w, back, n_off, has_sink):
    if has_sink:
        q_ref, kv_ref, bias_ref, sink_ref, o_ref, kt_sc, v_sc = refs
    else:
        q_ref, kv_ref, bias_ref, o_ref, kt_sc, v_sc = refs
    qi = pl.program_id(2)

    @pl.when(qi == 0)
    def _():
        _split_kv(kv_ref, kt_sc, v_sc)

    off = jnp.minimum(qi * tq, back)
    start = pl.multiple_of(qi * tq - off, LANE)
    q = jnp.concatenate(_head_rows(q_ref[...]), axis=0)
    s_all = _dot(q, kt_sc[:, pl.ds(start, w)])
    v1 = v_sc[pl.ds(start, w), :]
    tile = jnp.minimum(qi, n_off - 1)
    sum_lane = lax.broadcasted_iota(jnp.int32, (tq, LANE), 1) >= HEAD_DIM
    accs = []
    for h in range(GROUP):
        s = s_all[h * tq:(h + 1) * tq] + bias_ref[0, tile, h]
        m = jnp.broadcast_to(jnp.max(s, axis=-1, keepdims=True), (tq, LANE))
        if has_sink:
            sink = sink_ref[0, h]
            m = jnp.maximum(m, sink)
        p = jnp.exp2(s - jnp.concatenate([m] * (w // LANE), axis=1))
        acc = _dot(p.astype(BF16), v1)
        if has_sink:
            acc = acc + jnp.where(sum_lane, jnp.exp2(sink - m), 0.0)
        accs.append(acc)
    o_ref[...] = _finish_group(accs).astype(BF16)


def _sel_attn_kernel(q_ref, kv_ref, drop_ref, rows_ref, bias_ref, o_ref, kt_sc, v_sc, m_sc, acc_sc,
                     *, tq, tk, n_bias):
    qi = pl.program_id(2)

    @pl.when(qi == 0)
    def _():
        _split_kv(kv_ref, kt_sc, v_sc, rows_ref[...])

    q = _stack_heads(q_ref[...], drop_ref[0, 0].astype(F32))
    hi = ((qi + 1) * tq + tk - 1) // tk
    m_sc[...] = jnp.full(m_sc.shape, NEG_INF, F32)
    acc_sc[...] = jnp.zeros(acc_sc.shape, F32)

    def step(kj):
        start = pl.multiple_of(kj * tk, tk)
        s_all = _dot(q, kt_sc[:, pl.ds(start, tk)])
        v1 = v_sc[pl.ds(start, tk), :]
        u = jnp.minimum((qi * tq - kj * tk) // tq, n_bias - 1)
        for h in range(GROUP):
            s = s_all[h * tq:(h + 1) * tq] + bias_ref[0, u, h]
            _online_update(s, v1, m_sc, acc_sc, h)

    odd = hi % 2

    @pl.when(odd == 1)
    def _():
        step(hi - 1)

    def body(i, carry):
        kj = hi - odd - 1 - 2 * i
        step(kj)
        step(kj - 1)
        return carry

    lax.fori_loop(0, hi // 2, body, 0)
    o_ref[...] = _finish_group([acc_sc[h] for h in range(GROUP)]).astype(BF16)


def _gqa_specs(z, q_block, kv_block, tq, seq):
    nq = seq // tq
    qblk = q_block * LANE // GROUP_WIDTH
    return [
        pl.BlockSpec((tq, GROUP_WIDTH), lambda h, bb, i: (bb * nq + i, qblk + h)),
        pl.BlockSpec((seq, LANE), lambda h, bb, i: (bb, kv_block + h)),
    ]


def _resident(a):
    return pl.BlockSpec((1,) + a.shape[1:], lambda h, bb, i: (h,) + (0,) * (a.ndim - 1),
                        pipeline_mode=pl.Buffered(1))


def _gqa_out(m, tq, seq):
    nq = seq // tq
    return (pl.BlockSpec((tq, GROUP_WIDTH), lambda h, bb, i: (bb * nq + i, h)),
            jax.ShapeDtypeStruct((m, BRANCH_WIDTH), BF16))


def _kv_scratch(seq, key_rows):
    return [pltpu.VMEM((key_rows, seq), BF16), pltpu.VMEM((seq, LANE), BF16)]


def _band_attention(z, bias, *, q_block, kv_block, tq, seq, back, name, sinks=None):
    m = z.shape[0]
    b = m // seq
    n_off, w = bias.shape[1], bias.shape[4]
    assert w == back + tq and w <= seq and back % LANE == 0
    kern = functools.partial(_band_attn_kernel, tq=tq, w=w, back=back, n_off=n_off,
                             has_sink=sinks is not None)
    in_specs = _gqa_specs(z, q_block, kv_block, tq, seq) + [_resident(bias)]
    args = [z, z, bias]
    if sinks is not None:
        in_specs.append(pl.BlockSpec((1, GROUP, 1, 1), lambda h, bb, i: (h, 0, 0, 0)))
        args.append(sinks)
    out_spec, out_shape = _gqa_out(m, tq, seq)
    return pl.pallas_call(
        kern,
        grid=(N_KV, b, seq // tq),
        in_specs=in_specs,
        out_specs=out_spec,
        out_shape=out_shape,
        scratch_shapes=_kv_scratch(seq, GROUP_WIDTH),
        compiler_params=_cparams(("parallel", "parallel", "arbitrary")),
        name=name,
    )(*args)


def _sel_attention(z, dropped, sel_rows, bias, *, tq, tk, seq):
    m = z.shape[0]
    b = m // seq
    kern = functools.partial(_sel_attn_kernel, tq=tq, tk=tk, n_bias=bias.shape[1])
    in_specs = _gqa_specs(z, ZB_NQ, ZB_SEL_KV, tq, seq) + [
        pl.BlockSpec((1, 1, tq, LANE), lambda h, bb, i: (bb, h, i, 0)),
        pl.BlockSpec((HEAD_DIM, seq), lambda h, bb, i: (0, 0), pipeline_mode=pl.Buffered(1)),
        _resident(bias),
    ]
    out_spec, out_shape = _gqa_out(m, tq, seq)
    return pl.pallas_call(
        kern,
        grid=(N_KV, b, seq // tq),
        in_specs=in_specs,
        out_specs=out_spec,
        out_shape=out_shape,
        scratch_shapes=_kv_scratch(seq, LANE) + [pltpu.VMEM((GROUP, tq, LANE), F32),
                                                 pltpu.VMEM((GROUP, tq, LANE), F32)],
        compiler_params=_cparams(("parallel", "parallel", "arbitrary")),
        name="attn_sel",
    )(z, z, dropped, sel_rows, bias)


def _rope_lanes(x, cos_t, sin_a, sin_b):
    half = MLA_ROPE // 2
    return x * cos_t + pltpu.roll(x, LANE - half, 1) * sin_a + pltpu.roll(x, half, 1) * sin_b


def _mla_proj_kernel(mq_ref, mkv_ref, kr_ref, cos_ref, sina_ref, sinb_ref, qn_ref, wq_ref,
                     kvn_ref, wkv_ref, oq_ref, ok_ref, ov_ref):
    cos_t, sin_a, sin_b = cos_ref[...], sina_ref[...], sinb_ref[...]
    width = N_HEADS * LANE
    ql = _rms(mq_ref[...].astype(F32), qn_ref[...]).astype(BF16)
    q = _dot(ql, wq_ref[...]) * (MLA_QK ** -0.5 * LOG2E)
    kvl = _rms(mkv_ref[...].astype(F32), kvn_ref[...]).astype(BF16)
    kv = _dot(kvl, wkv_ref[...])
    kr = _rope_lanes(kr_ref[...].astype(F32), cos_t, sin_a, sin_b)
    lane = lax.broadcasted_iota(jnp.int32, kr.shape, 1)
    for h in range(N_HEADS):
        blk = slice(h * LANE, (h + 1) * LANE)
        oq_ref[:, blk] = _rope_lanes(q[:, blk], cos_t, sin_a, sin_b).astype(BF16)
        ok_ref[:, blk] = (kv[:, blk] + kr).astype(BF16)
        v = kv[:, width + h * LANE:width + (h + 1) * LANE]
        ov_ref[:, blk] = jnp.where(lane < MLA_V, v, 1.0).astype(BF16)


def _mla_proj(z, cos_t, sin_a, sin_b, q_norm, w_qb, kv_norm, w_kvb, *, tm, seq):
    m = z.shape[0]
    nt = seq // tm
    width = N_HEADS * LANE

    def full(a):
        return pl.BlockSpec(a.shape, lambda i: (0,) * a.ndim)

    tab = pl.BlockSpec((tm, LANE), lambda i: (i % nt, 0))
    out = jax.ShapeDtypeStruct((m, width), BF16)
    oblk = pl.BlockSpec((tm, width), lambda i: (i, 0))
    return pl.pallas_call(
        _mla_proj_kernel,
        grid=(m // tm,),
        in_specs=[
            pl.BlockSpec((tm, MLA_Q_RANK), lambda i: (i, ZB_MQ * LANE // MLA_Q_RANK)),
            pl.BlockSpec((tm, MLA_KV_RANK), lambda i: (i, ZB_MKV * LANE // MLA_KV_RANK)),
            pl.BlockSpec((tm, LANE), lambda i: (i, ZB_KR)),
            tab, tab, tab,
            full(q_norm), full(w_qb), full(kv_norm), full(w_kvb),
        ],
        out_specs=[oblk, oblk, oblk],
        out_shape=[out, out, out],
        compiler_params=_cparams(("parallel",)),
        name="mla_proj",
    )(z, z, z, cos_t, sin_a, sin_b, q_norm, w_qb, kv_norm, w_kvb)


def _mla_attn_kernel(q_ref, k_ref, v_ref, o_ref, m_sc, acc_sc, *, tq):
    qi = pl.program_id(2)
    n_chain = m_sc.shape[0]
    qs = [q_ref[:, c * LANE:(c + 1) * LANE] for c in range(n_chain)]
    m_sc[...] = jnp.full(m_sc.shape, NEG_INF, F32)
    acc_sc[...] = jnp.zeros(acc_sc.shape, F32)

    def step(tile, width, diagonal):
        start = pl.multiple_of(tile * tq, tq)
        for c in range(n_chain):
            blk = slice(c * LANE, (c + 1) * LANE)
            s = _dot_nt(qs[c], k_ref[pl.ds(start, width * tq), blk])
            if diagonal:
                row = lax.broadcasted_iota(jnp.int32, s.shape, 0)
                col = lax.broadcasted_iota(jnp.int32, s.shape, 1)
                s = jnp.where(col <= row, s, NEG_INF)
            _online_update(s, v_ref[pl.ds(start, width * tq), blk], m_sc, acc_sc, c)

    step(qi, 1, True)
    odd = qi % 2

    @pl.when(odd == 1)
    def _():
        step(qi - 1, 1, False)

    def body(i, carry):
        step(qi - odd - 2 * (i + 1), 2, False)
        return carry

    lax.fori_loop(0, qi // 2, body, 0)
    for c in range(n_chain):
        acc = acc_sc[c]
        lane = lax.broadcasted_iota(jnp.int32, acc.shape, 1)
        den = jnp.where(lane < MLA_V, pltpu.roll(acc, MLA_V, 1), acc)
        o_ref[:, c * LANE:(c + 1) * LANE] = (acc / den).astype(BF16)


def _mla_attention(q, k, v1, *, tq, seq):
    m = q.shape[0]
    b = m // seq
    nq = seq // tq
    width = MLA_CHAINS * LANE
    slab = pl.BlockSpec((seq, width), lambda h, bb, i: (bb, h))
    tile = pl.BlockSpec((tq, width), lambda h, bb, i: (bb * nq + i, h))
    return pl.pallas_call(
        functools.partial(_mla_attn_kernel, tq=tq),
        grid=(N_HEADS // MLA_CHAINS, b, nq),
        in_specs=[tile, slab, slab],
        out_specs=tile,
        out_shape=jax.ShapeDtypeStruct(q.shape, BF16),
        scratch_shapes=[pltpu.VMEM((MLA_CHAINS, tq, LANE), F32),
                        pltpu.VMEM((MLA_CHAINS, tq, LANE), F32)],
        compiler_params=_cparams(("parallel", "parallel", "arbitrary")),
        name="attn_mla",
    )(q, k, v1)


def _merge_kernel(ocmp_ref, osel_ref, owin_ref, ob_ref, oc_ref, ng_ref, mg_ref, x_ref,
                  gexp_ref, wab_ref, wc_ref, wo_ref, gpost_ref, o_ref):
    ng = jax.nn.sigmoid(ng_ref[...].astype(F32))
    o_a = jnp.zeros(ocmp_ref.shape, F32)
    for n, ref in enumerate((ocmp_ref, osel_ref, owin_ref)):
        gate = jnp.dot(ng, gexp_ref[n], precision=lax.Precision.HIGHEST,
                       preferred_element_type=F32)
        o_a = o_a + gate * ref[...].astype(F32)
    ys = (_dot(o_a.astype(BF16), wab_ref[0]), _dot(ob_ref[...], wab_ref[1]),
          _dot(oc_ref[...], wc_ref[...]))
    mixed = jnp.zeros(x_ref.shape, F32)
    for n, y in enumerate(ys):
        gate = jax.nn.sigmoid(mg_ref[:, n * D_MODEL:(n + 1) * D_MODEL].astype(F32))
        mixed = mixed + gate * y
    out = _dot(mixed.astype(BF16), wo_ref[...])
    o_ref[...] = x_ref[...] + _rms(out, gpost_ref[...])


def _merge(o_cmp, o_sel, o_win, o_b, o_c, z, x, gexp, w_ab, w_c, w_out, g_post, *, tm):
    m, d = x.shape
    br = pl.BlockSpec((tm, o_cmp.shape[1]), lambda i: (i, 0))

    def full(a):
        return pl.BlockSpec(a.shape, lambda i: (0,) * a.ndim)

    return pl.pallas_call(
        _merge_kernel,
        grid=(m // tm,),
        in_specs=[
            br, br, br, br,
            pl.BlockSpec((tm, o_c.shape[1]), lambda i: (i, 0)),
            pl.BlockSpec((tm, LANE), lambda i: (i, ZB_NGATE)),
            pl.BlockSpec((tm, 3 * d), lambda i: (i, ZB_MGATE * LANE // (3 * d))),
            pl.BlockSpec((tm, d), lambda i: (i, 0)),
            full(gexp), full(w_ab), full(w_c), full(w_out), full(g_post),
        ],
        out_specs=pl.BlockSpec((tm, d), lambda i: (i, 0)),
        out_shape=jax.ShapeDtypeStruct((m, d), F32),
        compiler_params=_cparams(("parallel",)),
        name="merge",
    )(o_cmp, o_sel, o_win, o_b, o_c, z, z, x, gexp, w_ab, w_c, w_out, g_post)


def _t5_bucket(dist):
    max_exact = NUM_BUCKETS // 2
    d = jnp.maximum(dist, 0)
    df = jnp.maximum(d, 1).astype(F32)
    large = max_exact + (jnp.log(df / max_exact) / math.log(MAX_DISTANCE / max_exact)
                         * (NUM_BUCKETS - max_exact)).astype(jnp.int32)
    large = jnp.minimum(large, NUM_BUCKETS - 1)
    return jnp.where(d < max_exact, d, large)


def _toeplitz(fn, offs, tq, tk):
    ln = tq + tk - 1
    i = jnp.arange(ln + 1)
    shift = jnp.where(i < tk, -i, ln + 1 - i)
    ext = jnp.moveaxis(fn(jnp.asarray(offs)[:, None] + shift[None, :]), -1, 0)
    flat = jnp.tile(ext, (1, 1, tq))[:, :, :tq * ln]
    return flat.reshape(ext.shape[0], len(offs), tq, ln)[:, :, :, :tk]


def _bias_fn(table, window):
    def fn(dist):
        ok = dist >= 0
        if window is not None:
            ok = ok & (dist < window)
        return jnp.where(ok[..., None], table[_t5_bucket(dist)].astype(F32), NEG_INF)
    return fn


def _head_tiles(t):
    return t.reshape(N_KV, GROUP, *t.shape[1:]).transpose(0, 2, 1, 3, 4)


def _in_proj_weight(w_in):
    sizes = (512, 128, 128, 128, 128, 128, 128, 24, 512, 128, 128, MLA_Q_RANK, MLA_KV_RANK,
             MLA_ROPE, 3 * D_MODEL)
    offs = np.concatenate([[0], np.cumsum(sizes)])
    seg = [w_in[:, offs[i]:offs[i + 1]] for i in range(len(sizes))]
    (nq, nkc, nvc, nks, nvs, nkw, nvw, ngate, sq, sk, sv, mq, mkv, mkr, mgate) = seg
    d = w_in.shape[0]
    hd = HEAD_DIM

    def pad(a, left=0):
        return jnp.pad(a, ((0, 0), (left, LANE - left - a.shape[1])))

    def kv_blocks(k, v):
        return [jnp.concatenate([k[:, h * hd:(h + 1) * hd], v[:, h * hd:(h + 1) * hd]], axis=1)
                for h in range(N_KV)]

    cols = ([mq, pad(ngate), mkv, pad(mkr, MLA_NOPE), jnp.zeros((d, LANE), w_in.dtype),
             nq * QK_SCALE, nkc, nvc] + kv_blocks(nks, nvs) + kv_blocks(nkw, nvw)
            + [sq * QK_SCALE] + kv_blocks(sk, sv) + [mgate])
    w = jnp.concatenate(cols, axis=1)
    assert w.shape == (d, Z_WIDTH)
    return w.astype(BF16)


def _mla_weights(w_qb, w_kvb, w_c):
    def blocks(a):
        return jnp.pad(a, ((0, 0), (0, 0), (0, LANE - a.shape[2]))).reshape(a.shape[0], -1)

    wq = blocks(w_qb.reshape(MLA_Q_RANK, N_HEADS, MLA_QK))
    wkv = w_kvb.reshape(MLA_KV_RANK, N_HEADS, MLA_NOPE + MLA_V)
    wkv = jnp.concatenate([blocks(wkv[:, :, :MLA_NOPE]), blocks(wkv[:, :, MLA_NOPE:])], axis=1)
    wc = jnp.pad(w_c.reshape(N_HEADS, MLA_V, -1), ((0, 0), (0, LANE - MLA_V), (0, 0)))
    return wq.astype(BF16), wkv.astype(BF16), wc.reshape(N_HEADS * LANE, -1).astype(BF16)


def _gate_expand():
    e = np.zeros((3, LANE, BRANCH_WIDTH), np.float32)
    for n in range(3):
        for h in range(N_HEADS):
            e[n, h * 3 + n, h * HEAD_DIM:(h + 1) * HEAD_DIM] = 1.0
    return jnp.asarray(e)


def _rope_tables(pos):
    half = MLA_ROPE // 2
    inv_freq = ROPE_BASE ** (-jnp.arange(half, dtype=F32) / half)
    ang = pos.astype(F32)[:, None] * inv_freq[None, :]
    cos, sin = jnp.cos(ang), jnp.sin(ang)
    n = pos.shape[0]
    ones = jnp.ones((n, MLA_NOPE), F32)
    z16 = jnp.zeros((n, half), F32)
    tail = jnp.zeros((n, LANE - MLA_QK), F32)
    cos_t = jnp.concatenate([ones, cos, cos, tail], axis=1)
    sin_a = jnp.concatenate([0 * ones, -sin, z16, tail], axis=1)
    sin_b = jnp.concatenate([0 * ones, z16, sin, tail], axis=1)
    return cos_t, sin_a, sin_b


def kernel(x, rel_bias_table, ffn1_norm_pre, ffn1_w_gu, ffn1_w_down, ffn1_norm_post,
           mix_norm_pre, w_in, nsa_pe_k, nsa_w1_k, nsa_w2_k, nsa_pe_v, nsa_w1_v, nsa_w2_v,
           swa_sinks, mla_q_norm, mla_w_qb, mla_kv_norm, mla_w_kvb, w_branch, w_out,
           mix_norm_post, ffn2_norm_pre, ffn2_w_gu, ffn2_w_down, ffn2_norm_post):
    b, s, d = x.shape
    depth = w_in.shape[0]
    m = b * s
    tq = 256
    tk_sel = min(512, s)
    tq_mla = min(512, s)
    ncp = s // CMP_STRIDE
    nsel = s // SEL_LEN
    tm = min(512, m)
    tm_big = min(1024, s)

    pos = jnp.arange(s, dtype=jnp.int32)
    tab_a = rel_bias_table[:, :N_HEADS] * LOG2E
    tab_b = rel_bias_table[:, N_HEADS:] * LOG2E
    n_far = -(-(LAST_BUCKET_DIST + tk_sel - 1) // tq)
    n_sel_tiles = min(n_far + 1, s // tq)
    bias_sel = _head_tiles(_toeplitz(_bias_fn(tab_a, None), [u * tq for u in range(n_sel_tiles)],
                                     tq, tk_sel))
    win_offs = sorted({min(i * tq, NSA_WINDOW) for i in range(-(-NSA_WINDOW // tq) + 1)})
    bias_win = _head_tiles(_toeplitz(_bias_fn(tab_a, NSA_WINDOW), win_offs, tq, NSA_WINDOW + tq))
    swa_offs = sorted({min(i * tq, SWA_WINDOW) for i in range(-(-SWA_WINDOW // tq) + 1)})
    bias_swa = _head_tiles(_toeplitz(_bias_fn(tab_b, SWA_WINDOW), swa_offs, tq, SWA_WINDOW + tq))

    def cmp_bias_fn(e):
        r = jnp.arange(CMP_STRIDE)
        dist = e[..., None] * CMP_STRIDE + r - (CMP_LEN - 1)
        return tab_a[_t5_bucket(dist)].astype(F32).reshape(*e.shape, CMP_STRIDE * N_HEADS)

    bias_c = _toeplitz(cmp_bias_fn, [0], ncp, ncp).reshape(CMP_STRIDE, N_HEADS, ncp, ncp)
    bias_c = bias_c.transpose(1, 2, 0, 3).reshape(N_KV, GROUP, s, ncp)
    ci = jnp.arange(ncp)[None, :] * CMP_STRIDE
    sj = jnp.arange(nsel)[:, None] * SEL_LEN
    ovl_t = (jnp.maximum(jnp.minimum(ci + CMP_LEN, sj + SEL_LEN) - jnp.maximum(ci, sj), 0)
             .astype(F32) / CMP_LEN)
    cos_t, sin_a, sin_b = _rope_tables(pos)
    gexp = _gate_expand()
    assert nsel <= HEAD_DIM
    sel_rows = jnp.where(jnp.arange(HEAD_DIM)[:, None] == (pos[None, :] >> SEL_SHIFT),
                         NEG_INF, 0.0).astype(BF16)
    chunk = CMP_STRIDE * HEAD_DIM

    xf = x.reshape(m, d)
    for l in range(depth):
        xf = _ffn(xf, ffn1_norm_pre[l][None], ffn1_w_gu[l].astype(BF16),
                  ffn1_w_down[l].astype(BF16), ffn1_norm_post[l][None], tm=tm)

        z = _inproj(xf, mix_norm_pre[l][None], _in_proj_weight(w_in[l]), tm=tm_big, tn=Z_WIDTH // 4)

        def chunks(blk):
            a = z[:, blk * LANE:(blk + 1) * LANE].reshape(b, s, N_KV, HEAD_DIM)
            return a.transpose(0, 2, 1, 3).reshape(b, N_KV, ncp, chunk)

        kc, vc = _cmp_mlp(chunks(ZB_CMP_K), chunks(ZB_CMP_V),
                          nsa_pe_k[l].reshape(2, chunk), nsa_w1_k[l].astype(BF16),
                          nsa_w2_k[l].astype(BF16),
                          nsa_pe_v[l].reshape(2, chunk), nsa_w1_v[l].astype(BF16),
                          nsa_w2_v[l].astype(BF16))
        o_cmp, dropped = _cmp_attn(z, jnp.tile(kc, (1, 1, 1, GROUP)), jnp.tile(vc, (1, 1, 1, 2)),
                                   bias_c, ovl_t, tq=tq, seq=s)
        o_sel = _sel_attention(z, dropped, sel_rows, bias_sel, tq=tq, tk=tk_sel, seq=s)
        o_win = _band_attention(z, bias_win, q_block=ZB_NQ, kv_block=ZB_WIN_KV, tq=tq, seq=s,
                                back=NSA_WINDOW, name="attn_win")
        o_b = _band_attention(z, bias_swa, q_block=ZB_SQ, kv_block=ZB_SWA_KV, tq=tq, seq=s,
                              back=SWA_WINDOW, name="attn_swa",
                              sinks=(swa_sinks[l] * LOG2E).reshape(N_KV, GROUP, 1, 1))
        wq, wkv, wc = _mla_weights(mla_w_qb[l], mla_w_kvb[l], w_branch[l, 2])
        mq, mk, mv1 = _mla_proj(z, cos_t, sin_a, sin_b, mla_q_norm[l][None], wq,
                                mla_kv_norm[l][None], wkv, tm=tm_big, seq=s)
        o_c = _mla_attention(mq, mk, mv1, tq=tq_mla, seq=s)

        xf = _merge(o_cmp, o_sel, o_win, o_b, o_c, z, xf, gexp, w_branch[l, :2].astype(BF16), wc,
                    w_out[l].astype(BF16), mix_norm_post[l][None], tm=tm)

        xf = _ffn(xf, ffn2_norm_pre[l][None], ffn2_w_gu[l].astype(BF16),
                  ffn2_w_down[l].astype(BF16), ffn2_norm_post[l][None], tm=tm)
    return xf.reshape(b, s, d)
```

```python
import functools
import math

import jax
import jax.numpy as jnp
import numpy as np
from jax import lax
from jax.experimental import pallas as pl
from jax.experimental.pallas import tpu as pltpu

F32 = jnp.float32
BF16 = jnp.bfloat16

D_MODEL = 1024
D_FF = 2816
HEAD_DIM = 64
NORM_EPS = 1e-6
NUM_BUCKETS = 32
MAX_DISTANCE = 1024
LAST_BUCKET_DIST = 1 + math.ceil(
    (NUM_BUCKETS // 2) * (MAX_DISTANCE / (NUM_BUCKETS // 2))
    ** ((NUM_BUCKETS // 2 - 1) / (NUM_BUCKETS - NUM_BUCKETS // 2)))
N_HEADS = 8
N_KV = 2
GROUP = N_HEADS // N_KV
CMP_LEN = 32
CMP_STRIDE = 16
CMP_HIDDEN = 128
SEL_LEN = 64
SEL_SHIFT = 6
SEL_TOPK = 16
NSA_WINDOW = 512
SWA_WINDOW = 128
MLA_Q_RANK = 384
MLA_KV_RANK = 256
MLA_NOPE = 64
MLA_ROPE = 32
MLA_V = 64
MLA_QK = MLA_NOPE + MLA_ROPE
ROPE_BASE = 10000.0
BRANCH_WIDTH = N_HEADS * HEAD_DIM
NEG_INF = -1e30
SEL_FORCE = 1e9
LOG2E = math.log2(math.e)
QK_SCALE = HEAD_DIM ** -0.5 * LOG2E

LANE = 128
SUBLANES = 8
VMEM_LIMIT = 56 * 1024 * 1024
HEAD_SHIFT = 6
GROUP_WIDTH = GROUP * HEAD_DIM
KV_SPLIT_CHUNK = 512
MLA_CHAINS = 4

ZB_MQ = 0
ZB_NGATE = 3
ZB_MKV = 4
ZB_KR = 6
ZB_NQ = 8
ZB_CMP_K = 12
ZB_CMP_V = 13
ZB_SEL_KV = 14
ZB_WIN_KV = 16
ZB_SQ = 18
ZB_SWA_KV = 22
ZB_MGATE = 24
Z_BLOCKS = 48
Z_WIDTH = Z_BLOCKS * LANE


def _cparams(sem):
    return pltpu.CompilerParams(dimension_semantics=sem, vmem_limit_bytes=VMEM_LIMIT)


def _rms(x, g):
    inv = lax.rsqrt(jnp.mean(x * x, axis=-1, keepdims=True) + NORM_EPS)
    return (x * inv) * g


def _dot(a, b):
    return jnp.dot(a, b, preferred_element_type=F32)


def _dot_nt(a, b, precision=None):
    return lax.dot_general(a, b, (((1,), (1,)), ((), ())), precision=precision,
                           preferred_element_type=F32)


def _ffn_kernel(x_ref, gpre_ref, wg_ref, wu_ref, wd_ref, gpost_ref, o_ref):
    x = x_ref[...]
    h = _rms(x, gpre_ref[...]).astype(BF16)
    g = _dot(h, wg_ref[...])
    u = _dot(h, wu_ref[...])
    act = (g * jax.nn.sigmoid(g)) * u
    y = _dot(act.astype(BF16), wd_ref[...])
    o_ref[...] = x + 0.5 * _rms(y, gpost_ref[...])


def _ffn(x, g_pre, w_gu, w_down, g_post, *, tm):
    m, d = x.shape
    once = pl.Buffered(1)
    return pl.pallas_call(
        _ffn_kernel,
        grid=(m // tm,),
        in_specs=[
            pl.BlockSpec((tm, d), lambda i: (i, 0)),
            pl.BlockSpec((1, d), lambda i: (0, 0)),
            pl.BlockSpec((d, D_FF), lambda i: (0, 0), pipeline_mode=once),
            pl.BlockSpec((d, D_FF), lambda i: (0, 1), pipeline_mode=once),
            pl.BlockSpec((D_FF, d), lambda i: (0, 0), pipeline_mode=once),
            pl.BlockSpec((1, d), lambda i: (0, 0)),
        ],
        out_specs=pl.BlockSpec((tm, d), lambda i: (i, 0)),
        out_shape=jax.ShapeDtypeStruct((m, d), F32),
        compiler_params=_cparams(("parallel",)),
        name="ffn",
    )(x, g_pre, w_gu, w_gu, w_down, g_post)


def _inproj_kernel(x_ref, g_ref, w_ref, o_ref, h_sc):
    @pl.when(pl.program_id(1) == 0)
    def _():
        h_sc[...] = _rms(x_ref[...], g_ref[...]).astype(BF16)

    o_ref[...] = _dot(h_sc[...], w_ref[...]).astype(BF16)


def _inproj(x, g, w, *, tm, tn):
    m, d = x.shape
    n = w.shape[1]
    return pl.pallas_call(
        _inproj_kernel,
        grid=(m // tm, n // tn),
        in_specs=[
            pl.BlockSpec((tm, d), lambda i, j: (i, 0)),
            pl.BlockSpec((1, d), lambda i, j: (0, 0)),
            pl.BlockSpec((d, tn), lambda i, j: (0, j)),
        ],
        out_specs=pl.BlockSpec((tm, tn), lambda i, j: (i, j)),
        out_shape=jax.ShapeDtypeStruct((m, n), BF16),
        scratch_shapes=[pltpu.VMEM((tm, d), BF16)],
        compiler_params=_cparams(("parallel", "arbitrary")),
        name="inproj",
    )(x, g, w)


def _cmp_mlp_kernel(k_ref, v_ref, pek_ref, w1k_ref, w2k_ref, pev_ref, w1v_ref, w2v_ref,
                    kc_ref, vc_ref):
    half = CMP_STRIDE * HEAD_DIM

    def one(x_ref, pe_ref, w1_ref, w2_ref, o_ref):
        x = x_ref[0, 0].astype(F32)
        ncp = x.shape[0]
        xa = (x + pe_ref[0:1, :]).astype(BF16)
        xb = (x + pe_ref[1:2, :]).astype(BF16)
        pa = _dot(xa, w1_ref[0:half, :])
        pb = _dot(xb, w1_ref[half:2 * half, :])
        hid = pa + pltpu.roll(pb, ncp - 1, 0)
        out = _dot(jax.nn.gelu(hid).astype(BF16), w2_ref[...])
        row = lax.broadcasted_iota(jnp.int32, out.shape, 0)
        o_ref[0, 0] = jnp.where(row < ncp - 1, out, 0.0).astype(BF16)

    one(k_ref, pek_ref, w1k_ref, w2k_ref, kc_ref)
    one(v_ref, pev_ref, w1v_ref, w2v_ref, vc_ref)


def _cmp_mlp(k4, v4, pek, w1k, w2k, pev, w1v, w2v):
    b, hkv, ncp, width = k4.shape
    blk = pl.BlockSpec((1, 1, ncp, width), lambda i, j: (i, j, 0, 0))
    oblk = pl.BlockSpec((1, 1, ncp, HEAD_DIM), lambda i, j: (i, j, 0, 0))

    def full(a):
        return pl.BlockSpec(a.shape, lambda i, j: (0,) * a.ndim)

    out = jax.ShapeDtypeStruct((b, hkv, ncp, HEAD_DIM), BF16)
    return pl.pallas_call(
        _cmp_mlp_kernel,
        grid=(b, hkv),
        in_specs=[blk, blk, full(pek), full(w1k), full(w2k), full(pev), full(w1v), full(w2v)],
        out_specs=[oblk, oblk],
        out_shape=[out, out],
        compiler_params=_cparams(("parallel", "parallel")),
        name="cmp_mlp",
    )(k4, v4, pek, w1k, w2k, pev, w1v, w2v)


def _keep_lanes(q, keep):
    group = lax.broadcasted_iota(jnp.int32, (1, q.shape[1]), 1) >> HEAD_SHIFT
    mask = functools.reduce(jnp.logical_or, [group == g for g, k in enumerate(keep) if k])
    return q * jnp.where(mask, 1.0, 0.0).astype(q.dtype)


def _head_rows(q_all):
    return [_keep_lanes(q_all, [g == h for g in range(GROUP)]) for h in range(GROUP)]


def _stack_heads(q_all, upper):
    q32 = q_all.astype(F32)
    lane = lax.broadcasted_iota(jnp.int32, upper.shape, 1)
    rows = []
    for pair in range(GROUP // 2):
        half = q32[:, pair * LANE:(pair + 1) * LANE]
        rows.append(jnp.where(lane < HEAD_DIM, half, upper))
        rows.append(jnp.where(lane < HEAD_DIM, pltpu.roll(half, HEAD_DIM, 1), upper))
    return jnp.concatenate(rows, axis=0).astype(BF16)


def _split_kv(kv_ref, kt_sc, v_sc, key_rows=None):
    seq = kv_ref.shape[0]
    chunk = min(KV_SPLIT_CHUNK, seq)
    copies = kt_sc.shape[0] // HEAD_DIM - (key_rows is not None)
    lane = lax.broadcasted_iota(jnp.int32, (chunk, LANE), 1)
    for c in range(seq // chunk):
        rows = slice(c * chunk, (c + 1) * chunk)
        x = kv_ref[rows, :].astype(F32)
        kt = x.T[0:HEAD_DIM, :].astype(BF16)
        for r in range(copies):
            kt_sc[r * HEAD_DIM:(r + 1) * HEAD_DIM, rows] = kt
        v_sc[rows, :] = jnp.where(lane < HEAD_DIM, pltpu.roll(x, HEAD_DIM, 1), 1.0).astype(BF16)
    if key_rows is not None:
        kt_sc[copies * HEAD_DIM:(copies + 1) * HEAD_DIM, :] = key_rows


def _finish_pair_rolled(acc_even, acc_odd):
    lane = lax.broadcasted_iota(jnp.int32, acc_even.shape, 1)
    even = acc_even / pltpu.roll(acc_even, HEAD_DIM, 1)
    odd = pltpu.roll(acc_odd, HEAD_DIM, 1) / acc_odd
    return jnp.where(lane < HEAD_DIM, even, odd)


def _finish_group(accs):
    return jnp.concatenate([_finish_pair_rolled(accs[0], accs[1]),
                            _finish_pair_rolled(accs[2], accs[3])], axis=1)


def _online_update(s, v, m_ref, acc_ref, idx):
    m_old = m_ref[idx]
    m_new = jnp.maximum(m_old, jnp.max(s, axis=-1, keepdims=True))
    alpha = jnp.exp2(m_old - m_new)
    p = jnp.exp2(s - jnp.concatenate([m_new] * (s.shape[1] // LANE), axis=1))
    acc = acc_ref[idx]
    acc_ref[idx] = (jnp.concatenate([alpha] * (acc.shape[1] // LANE), axis=1) * acc
                    + _dot(p.astype(BF16), v))
    m_ref[idx] = m_new


def _cmp_attn_kernel(q_ref, kc_ref, vc_ref, bias_ref, ovl_ref, o_ref, sel_ref, *, tq, ncp, nsel):
    qi = pl.program_id(1)
    vc = vc_ref[0, 0]
    lane = lax.broadcasted_iota(jnp.int32, (tq, LANE), 1)
    row = lax.broadcasted_iota(jnp.int32, (tq, ncp), 0) + qi * tq
    col = lax.broadcasted_iota(jnp.int32, (tq, ncp), 1)
    valid = (col * CMP_STRIDE + (CMP_LEN - 1)) <= row
    p_sum = jnp.zeros((tq, ncp), F32)
    outs = []
    s_all = _dot(jnp.concatenate(_head_rows(q_ref[...]), axis=0), kc_ref[0, 0])
    for g in range(GROUP):
        s = s_all[g * tq:(g + 1) * tq] + bias_ref[0, g]
        s = jnp.where(valid, s, NEG_INF)
        e = jnp.where(valid, jnp.exp2(s - jnp.max(s, axis=-1, keepdims=True)), 0.0)
        den = jnp.sum(e, axis=-1, keepdims=True)
        p = e / jnp.where(den > 0, den, 1.0)
        outs.append(_dot(p.astype(BF16), vc))
        p_sum = p_sum + p
    o_ref[...] = jnp.concatenate([jnp.where(lane < HEAD_DIM, outs[0], outs[1]),
                                  jnp.where(lane < HEAD_DIM, outs[2], outs[3])],
                                 axis=1).astype(BF16)

    imp = _dot_nt(ovl_ref[...], p_sum, precision=lax.Precision.HIGHEST)
    jj = lax.broadcasted_iota(jnp.int32, (nsel, tq), 0)
    cur = (lax.broadcasted_iota(jnp.int32, (nsel, tq), 1) + qi * tq) >> SEL_SHIFT
    forced = (jj == 0) | (jj == cur) | (jj == cur - 1)
    score = jnp.where(forced, SEL_FORCE, jnp.where(jj <= cur, imp, -SEL_FORCE))
    sub = SUBLANES
    groups = [score[g * sub:(g + 1) * sub, :] for g in range(nsel // sub)]
    ranks = [jnp.zeros((sub, tq), F32) for _ in groups]
    jg = lax.broadcasted_iota(jnp.int32, (sub, tq), 0)
    for i in range(nsel):
        si = score[i:i + 1, :]
        for g, sg in enumerate(groups):
            if g * sub > i:
                ahead = jnp.where(si >= sg, 1.0, 0.0)
            elif (g + 1) * sub <= i:
                ahead = jnp.where(si > sg, 1.0, 0.0)
            else:
                ahead = jnp.where(jg + g * sub > i, jnp.where(si >= sg, 1.0, 0.0),
                                  jnp.where(si > sg, 1.0, 0.0))
            ranks[g] = ranks[g] + ahead
    rank = jnp.concatenate(ranks, axis=0)
    dropped = jnp.where(rank < min(SEL_TOPK, nsel), 0.0, 1.0).astype(BF16)
    pieces = [jnp.zeros((HEAD_DIM, tq), BF16), dropped]
    if nsel < LANE - HEAD_DIM:
        pieces.append(jnp.zeros((LANE - HEAD_DIM - nsel, tq), BF16))
    dropped = jnp.concatenate(pieces, axis=0)
    eye = (lax.broadcasted_iota(jnp.int32, (tq, tq), 0)
           == lax.broadcasted_iota(jnp.int32, (tq, tq), 1))
    sel_ref[0, 0] = _dot_nt(jnp.where(eye, 1.0, 0.0).astype(BF16), dropped).astype(BF16)


def _cmp_attn(z, kc4, vc, bias_c, ovl_t, *, tq, seq):
    b, hkv, _, ncp = kc4.shape
    m = z.shape[0]
    nq = seq // tq
    nsel = ovl_t.shape[0]
    kern = functools.partial(_cmp_attn_kernel, tq=tq, ncp=ncp, nsel=nsel)
    qblk = ZB_NQ * LANE // GROUP_WIDTH
    return pl.pallas_call(
        kern,
        grid=(hkv, nq, b),
        in_specs=[
            pl.BlockSpec((tq, GROUP_WIDTH), lambda h, i, bb: (bb * nq + i, qblk + h)),
            pl.BlockSpec((1, 1, GROUP_WIDTH, ncp), lambda h, i, bb: (bb, h, 0, 0)),
            pl.BlockSpec((1, 1, ncp, LANE), lambda h, i, bb: (bb, h, 0, 0)),
            pl.BlockSpec((1, GROUP, tq, ncp), lambda h, i, bb: (h, 0, i, 0)),
            pl.BlockSpec((nsel, ncp), lambda h, i, bb: (0, 0)),
        ],
        out_specs=[
            pl.BlockSpec((tq, GROUP_WIDTH), lambda h, i, bb: (bb * nq + i, h)),
            pl.BlockSpec((1, 1, tq, LANE), lambda h, i, bb: (bb, h, i, 0)),
        ],
        out_shape=[
            jax.ShapeDtypeStruct((m, BRANCH_WIDTH), BF16),
            jax.ShapeDtypeStruct((b, hkv, seq, LANE), BF16),
        ],
        compiler_params=_cparams(("parallel", "parallel", "parallel")),
        name="cmp_attn",
    )(z, kc4, vc, bias_c, ovl_t)


def _band_attn_kernel(*refs, tq, w, back, n_off, has_sink):
    if has_sink:
        q_ref, kv_ref, bias_ref, sink_ref, o_ref, kt_sc, v_sc = refs
    else:
        q_ref, kv_ref, bias_ref, o_ref, kt_sc, v_sc = refs
    qi = pl.program_id(2)

    @pl.when(qi == 0)
    def _():
        _split_kv(kv_ref, kt_sc, v_sc)

    off = jnp.minimum(qi * tq, back)
    start = pl.multiple_of(qi * tq - off, LANE)
    q = jnp.concatenate(_head_rows(q_ref[...]), axis=0)
    s_all = _dot(q, kt_sc[:, pl.ds(start, w)])
    v1 = v_sc[pl.ds(start, w), :]
    tile = jnp.minimum(qi, n_off - 1)
    sum_lane = lax.broadcasted_iota(jnp.int32, (tq, LANE), 1) >= HEAD_DIM
    accs = []
    for h in range(GROUP):
        s = s_all[h * tq:(h + 1) * tq] + bias_ref[0, tile, h]
        m = jnp.broadcast_to(jnp.max(s, axis=-1, keepdims=True), (tq, LANE))
        if has_sink:
            sink = sink_ref[0, h]
            m = jnp.maximum(m, sink)
        p = jnp.exp2(s - jnp.concatenate([m] * (w // LANE), axis=1))
        acc = _dot(p.astype(BF16), v1)
        if has_sink:
            acc = acc + jnp.where(sum_lane, jnp.exp2(sink - m), 0.0)
        accs.append(acc)
    o_ref[...] = _finish_group(accs).astype(BF16)


def _sel_attn_kernel(q_ref, kv_ref, drop_ref, rows_ref, bias_ref, o_ref, kt_sc, v_sc, m_sc, acc_sc,
                     *, tq, tk, n_bias):
    qi = pl.program_id(2)

    @pl.when(qi == 0)
    def _():
        _split_kv(kv_ref, kt_sc, v_sc, rows_ref[...])

    q = _stack_heads(q_ref[...], drop_ref[0, 0].astype(F32))
    hi = ((qi + 1) * tq + tk - 1) // tk
    m_sc[...] = jnp.full(m_sc.shape, NEG_INF, F32)
    acc_sc[...] = jnp.zeros(acc_sc.shape, F32)

    def step(kj):
        start = pl.multiple_of(kj * tk, tk)
        s_all = _dot(q, kt_sc[:, pl.ds(start, tk)])
        v1 = v_sc[pl.ds(start, tk), :]
        u = jnp.minimum((qi * tq - kj * tk) // tq, n_bias - 1)
        for h in range(GROUP):
            s = s_all[h * tq:(h + 1) * tq] + bias_ref[0, u, h]
            _online_update(s, v1, m_sc, acc_sc, h)

    odd = hi % 2

    @pl.when(odd == 1)
    def _():
        step(hi - 1)

    def body(i, carry):
        kj = hi - odd - 1 - 2 * i
        step(kj)
        step(kj - 1)
        return carry

    lax.fori_loop(0, hi // 2, body, 0)
    o_ref[...] = _finish_group([acc_sc[h] for h in range(GROUP)]).astype(BF16)


def _gqa_specs(z, q_block, kv_block, tq, seq):
    nq = seq // tq
    qblk = q_block * LANE // GROUP_WIDTH
    return [
        pl.BlockSpec((tq, GROUP_WIDTH), lambda h, bb, i: (bb * nq + i, qblk + h)),
        pl.BlockSpec((seq, LANE), lambda h, bb, i: (bb, kv_block + h)),
    ]


def _resident(a):
    return pl.BlockSpec((1,) + a.shape[1:], lambda h, bb, i: (h,) + (0,) * (a.ndim - 1),
                        pipeline_mode=pl.Buffered(1))


def _gqa_out(m, tq, seq):
    nq = seq // tq
    return (pl.BlockSpec((tq, GROUP_WIDTH), lambda h, bb, i: (bb * nq + i, h)),
            jax.ShapeDtypeStruct((m, BRANCH_WIDTH), BF16))


def _kv_scratch(seq, key_rows):
    return [pltpu.VMEM((key_rows, seq), BF16), pltpu.VMEM((seq, LANE), BF16)]


def _band_attention(z, bias, *, q_block, kv_block, tq, seq, back, name, sinks=None):
    m = z.shape[0]
    b = m // seq
    n_off, w = bias.shape[1], bias.shape[4]
    assert w == back + tq and w <= seq and back % LANE == 0
    kern = functools.partial(_band_attn_kernel, tq=tq, w=w, back=back, n_off=n_off,
                             has_sink=sinks is not None)
    in_specs = _gqa_specs(z, q_block, kv_block, tq, seq) + [_resident(bias)]
    args = [z, z, bias]
    if sinks is not None:
        in_specs.append(pl.BlockSpec((1, GROUP, 1, 1), lambda h, bb, i: (h, 0, 0, 0)))
        args.append(sinks)
    out_spec, out_shape = _gqa_out(m, tq, seq)
    return pl.pallas_call(
        kern,
        grid=(N_KV, b, seq // tq),
        in_specs=in_specs,
        out_specs=out_spec,
        out_shape=out_shape,
        scratch_shapes=_kv_scratch(seq, GROUP_WIDTH),
        compiler_params=_cparams(("parallel", "parallel", "arbitrary")),
        name=name,
    )(*args)


def _sel_attention(z, dropped, sel_rows, bias, *, tq, tk, seq):
    m = z.shape[0]
    b = m // seq
    kern = functools.partial(_sel_attn_kernel, tq=tq, tk=tk, n_bias=bias.shape[1])
    in_specs = _gqa_specs(z, ZB_NQ, ZB_SEL_KV, tq, seq) + [
        pl.BlockSpec((1, 1, tq, LANE), lambda h, bb, i: (bb, h, i, 0)),
        pl.BlockSpec((HEAD_DIM, seq), lambda h, bb, i: (0, 0), pipeline_mode=pl.Buffered(1)),
        _resident(bias),
    ]
    out_spec, out_shape = _gqa_out(m, tq, seq)
    return pl.pallas_call(
        kern,
        grid=(N_KV, b, seq // tq),
        in_specs=in_specs,
        out_specs=out_spec,
        out_shape=out_shape,
        scratch_shapes=_kv_scratch(seq, LANE) + [pltpu.VMEM((GROUP, tq, LANE), F32),
                                                 pltpu.VMEM((GROUP, tq, LANE), F32)],
        compiler_params=_cparams(("parallel", "parallel", "arbitrary")),
        name="attn_sel",
    )(z, z, dropped, sel_rows, bias)


def _rope_lanes(x, cos_t, sin_a, sin_b):
    half = MLA_ROPE // 2
    return x * cos_t + pltpu.roll(x, LANE - half, 1) * sin_a + pltpu.roll(x, half, 1) * sin_b


def _mla_proj_kernel(mq_ref, mkv_ref, kr_ref, cos_ref, sina_ref, sinb_ref, qn_ref, wq_ref,
                     kvn_ref, wkt_ref, wv_ref, oq_ref, okt_ref, ov_ref):
    cos_t, sin_a, sin_b = cos_ref[...], sina_ref[...], sinb_ref[...]
    ql = _rms(mq_ref[...].astype(F32), qn_ref[...]).astype(BF16)
    q = _dot(ql, wq_ref[...]) * (MLA_QK ** -0.5 * LOG2E)
    kvl = _rms(mkv_ref[...].astype(F32), kvn_ref[...]).astype(BF16)
    kt = _dot_nt(wkt_ref[...], kvl)
    v = _dot(kvl, wv_ref[...])
    kr_t = _rope_lanes(kr_ref[...].astype(F32), cos_t, sin_a, sin_b).T
    lane = lax.broadcasted_iota(jnp.int32, (v.shape[0], LANE), 1)
    for h in range(N_HEADS):
        blk = slice(h * LANE, (h + 1) * LANE)
        oq_ref[:, blk] = _rope_lanes(q[:, blk], cos_t, sin_a, sin_b).astype(BF16)
        okt_ref[0, blk, :] = (kt[blk, :] + kr_t).astype(BF16)
        ov_ref[:, blk] = jnp.where(lane < MLA_V, v[:, blk], 1.0).astype(BF16)


def _mla_proj(z, cos_t, sin_a, sin_b, q_norm, w_qb, kv_norm, w_kt, w_v, *, tm, seq):
    m = z.shape[0]
    nt = seq // tm
    width = N_HEADS * LANE

    def full(a):
        return pl.BlockSpec(a.shape, lambda i: (0,) * a.ndim)

    tab = pl.BlockSpec((tm, LANE), lambda i: (i % nt, 0))
    out = jax.ShapeDtypeStruct((m, width), BF16)
    oblk = pl.BlockSpec((tm, width), lambda i: (i, 0))
    return pl.pallas_call(
        _mla_proj_kernel,
        grid=(m // tm,),
        in_specs=[
            pl.BlockSpec((tm, MLA_Q_RANK), lambda i: (i, ZB_MQ * LANE // MLA_Q_RANK)),
            pl.BlockSpec((tm, MLA_KV_RANK), lambda i: (i, ZB_MKV * LANE // MLA_KV_RANK)),
            pl.BlockSpec((tm, LANE), lambda i: (i, ZB_KR)),
            tab, tab, tab,
            full(q_norm), full(w_qb), full(kv_norm), full(w_kt), full(w_v),
        ],
        out_specs=[oblk, pl.BlockSpec((1, width, tm), lambda i: (i // nt, 0, i % nt)), oblk],
        out_shape=[out, jax.ShapeDtypeStruct((m // seq, width, seq), BF16), out],
        compiler_params=_cparams(("parallel",)),
        name="mla_proj",
    )(z, z, z, cos_t, sin_a, sin_b, q_norm, w_qb, kv_norm, w_kt, w_v)


def _mla_attn_kernel(q_ref, k_ref, v_ref, o_ref, m_sc, acc_sc, *, tq):
    qi = pl.program_id(2)
    n_chain = m_sc.shape[0]
    qs = [q_ref[:, c * LANE:(c + 1) * LANE] for c in range(n_chain)]
    m_sc[...] = jnp.full(m_sc.shape, NEG_INF, F32)
    acc_sc[...] = jnp.zeros(acc_sc.shape, F32)

    def step(tile, width, diagonal):
        start = pl.multiple_of(tile * tq, tq)
        for c in range(n_chain):
            blk = slice(c * LANE, (c + 1) * LANE)
            s = _dot(qs[c], k_ref[0, blk, pl.ds(start, width * tq)])
            if diagonal:
                row = lax.broadcasted_iota(jnp.int32, s.shape, 0)
                col = lax.broadcasted_iota(jnp.int32, s.shape, 1)
                s = jnp.where(col <= row, s, NEG_INF)
            _online_update(s, v_ref[pl.ds(start, width * tq), blk], m_sc, acc_sc, c)

    step(qi, 1, True)
    odd = qi % 2

    @pl.when(odd == 1)
    def _():
        step(qi - 1, 1, False)

    def body(i, carry):
        step(qi - odd - 2 * (i + 1), 2, False)
        return carry

    lax.fori_loop(0, qi // 2, body, 0)
    for c in range(n_chain):
        acc = acc_sc[c]
        lane = lax.broadcasted_iota(jnp.int32, acc.shape, 1)
        den = jnp.where(lane < MLA_V, pltpu.roll(acc, MLA_V, 1), acc)
        o_ref[:, c * LANE:(c + 1) * LANE] = (acc / den).astype(BF16)


def _mla_attention(q, k, v1, *, tq, seq):
    m = q.shape[0]
    b = m // seq
    nq = seq // tq
    width = MLA_CHAINS * LANE
    slab = pl.BlockSpec((seq, width), lambda h, bb, i: (bb, h))
    slab_t = pl.BlockSpec((1, width, seq), lambda h, bb, i: (bb, h, 0))
    tile = pl.BlockSpec((tq, width), lambda h, bb, i: (bb * nq + i, h))
    return pl.pallas_call(
        functools.partial(_mla_attn_kernel, tq=tq),
        grid=(N_HEADS // MLA_CHAINS, b, nq),
        in_specs=[tile, slab_t, slab],
        out_specs=tile,
        out_shape=jax.ShapeDtypeStruct(q.shape, BF16),
        scratch_shapes=[pltpu.VMEM((MLA_CHAINS, tq, LANE), F32),
                        pltpu.VMEM((MLA_CHAINS, tq, LANE), F32)],
        compiler_params=_cparams(("parallel", "parallel", "arbitrary")),
        name="attn_mla",
    )(q, k, v1)


def _merge_kernel(ocmp_ref, osel_ref, owin_ref, ob_ref, oc_ref, ng_ref, mg_ref, x_ref,
                  gexp_ref, wab_ref, wc_ref, wo_ref, gpost_ref, o_ref):
    ng = jax.nn.sigmoid(ng_ref[...].astype(F32))
    ng_hi = ng.astype(BF16)
    ng_lo = (ng - ng_hi.astype(F32)).astype(BF16)
    gates = _dot(jnp.concatenate([ng_hi, ng_lo], axis=1), gexp_ref[...])
    o_a = jnp.zeros(ocmp_ref.shape, F32)
    for n, ref in enumerate((ocmp_ref, osel_ref, owin_ref)):
        gate = gates[:, n * BRANCH_WIDTH:(n + 1) * BRANCH_WIDTH]
        o_a = o_a + gate * ref[...].astype(F32)
    ys = (_dot(o_a.astype(BF16), wab_ref[0]), _dot(ob_ref[...], wab_ref[1]),
          _dot(oc_ref[...], wc_ref[...]))
    mixed = jnp.zeros(x_ref.shape, F32)
    for n, y in enumerate(ys):
        gate = jax.nn.sigmoid(mg_ref[:, n * D_MODEL:(n + 1) * D_MODEL].astype(F32))
        mixed = mixed + gate * y
    out = _dot(mixed.astype(BF16), wo_ref[...])
    o_ref[...] = x_ref[...] + _rms(out, gpost_ref[...])


def _merge(o_cmp, o_sel, o_win, o_b, o_c, z, x, gexp, w_ab, w_c, w_out, g_post, *, tm):
    m, d = x.shape
    br = pl.BlockSpec((tm, o_cmp.shape[1]), lambda i: (i, 0))

    def full(a):
        return pl.BlockSpec(a.shape, lambda i: (0,) * a.ndim)

    return pl.pallas_call(
        _merge_kernel,
        grid=(m // tm,),
        in_specs=[
            br, br, br, br,
            pl.BlockSpec((tm, o_c.shape[1]), lambda i: (i, 0)),
            pl.BlockSpec((tm, LANE), lambda i: (i, ZB_NGATE)),
            pl.BlockSpec((tm, 3 * d), lambda i: (i, ZB_MGATE * LANE // (3 * d))),
            pl.BlockSpec((tm, d), lambda i: (i, 0)),
            full(gexp), full(w_ab), full(w_c), full(w_out), full(g_post),
        ],
        out_specs=pl.BlockSpec((tm, d), lambda i: (i, 0)),
        out_shape=jax.ShapeDtypeStruct((m, d), F32),
        compiler_params=_cparams(("parallel",)),
        name="merge",
    )(o_cmp, o_sel, o_win, o_b, o_c, z, z, x, gexp, w_ab, w_c, w_out, g_post)


def _t5_bucket(dist):
    max_exact = NUM_BUCKETS // 2
    d = jnp.maximum(dist, 0)
    df = jnp.maximum(d, 1).astype(F32)
    large = max_exact + (jnp.log(df / max_exact) / math.log(MAX_DISTANCE / max_exact)
                         * (NUM_BUCKETS - max_exact)).astype(jnp.int32)
    large = jnp.minimum(large, NUM_BUCKETS - 1)
    return jnp.where(d < max_exact, d, large)


def _toeplitz(fn, offs, tq, tk):
    ln = tq + tk - 1
    i = jnp.arange(ln + 1)
    shift = jnp.where(i < tk, -i, ln + 1 - i)
    ext = jnp.moveaxis(fn(jnp.asarray(offs)[:, None] + shift[None, :]), -1, 0)
    flat = jnp.tile(ext, (1, 1, tq))[:, :, :tq * ln]
    return flat.reshape(ext.shape[0], len(offs), tq, ln)[:, :, :, :tk]


def _bias_fn(table, window):
    def fn(dist):
        ok = dist >= 0
        if window is not None:
            ok = ok & (dist < window)
        return jnp.where(ok[..., None], table[_t5_bucket(dist)].astype(F32), NEG_INF)
    return fn


def _head_tiles(t):
    return t.reshape(N_KV, GROUP, *t.shape[1:]).transpose(0, 2, 1, 3, 4)


def _in_proj_weight(w_in):
    sizes = (512, 128, 128, 128, 128, 128, 128, 24, 512, 128, 128, MLA_Q_RANK, MLA_KV_RANK,
             MLA_ROPE, 3 * D_MODEL)
    offs = np.concatenate([[0], np.cumsum(sizes)])
    seg = [w_in[:, offs[i]:offs[i + 1]] for i in range(len(sizes))]
    (nq, nkc, nvc, nks, nvs, nkw, nvw, ngate, sq, sk, sv, mq, mkv, mkr, mgate) = seg
    d = w_in.shape[0]
    hd = HEAD_DIM

    def pad(a, left=0):
        return jnp.pad(a, ((0, 0), (left, LANE - left - a.shape[1])))

    def kv_blocks(k, v):
        return [jnp.concatenate([k[:, h * hd:(h + 1) * hd], v[:, h * hd:(h + 1) * hd]], axis=1)
                for h in range(N_KV)]

    cols = ([mq, pad(ngate), mkv, pad(mkr, MLA_NOPE), jnp.zeros((d, LANE), w_in.dtype),
             nq * QK_SCALE, nkc, nvc] + kv_blocks(nks, nvs) + kv_blocks(nkw, nvw)
            + [sq * QK_SCALE] + kv_blocks(sk, sv) + [mgate])
    w = jnp.concatenate(cols, axis=1)
    assert w.shape == (d, Z_WIDTH)
    return w.astype(BF16)


def _mla_weights(w_qb, w_kvb, w_c):
    def blocks(a):
        return jnp.pad(a, ((0, 0), (0, 0), (0, LANE - a.shape[2]))).reshape(a.shape[0], -1)

    wq = blocks(w_qb.reshape(MLA_Q_RANK, N_HEADS, MLA_QK))
    wkv = w_kvb.reshape(MLA_KV_RANK, N_HEADS, MLA_NOPE + MLA_V)
    wkt = blocks(wkv[:, :, :MLA_NOPE]).T
    wv = blocks(wkv[:, :, MLA_NOPE:])
    wc = jnp.pad(w_c.reshape(N_HEADS, MLA_V, -1), ((0, 0), (0, LANE - MLA_V), (0, 0)))
    return (wq.astype(BF16), wkt.astype(BF16), wv.astype(BF16),
            wc.reshape(N_HEADS * LANE, -1).astype(BF16))


def _gate_expand():
    e = np.zeros((2 * LANE, 3 * BRANCH_WIDTH), np.float32)
    for n in range(3):
        for h in range(N_HEADS):
            cols = slice(n * BRANCH_WIDTH + h * HEAD_DIM, n * BRANCH_WIDTH + (h + 1) * HEAD_DIM)
            e[h * 3 + n, cols] = 1.0
            e[LANE + h * 3 + n, cols] = 1.0
    return jnp.asarray(e, BF16)


def _rope_tables(pos):
    half = MLA_ROPE // 2
    inv_freq = ROPE_BASE ** (-jnp.arange(half, dtype=F32) / half)
    ang = pos.astype(F32)[:, None] * inv_freq[None, :]
    cos, sin = jnp.cos(ang), jnp.sin(ang)
    n = pos.shape[0]
    ones = jnp.ones((n, MLA_NOPE), F32)
    z16 = jnp.zeros((n, half), F32)
    tail = jnp.zeros((n, LANE - MLA_QK), F32)
    cos_t = jnp.concatenate([ones, cos, cos, tail], axis=1)
    sin_a = jnp.concatenate([0 * ones, -sin, z16, tail], axis=1)
    sin_b = jnp.concatenate([0 * ones, z16, sin, tail], axis=1)
    return cos_t, sin_a, sin_b


def kernel(x, rel_bias_table, ffn1_norm_pre, ffn1_w_gu, ffn1_w_down, ffn1_norm_post,
           mix_norm_pre, w_in, nsa_pe_k, nsa_w1_k, nsa_w2_k, nsa_pe_v, nsa_w1_v, nsa_w2_v,
           swa_sinks, mla_q_norm, mla_w_qb, mla_kv_norm, mla_w_kvb, w_branch, w_out,
           mix_norm_post, ffn2_norm_pre, ffn2_w_gu, ffn2_w_down, ffn2_norm_post):
    b, s, d = x.shape
    depth = w_in.shape[0]
    m = b * s
    tq = 256
    tk_sel = min(512, s)
    tq_mla = min(512, s)
    ncp = s // CMP_STRIDE
    nsel = s // SEL_LEN
    tm = min(512, m)
    tm_big = min(1024, s)

    pos = jnp.arange(s, dtype=jnp.int32)
    tab_a = rel_bias_table[:, :N_HEADS] * LOG2E
    tab_b = rel_bias_table[:, N_HEADS:] * LOG2E
    n_far = -(-(LAST_BUCKET_DIST + tk_sel - 1) // tq)
    n_sel_tiles = min(n_far + 1, s // tq)
    bias_sel = _head_tiles(_toeplitz(_bias_fn(tab_a, None), [u * tq for u in range(n_sel_tiles)],
                                     tq, tk_sel))
    win_offs = sorted({min(i * tq, NSA_WINDOW) for i in range(-(-NSA_WINDOW // tq) + 1)})
    bias_win = _head_tiles(_toeplitz(_bias_fn(tab_a, NSA_WINDOW), win_offs, tq, NSA_WINDOW + tq))
    swa_offs = sorted({min(i * tq, SWA_WINDOW) for i in range(-(-SWA_WINDOW // tq) + 1)})
    bias_swa = _head_tiles(_toeplitz(_bias_fn(tab_b, SWA_WINDOW), swa_offs, tq, SWA_WINDOW + tq))

    def cmp_bias_fn(e):
        r = jnp.arange(CMP_STRIDE)
        dist = e[..., None] * CMP_STRIDE + r - (CMP_LEN - 1)
        return tab_a[_t5_bucket(dist)].astype(F32).reshape(*e.shape, CMP_STRIDE * N_HEADS)

    bias_c = _toeplitz(cmp_bias_fn, [0], ncp, ncp).reshape(CMP_STRIDE, N_HEADS, ncp, ncp)
    bias_c = bias_c.transpose(1, 2, 0, 3).reshape(N_KV, GROUP, s, ncp)
    ci = jnp.arange(ncp)[None, :] * CMP_STRIDE
    sj = jnp.arange(nsel)[:, None] * SEL_LEN
    ovl_t = (jnp.maximum(jnp.minimum(ci + CMP_LEN, sj + SEL_LEN) - jnp.maximum(ci, sj), 0)
             .astype(F32) / CMP_LEN)
    cos_t, sin_a, sin_b = _rope_tables(pos)
    gexp = _gate_expand()
    assert nsel <= HEAD_DIM
    sel_rows = jnp.where(jnp.arange(HEAD_DIM)[:, None] == (pos[None, :] >> SEL_SHIFT),
                         NEG_INF, 0.0).astype(BF16)
    chunk = CMP_STRIDE * HEAD_DIM

    xf = x.reshape(m, d)
    for l in range(depth):
        xf = _ffn(xf, ffn1_norm_pre[l][None], ffn1_w_gu[l].astype(BF16),
                  ffn1_w_down[l].astype(BF16), ffn1_norm_post[l][None], tm=tm)

        z = _inproj(xf, mix_norm_pre[l][None], _in_proj_weight(w_in[l]), tm=tm_big, tn=Z_WIDTH // 4)

        def chunks(blk):
            a = z[:, blk * LANE:(blk + 1) * LANE].reshape(b, s, N_KV, HEAD_DIM)
            return a.transpose(0, 2, 1, 3).reshape(b, N_KV, ncp, chunk)

        kc, vc = _cmp_mlp(chunks(ZB_CMP_K), chunks(ZB_CMP_V),
                          nsa_pe_k[l].reshape(2, chunk), nsa_w1_k[l].astype(BF16),
                          nsa_w2_k[l].astype(BF16),
                          nsa_pe_v[l].reshape(2, chunk), nsa_w1_v[l].astype(BF16),
                          nsa_w2_v[l].astype(BF16))
        o_cmp, dropped = _cmp_attn(z, jnp.tile(kc.transpose(0, 1, 3, 2), (1, 1, GROUP, 1)),
                                   jnp.tile(vc, (1, 1, 1, 2)),
                                   bias_c, ovl_t, tq=tq, seq=s)
        o_sel = _sel_attention(z, dropped, sel_rows, bias_sel, tq=tq, tk=tk_sel, seq=s)
        o_win = _band_attention(z, bias_win, q_block=ZB_NQ, kv_block=ZB_WIN_KV, tq=tq, seq=s,
                                back=NSA_WINDOW, name="attn_win")
        o_b = _band_attention(z, bias_swa, q_block=ZB_SQ, kv_block=ZB_SWA_KV, tq=tq, seq=s,
                              back=SWA_WINDOW, name="attn_swa",
                              sinks=(swa_sinks[l] * LOG2E).reshape(N_KV, GROUP, 1, 1))
        wq, wkt, wv, wc = _mla_weights(mla_w_qb[l], mla_w_kvb[l], w_branch[l, 2])
        mq, mk, mv1 = _mla_proj(z, cos_t, sin_a, sin_b, mla_q_norm[l][None], wq,
                                mla_kv_norm[l][None], wkt, wv, tm=tm_big, seq=s)
        o_c = _mla_attention(mq, mk, mv1, tq=tq_mla, seq=s)

        xf = _merge(o_cmp, o_sel, o_win, o_b, o_c, z, xf, gexp, w_branch[l, :2].astype(BF16), wc,
                    w_out[l].astype(BF16), mix_norm_post[l][None], tm=tm)

        xf = _ffn(xf, ffn2_norm_pre[l][None], ffn2_w_gu[l].astype(BF16),
                  ffn2_w_down[l].astype(BF16), ffn2_norm_post[l][None], tm=tm)
    return xf.reshape(b, s, d)
```

```python
import functools
import math

import jax
import jax.numpy as jnp
import numpy as np
from jax import lax
from jax.experimental import pallas as pl
from jax.experimental.pallas import tpu as pltpu

F32 = jnp.float32
BF16 = jnp.bfloat16

D_MODEL = 1024
D_FF = 2816
HEAD_DIM = 64
NORM_EPS = 1e-6
NUM_BUCKETS = 32
MAX_DISTANCE = 1024
LAST_BUCKET_DIST = 1 + math.ceil(
    (NUM_BUCKETS // 2) * (MAX_DISTANCE / (NUM_BUCKETS // 2))
    ** ((NUM_BUCKETS // 2 - 1) / (NUM_BUCKETS - NUM_BUCKETS // 2)))
N_HEADS = 8
N_KV = 2
GROUP = N_HEADS // N_KV
CMP_LEN = 32
CMP_STRIDE = 16
CMP_HIDDEN = 128
SEL_LEN = 64
SEL_SHIFT = 6
SEL_TOPK = 16
NSA_WINDOW = 512
SWA_WINDOW = 128
MLA_Q_RANK = 384
MLA_KV_RANK = 256
MLA_NOPE = 64
MLA_ROPE = 32
MLA_V = 64
MLA_QK = MLA_NOPE + MLA_ROPE
ROPE_BASE = 10000.0
BRANCH_WIDTH = N_HEADS * HEAD_DIM
NEG_INF = -1e30
SEL_FORCE = 1e9
LOG2E = math.log2(math.e)
QK_SCALE = HEAD_DIM ** -0.5 * LOG2E

LANE = 128
SUBLANES = 8
VMEM_LIMIT = 56 * 1024 * 1024
HEAD_SHIFT = 6
GROUP_WIDTH = GROUP * HEAD_DIM
KV_SPLIT_CHUNK = 512
MLA_CHAINS = 4

ZB_MQ = 0
ZB_NGATE = 3
ZB_MKV = 4
ZB_KR = 6
ZB_NQ = 8
ZB_CMP_K = 12
ZB_CMP_V = 13
ZB_SEL_KV = 14
ZB_WIN_KV = 16
ZB_SQ = 18
ZB_SWA_KV = 22
ZB_MGATE = 24
Z_BLOCKS = 48
Z_WIDTH = Z_BLOCKS * LANE


def _cparams(sem):
    return pltpu.CompilerParams(dimension_semantics=sem, vmem_limit_bytes=VMEM_LIMIT)


def _rms(x, g):
    inv = lax.rsqrt(jnp.mean(x * x, axis=-1, keepdims=True) + NORM_EPS)
    return (x * inv) * g


def _dot(a, b):
    return jnp.dot(a, b, preferred_element_type=F32)


def _dot_nt(a, b, precision=None):
    return lax.dot_general(a, b, (((1,), (1,)), ((), ())), precision=precision,
                           preferred_element_type=F32)


def _ffn_kernel(x_ref, gpre_ref, wg_ref, wu_ref, wd_ref, gpost_ref, o_ref):
    x = x_ref[...]
    h = _rms(x, gpre_ref[...]).astype(BF16)
    g = _dot(h, wg_ref[...])
    u = _dot(h, wu_ref[...])
    act = (g * jax.nn.sigmoid(g)) * u
    y = _dot(act.astype(BF16), wd_ref[...])
    o_ref[...] = x + 0.5 * _rms(y, gpost_ref[...])


def _ffn(x, g_pre, w_gu, w_down, g_post, *, tm):
    m, d = x.shape
    once = pl.Buffered(1)
    return pl.pallas_call(
        _ffn_kernel,
        grid=(m // tm,),
        in_specs=[
            pl.BlockSpec((tm, d), lambda i: (i, 0)),
            pl.BlockSpec((1, d), lambda i: (0, 0)),
            pl.BlockSpec((d, D_FF), lambda i: (0, 0), pipeline_mode=once),
            pl.BlockSpec((d, D_FF), lambda i: (0, 1), pipeline_mode=once),
            pl.BlockSpec((D_FF, d), lambda i: (0, 0), pipeline_mode=once),
            pl.BlockSpec((1, d), lambda i: (0, 0)),
        ],
        out_specs=pl.BlockSpec((tm, d), lambda i: (i, 0)),
        out_shape=jax.ShapeDtypeStruct((m, d), F32),
        compiler_params=_cparams(("parallel",)),
        name="ffn",
    )(x, g_pre, w_gu, w_gu, w_down, g_post)


def _inproj_kernel(x_ref, g_ref, w_ref, o_ref, h_sc):
    @pl.when(pl.program_id(1) == 0)
    def _():
        h_sc[...] = _rms(x_ref[...], g_ref[...]).astype(BF16)

    o_ref[...] = _dot(h_sc[...], w_ref[...]).astype(BF16)


def _inproj(x, g, w, *, tm, tn):
    m, d = x.shape
    n = w.shape[1]
    return pl.pallas_call(
        _inproj_kernel,
        grid=(m // tm, n // tn),
        in_specs=[
            pl.BlockSpec((tm, d), lambda i, j: (i, 0)),
            pl.BlockSpec((1, d), lambda i, j: (0, 0)),
            pl.BlockSpec((d, tn), lambda i, j: (0, j)),
        ],
        out_specs=pl.BlockSpec((tm, tn), lambda i, j: (i, j)),
        out_shape=jax.ShapeDtypeStruct((m, n), BF16),
        scratch_shapes=[pltpu.VMEM((tm, d), BF16)],
        compiler_params=_cparams(("parallel", "arbitrary")),
        name="inproj",
    )(x, g, w)


def _cmp_mlp_kernel(k_ref, v_ref, pek_ref, w1k_ref, w2k_ref, pev_ref, w1v_ref, w2v_ref,
                    kc_ref, vc_ref):
    half = CMP_STRIDE * HEAD_DIM

    def one(x_ref, pe_ref, w1_ref, w2_ref, o_ref):
        x = x_ref[0, 0].astype(F32)
        ncp = x.shape[0]
        xa = (x + pe_ref[0:1, :]).astype(BF16)
        xb = (x + pe_ref[1:2, :]).astype(BF16)
        pa = _dot(xa, w1_ref[0:half, :])
        pb = _dot(xb, w1_ref[half:2 * half, :])
        hid = pa + pltpu.roll(pb, ncp - 1, 0)
        out = _dot(jax.nn.gelu(hid).astype(BF16), w2_ref[...])
        row = lax.broadcasted_iota(jnp.int32, out.shape, 0)
        o_ref[0, 0] = jnp.where(row < ncp - 1, out, 0.0).astype(BF16)

    one(k_ref, pek_ref, w1k_ref, w2k_ref, kc_ref)
    one(v_ref, pev_ref, w1v_ref, w2v_ref, vc_ref)


def _cmp_mlp(k4, v4, pek, w1k, w2k, pev, w1v, w2v):
    b, hkv, ncp, width = k4.shape
    blk = pl.BlockSpec((1, 1, ncp, width), lambda i, j: (i, j, 0, 0))
    oblk = pl.BlockSpec((1, 1, ncp, HEAD_DIM), lambda i, j: (i, j, 0, 0))

    def full(a):
        return pl.BlockSpec(a.shape, lambda i, j: (0,) * a.ndim)

    out = jax.ShapeDtypeStruct((b, hkv, ncp, HEAD_DIM), BF16)
    return pl.pallas_call(
        _cmp_mlp_kernel,
        grid=(b, hkv),
        in_specs=[blk, blk, full(pek), full(w1k), full(w2k), full(pev), full(w1v), full(w2v)],
        out_specs=[oblk, oblk],
        out_shape=[out, out],
        compiler_params=_cparams(("parallel", "parallel")),
        name="cmp_mlp",
    )(k4, v4, pek, w1k, w2k, pev, w1v, w2v)


def _keep_lanes(q, keep):
    group = lax.broadcasted_iota(jnp.int32, (1, q.shape[1]), 1) >> HEAD_SHIFT
    mask = functools.reduce(jnp.logical_or, [group == g for g, k in enumerate(keep) if k])
    return q * jnp.where(mask, 1.0, 0.0).astype(q.dtype)


def _head_rows(q_all):
    return [_keep_lanes(q_all, [g == h for g in range(GROUP)]) for h in range(GROUP)]


def _stack_heads(q_all, upper):
    q32 = q_all.astype(F32)
    lane = lax.broadcasted_iota(jnp.int32, upper.shape, 1)
    rows = []
    for pair in range(GROUP // 2):
        half = q32[:, pair * LANE:(pair + 1) * LANE]
        rows.append(jnp.where(lane < HEAD_DIM, half, upper))
        rows.append(jnp.where(lane < HEAD_DIM, pltpu.roll(half, HEAD_DIM, 1), upper))
    return jnp.concatenate(rows, axis=0).astype(BF16)


def _split_kv(kv_ref, kt_sc, v_sc, key_rows=None):
    seq = kv_ref.shape[0]
    chunk = min(KV_SPLIT_CHUNK, seq)
    copies = kt_sc.shape[0] // HEAD_DIM - (key_rows is not None)
    lane = lax.broadcasted_iota(jnp.int32, (chunk, LANE), 1)
    for c in range(seq // chunk):
        rows = slice(c * chunk, (c + 1) * chunk)
        x = kv_ref[rows, :].astype(F32)
        kt = x.T[0:HEAD_DIM, :].astype(BF16)
        for r in range(copies):
            kt_sc[r * HEAD_DIM:(r + 1) * HEAD_DIM, rows] = kt
        v_sc[rows, :] = jnp.where(lane < HEAD_DIM, pltpu.roll(x, HEAD_DIM, 1), 1.0).astype(BF16)
    if key_rows is not None:
        kt_sc[copies * HEAD_DIM:(copies + 1) * HEAD_DIM, :] = key_rows


def _finish_pair_rolled(acc_even, acc_odd):
    lane = lax.broadcasted_iota(jnp.int32, acc_even.shape, 1)
    even = acc_even / pltpu.roll(acc_even, HEAD_DIM, 1)
    odd = pltpu.roll(acc_odd, HEAD_DIM, 1) / acc_odd
    return jnp.where(lane < HEAD_DIM, even, odd)


def _finish_group(accs):
    return jnp.concatenate([_finish_pair_rolled(accs[0], accs[1]),
                            _finish_pair_rolled(accs[2], accs[3])], axis=1)


def _online_update(s, v, m_ref, acc_ref, idx):
    m_old = m_ref[idx]
    m_new = jnp.maximum(m_old, jnp.max(s, axis=-1, keepdims=True))
    alpha = jnp.exp2(m_old - m_new)
    p = jnp.exp2(s - jnp.concatenate([m_new] * (s.shape[1] // LANE), axis=1))
    acc = acc_ref[idx]
    acc_ref[idx] = (jnp.concatenate([alpha] * (acc.shape[1] // LANE), axis=1) * acc
                    + _dot(p.astype(BF16), v))
    m_ref[idx] = m_new


def _cmp_attn_kernel(q_ref, kc_ref, vc_ref, bias_ref, ovl_ref, o_ref, sel_ref, *, tq, ncp, nsel):
    qi = pl.program_id(1)
    vc = vc_ref[0, 0]
    lane = lax.broadcasted_iota(jnp.int32, (tq, LANE), 1)
    row = lax.broadcasted_iota(jnp.int32, (tq, ncp), 0) + qi * tq
    col = lax.broadcasted_iota(jnp.int32, (tq, ncp), 1)
    valid = (col * CMP_STRIDE + (CMP_LEN - 1)) <= row
    p_sum = jnp.zeros((tq, ncp), F32)
    outs = []
    s_all = _dot(jnp.concatenate(_head_rows(q_ref[...]), axis=0), kc_ref[0, 0])
    for g in range(GROUP):
        s = s_all[g * tq:(g + 1) * tq] + bias_ref[0, g]
        s = jnp.where(valid, s, NEG_INF)
        e = jnp.where(valid, jnp.exp2(s - jnp.max(s, axis=-1, keepdims=True)), 0.0)
        den = jnp.sum(e, axis=-1, keepdims=True)
        p = e / jnp.where(den > 0, den, 1.0)
        outs.append(_dot(p.astype(BF16), vc))
        p_sum = p_sum + p
    o_ref[...] = jnp.concatenate([jnp.where(lane < HEAD_DIM, outs[0], outs[1]),
                                  jnp.where(lane < HEAD_DIM, outs[2], outs[3])],
                                 axis=1).astype(BF16)

    imp = _dot_nt(ovl_ref[...], p_sum, precision=lax.Precision.HIGHEST)
    jj = lax.broadcasted_iota(jnp.int32, (nsel, tq), 0)
    cur = (lax.broadcasted_iota(jnp.int32, (nsel, tq), 1) + qi * tq) >> SEL_SHIFT
    forced = (jj == 0) | (jj == cur) | (jj == cur - 1)
    score = jnp.where(forced, SEL_FORCE, jnp.where(jj <= cur, imp, -SEL_FORCE))
    sub = SUBLANES
    groups = [score[g * sub:(g + 1) * sub, :] for g in range(nsel // sub)]
    ranks = [jnp.zeros((sub, tq), F32) for _ in groups]
    jg = lax.broadcasted_iota(jnp.int32, (sub, tq), 0)
    for i in range(nsel):
        si = score[i:i + 1, :]
        for g, sg in enumerate(groups):
            if g * sub > i:
                ahead = jnp.where(si >= sg, 1.0, 0.0)
            elif (g + 1) * sub <= i:
                ahead = jnp.where(si > sg, 1.0, 0.0)
            else:
                ahead = jnp.where(jg + g * sub > i, jnp.where(si >= sg, 1.0, 0.0),
                                  jnp.where(si > sg, 1.0, 0.0))
            ranks[g] = ranks[g] + ahead
    rank = jnp.concatenate(ranks, axis=0)
    dropped = jnp.where(rank < min(SEL_TOPK, nsel), 0.0, 1.0)
    pieces = [jnp.zeros((HEAD_DIM, tq), F32), dropped]
    if nsel < LANE - HEAD_DIM:
        pieces.append(jnp.zeros((LANE - HEAD_DIM - nsel, tq), F32))
    sel_ref[0, 0] = jnp.concatenate(pieces, axis=0).T.astype(BF16)


def _cmp_attn(z, kc4, vc, bias_c, ovl_t, *, tq, seq):
    b, hkv, _, ncp = kc4.shape
    m = z.shape[0]
    nq = seq // tq
    nsel = seq // SEL_LEN
    kern = functools.partial(_cmp_attn_kernel, tq=tq, ncp=ncp, nsel=nsel)
    qblk = ZB_NQ * LANE // GROUP_WIDTH
    return pl.pallas_call(
        kern,
        grid=(hkv, nq, b),
        in_specs=[
            pl.BlockSpec((tq, GROUP_WIDTH), lambda h, i, bb: (bb * nq + i, qblk + h)),
            pl.BlockSpec((1, 1, GROUP_WIDTH, ncp), lambda h, i, bb: (bb, h, 0, 0)),
            pl.BlockSpec((1, 1, ncp, LANE), lambda h, i, bb: (bb, h, 0, 0)),
            pl.BlockSpec((1, GROUP, tq, ncp), lambda h, i, bb: (h, 0, i, 0)),
            pl.BlockSpec((nsel, ncp), lambda h, i, bb: (0, 0)),
        ],
        out_specs=[
            pl.BlockSpec((tq, GROUP_WIDTH), lambda h, i, bb: (bb * nq + i, h)),
            pl.BlockSpec((1, 1, tq, LANE), lambda h, i, bb: (bb, h, i, 0)),
        ],
        out_shape=[
            jax.ShapeDtypeStruct((m, BRANCH_WIDTH), BF16),
            jax.ShapeDtypeStruct((b, hkv, seq, LANE), BF16),
        ],
        compiler_params=_cparams(("parallel", "parallel", "parallel")),
        name="cmp_attn",
    )(z, kc4, vc, bias_c, ovl_t)


def _band_attn_kernel(*refs, tq, w, back, n_off, has_sink):
    if has_sink:
        q_ref, kv_ref, bias_ref, sink_ref, o_ref, kt_sc, v_sc = refs
    else:
        q_ref, kv_ref, bias_ref, o_ref, kt_sc, v_sc = refs
    qi = pl.program_id(2)

    @pl.when(qi == 0)
    def _():
        _split_kv(kv_ref, kt_sc, v_sc)

    off = jnp.minimum(qi * tq, back)
    start = pl.multiple_of(qi * tq - off, LANE)
    q = jnp.concatenate(_head_rows(q_ref[...]), axis=0)
    s_all = _dot(q, kt_sc[:, pl.ds(start, w)])
    v1 = v_sc[pl.ds(start, w), :]
    tile = jnp.minimum(qi, n_off - 1)
    sum_lane = lax.broadcasted_iota(jnp.int32, (tq, LANE), 1) >= HEAD_DIM
    accs = []
    for h in range(GROUP):
        s = s_all[h * tq:(h + 1) * tq] + bias_ref[0, tile, h]
        m = jnp.broadcast_to(jnp.max(s, axis=-1, keepdims=True), (tq, LANE))
        if has_sink:
            sink = sink_ref[0, h]
            m = jnp.maximum(m, sink)
        p = jnp.exp2(s - jnp.concatenate([m] * (w // LANE), axis=1))
        acc = _dot(p.astype(BF16), v1)
        if has_sink:
            acc = acc + jnp.where(sum_lane, jnp.exp2(sink - m), 0.0)
        accs.append(acc)
    o_ref[...] = _finish_group(accs).astype(BF16)


def _sel_attn_kernel(q_ref, kv_ref, drop_ref, rows_ref, bias_ref, o_ref, kt_sc, v_sc, m_sc, acc_sc,
                     *, tq, tk, n_bias):
    qi = pl.program_id(2)

    @pl.when(qi == 0)
    def _():
        _split_kv(kv_ref, kt_sc, v_sc, rows_ref[...])

    q = _stack_heads(q_ref[...], drop_ref[0, 0].astype(F32))
    hi = ((qi + 1) * tq + tk - 1) // tk
    m_sc[...] = jnp.full(m_sc.shape, NEG_INF, F32)
    acc_sc[...] = jnp.zeros(acc_sc.shape, F32)

    def step(kj):
        start = pl.multiple_of(kj * tk, tk)
        s_all = _dot(q, kt_sc[:, pl.ds(start, tk)])
        v1 = v_sc[pl.ds(start, tk), :]
        u = jnp.minimum((qi * tq - kj * tk) // tq, n_bias - 1)
        for h in range(GROUP):
            s = s_all[h * tq:(h + 1) * tq] + bias_ref[0, u, h]
            _online_update(s, v1, m_sc, acc_sc, h)

    odd = hi % 2

    @pl.when(odd == 1)
    def _():
        step(hi - 1)

    def body(i, carry):
        kj = hi - odd - 1 - 2 * i
        step(kj)
        step(kj - 1)
        return carry

    lax.fori_loop(0, hi // 2, body, 0)
    o_ref[...] = _finish_group([acc_sc[h] for h in range(GROUP)]).astype(BF16)


def _gqa_specs(z, q_block, kv_block, tq, seq):
    nq = seq // tq
    qblk = q_block * LANE // GROUP_WIDTH
    return [
        pl.BlockSpec((tq, GROUP_WIDTH), lambda h, bb, i: (bb * nq + i, qblk + h)),
        pl.BlockSpec((seq, LANE), lambda h, bb, i: (bb, kv_block + h)),
    ]


def _resident(a):
    return pl.BlockSpec((1,) + a.shape[1:], lambda h, bb, i: (h,) + (0,) * (a.ndim - 1),
                        pipeline_mode=pl.Buffered(1))


def _gqa_out(m, tq, seq):
    nq = seq // tq
    return (pl.BlockSpec((tq, GROUP_WIDTH), lambda h, bb, i: (bb * nq + i, h)),
            jax.ShapeDtypeStruct((m, BRANCH_WIDTH), BF16))


def _kv_scratch(seq, key_rows):
    return [pltpu.VMEM((key_rows, seq), BF16), pltpu.VMEM((seq, LANE), BF16)]


def _band_attention(z, bias, *, q_block, kv_block, tq, seq, back, name, sinks=None):
    m = z.shape[0]
    b = m // seq
    n_off, w = bias.shape[1], bias.shape[4]
    assert w == back + tq and w <= seq and back % LANE == 0
    kern = functools.partial(_band_attn_kernel, tq=tq, w=w, back=back, n_off=n_off,
                             has_sink=sinks is not None)
    in_specs = _gqa_specs(z, q_block, kv_block, tq, seq) + [_resident(bias)]
    args = [z, z, bias]
    if sinks is not None:
        in_specs.append(pl.BlockSpec((1, GROUP, 1, 1), lambda h, bb, i: (h, 0, 0, 0)))
        args.append(sinks)
    out_spec, out_shape = _gqa_out(m, tq, seq)
    return pl.pallas_call(
        kern,
        grid=(N_KV, b, seq // tq),
        in_specs=in_specs,
        out_specs=out_spec,
        out_shape=out_shape,
        scratch_shapes=_kv_scratch(seq, GROUP_WIDTH),
        compiler_params=_cparams(("parallel", "parallel", "arbitrary")),
        name=name,
    )(*args)


def _sel_attention(z, dropped, sel_rows, bias, *, tq, tk, seq):
    m = z.shape[0]
    b = m // seq
    kern = functools.partial(_sel_attn_kernel, tq=tq, tk=tk, n_bias=bias.shape[1])
    in_specs = _gqa_specs(z, ZB_NQ, ZB_SEL_KV, tq, seq) + [
        pl.BlockSpec((1, 1, tq, LANE), lambda h, bb, i: (bb, h, i, 0)),
        pl.BlockSpec((HEAD_DIM, seq), lambda h, bb, i: (0, 0), pipeline_mode=pl.Buffered(1)),
        _resident(bias),
    ]
    out_spec, out_shape = _gqa_out(m, tq, seq)
    return pl.pallas_call(
        kern,
        grid=(N_KV, b, seq // tq),
        in_specs=in_specs,
        out_specs=out_spec,
        out_shape=out_shape,
        scratch_shapes=_kv_scratch(seq, LANE) + [pltpu.VMEM((GROUP, tq, LANE), F32),
                                                 pltpu.VMEM((GROUP, tq, LANE), F32)],
        compiler_params=_cparams(("parallel", "parallel", "arbitrary")),
        name="attn_sel",
    )(z, z, dropped, sel_rows, bias)


def _rope_lanes(x, cos_t, sin_a, sin_b):
    half = MLA_ROPE // 2
    return x * cos_t + pltpu.roll(x, LANE - half, 1) * sin_a + pltpu.roll(x, half, 1) * sin_b


def _mla_proj_kernel(mq_ref, mkv_ref, kr_ref, cos_ref, sina_ref, sinb_ref, qn_ref, wq_ref,
                     kvn_ref, wkt_ref, wv_ref, oq_ref, okt_ref, ov_ref):
    cos_t, sin_a, sin_b = cos_ref[...], sina_ref[...], sinb_ref[...]
    ql = _rms(mq_ref[...].astype(F32), qn_ref[...]).astype(BF16)
    q = _dot(ql, wq_ref[...]) * (MLA_QK ** -0.5 * LOG2E)
    kvl = _rms(mkv_ref[...].astype(F32), kvn_ref[...]).astype(BF16)
    kt = _dot_nt(wkt_ref[...], kvl)
    v = _dot(kvl, wv_ref[...])
    kr_t = _rope_lanes(kr_ref[...].astype(F32), cos_t, sin_a, sin_b).T
    lane = lax.broadcasted_iota(jnp.int32, (v.shape[0], LANE), 1)
    for h in range(N_HEADS):
        blk = slice(h * LANE, (h + 1) * LANE)
        oq_ref[:, blk] = _rope_lanes(q[:, blk], cos_t, sin_a, sin_b).astype(BF16)
        okt_ref[0, blk, :] = (kt[blk, :] + kr_t).astype(BF16)
        ov_ref[:, blk] = jnp.where(lane < MLA_V, v[:, blk], 1.0).astype(BF16)


def _mla_proj(z, cos_t, sin_a, sin_b, q_norm, w_qb, kv_norm, w_kt, w_v, *, tm, seq):
    m = z.shape[0]
    nt = seq // tm
    width = N_HEADS * LANE

    def full(a):
        return pl.BlockSpec(a.shape, lambda i: (0,) * a.ndim)

    tab = pl.BlockSpec((tm, LANE), lambda i: (i % nt, 0))
    out = jax.ShapeDtypeStruct((m, width), BF16)
    oblk = pl.BlockSpec((tm, width), lambda i: (i, 0))
    return pl.pallas_call(
        _mla_proj_kernel,
        grid=(m // tm,),
        in_specs=[
            pl.BlockSpec((tm, MLA_Q_RANK), lambda i: (i, ZB_MQ * LANE // MLA_Q_RANK)),
            pl.BlockSpec((tm, MLA_KV_RANK), lambda i: (i, ZB_MKV * LANE // MLA_KV_RANK)),
            pl.BlockSpec((tm, LANE), lambda i: (i, ZB_KR)),
            tab, tab, tab,
            full(q_norm), full(w_qb), full(kv_norm), full(w_kt), full(w_v),
        ],
        out_specs=[oblk, pl.BlockSpec((1, width, tm), lambda i: (i // nt, 0, i % nt)), oblk],
        out_shape=[out, jax.ShapeDtypeStruct((m // seq, width, seq), BF16), out],
        compiler_params=_cparams(("parallel",)),
        name="mla_proj",
    )(z, z, z, cos_t, sin_a, sin_b, q_norm, w_qb, kv_norm, w_kt, w_v)


def _mla_attn_kernel(q_ref, k_ref, v_ref, o_ref, m_sc, acc_sc, *, tq):
    qi = pl.program_id(2)
    n_chain = m_sc.shape[0]
    qs = [q_ref[:, c * LANE:(c + 1) * LANE] for c in range(n_chain)]
    m_sc[...] = jnp.full(m_sc.shape, NEG_INF, F32)
    acc_sc[...] = jnp.zeros(acc_sc.shape, F32)

    def step(tile, width, diagonal):
        start = pl.multiple_of(tile * tq, tq)
        for c in range(n_chain):
            blk = slice(c * LANE, (c + 1) * LANE)
            s = _dot(qs[c], k_ref[0, blk, pl.ds(start, width * tq)])
            if diagonal:
                row = lax.broadcasted_iota(jnp.int32, s.shape, 0)
                col = lax.broadcasted_iota(jnp.int32, s.shape, 1)
                s = jnp.where(col <= row, s, NEG_INF)
            _online_update(s, v_ref[pl.ds(start, width * tq), blk], m_sc, acc_sc, c)

    step(qi, 1, True)
    odd = qi % 2

    @pl.when(odd == 1)
    def _():
        step(qi - 1, 1, False)

    def body(i, carry):
        step(qi - odd - 2 * (i + 1), 2, False)
        return carry

    lax.fori_loop(0, qi // 2, body, 0)
    for c in range(n_chain):
        acc = acc_sc[c]
        lane = lax.broadcasted_iota(jnp.int32, acc.shape, 1)
        den = jnp.where(lane < MLA_V, pltpu.roll(acc, MLA_V, 1), acc)
        o_ref[:, c * LANE:(c + 1) * LANE] = (acc / den).astype(BF16)


def _mla_attention(q, k, v1, *, tq, seq):
    m = q.shape[0]
    b = m // seq
    nq = seq // tq
    width = MLA_CHAINS * LANE
    slab = pl.BlockSpec((seq, width), lambda h, bb, i: (bb, h))
    slab_t = pl.BlockSpec((1, width, seq), lambda h, bb, i: (bb, h, 0))
    tile = pl.BlockSpec((tq, width), lambda h, bb, i: (bb * nq + i, h))
    return pl.pallas_call(
        functools.partial(_mla_attn_kernel, tq=tq),
        grid=(N_HEADS // MLA_CHAINS, b, nq),
        in_specs=[tile, slab_t, slab],
        out_specs=tile,
        out_shape=jax.ShapeDtypeStruct(q.shape, BF16),
        scratch_shapes=[pltpu.VMEM((MLA_CHAINS, tq, LANE), F32),
                        pltpu.VMEM((MLA_CHAINS, tq, LANE), F32)],
        compiler_params=_cparams(("parallel", "parallel", "arbitrary")),
        name="attn_mla",
    )(q, k, v1)


def _merge_kernel(ocmp_ref, osel_ref, owin_ref, ob_ref, oc_ref, ng_ref, mg_ref, x_ref,
                  gexp_ref, wab_ref, wc_ref, wo_ref, gpost_ref, o_ref):
    ng = jax.nn.sigmoid(ng_ref[...].astype(F32))
    ng_hi = ng.astype(BF16)
    ng_lo = (ng - ng_hi.astype(F32)).astype(BF16)
    gates = _dot(jnp.concatenate([ng_hi, ng_lo], axis=1), gexp_ref[...])
    o_a = jnp.zeros(ocmp_ref.shape, F32)
    for n, ref in enumerate((ocmp_ref, osel_ref, owin_ref)):
        gate = gates[:, n * BRANCH_WIDTH:(n + 1) * BRANCH_WIDTH]
        o_a = o_a + gate * ref[...].astype(F32)
    ys = (_dot(o_a.astype(BF16), wab_ref[0]), _dot(ob_ref[...], wab_ref[1]),
          _dot(oc_ref[...], wc_ref[...]))
    mixed = jnp.zeros(x_ref.shape, F32)
    for n, y in enumerate(ys):
        gate = jax.nn.sigmoid(mg_ref[:, n * D_MODEL:(n + 1) * D_MODEL].astype(F32))
        mixed = mixed + gate * y
    out = _dot(mixed.astype(BF16), wo_ref[...])
    o_ref[...] = x_ref[...] + _rms(out, gpost_ref[...])


def _merge(o_cmp, o_sel, o_win, o_b, o_c, z, x, gexp, w_ab, w_c, w_out, g_post, *, tm):
    m, d = x.shape
    br = pl.BlockSpec((tm, o_cmp.shape[1]), lambda i: (i, 0))

    def full(a):
        return pl.BlockSpec(a.shape, lambda i: (0,) * a.ndim)

    return pl.pallas_call(
        _merge_kernel,
        grid=(m // tm,),
        in_specs=[
            br, br, br, br,
            pl.BlockSpec((tm, o_c.shape[1]), lambda i: (i, 0)),
            pl.BlockSpec((tm, LANE), lambda i: (i, ZB_NGATE)),
            pl.BlockSpec((tm, 3 * d), lambda i: (i, ZB_MGATE * LANE // (3 * d))),
            pl.BlockSpec((tm, d), lambda i: (i, 0)),
            full(gexp), full(w_ab), full(w_c), full(w_out), full(g_post),
        ],
        out_specs=pl.BlockSpec((tm, d), lambda i: (i, 0)),
        out_shape=jax.ShapeDtypeStruct((m, d), F32),
        compiler_params=_cparams(("parallel",)),
        name="merge",
    )(o_cmp, o_sel, o_win, o_b, o_c, z, z, x, gexp, w_ab, w_c, w_out, g_post)


def _t5_bucket(dist):
    max_exact = NUM_BUCKETS // 2
    d = jnp.maximum(dist, 0)
    df = jnp.maximum(d, 1).astype(F32)
    large = max_exact + (jnp.log(df / max_exact) / math.log(MAX_DISTANCE / max_exact)
                         * (NUM_BUCKETS - max_exact)).astype(jnp.int32)
    large = jnp.minimum(large, NUM_BUCKETS - 1)
    return jnp.where(d < max_exact, d, large)


def _toeplitz(fn, offs, tq, tk):
    ln = tq + tk - 1
    i = jnp.arange(ln + 1)
    shift = jnp.where(i < tk, -i, ln + 1 - i)
    ext = jnp.moveaxis(fn(jnp.asarray(offs)[:, None] + shift[None, :]), -1, 0)
    flat = jnp.tile(ext, (1, 1, tq))[:, :, :tq * ln]
    return flat.reshape(ext.shape[0], len(offs), tq, ln)[:, :, :, :tk]


def _bias_fn(table, window):
    def fn(dist):
        ok = dist >= 0
        if window is not None:
            ok = ok & (dist < window)
        return jnp.where(ok[..., None], table[_t5_bucket(dist)].astype(F32), NEG_INF)
    return fn


def _head_tiles(t):
    return t.reshape(N_KV, GROUP, *t.shape[1:]).transpose(0, 2, 1, 3, 4)


def _in_proj_weight(w_in):
    sizes = (512, 128, 128, 128, 128, 128, 128, 24, 512, 128, 128, MLA_Q_RANK, MLA_KV_RANK,
             MLA_ROPE, 3 * D_MODEL)
    offs = np.concatenate([[0], np.cumsum(sizes)])
    seg = [w_in[:, offs[i]:offs[i + 1]] for i in range(len(sizes))]
    (nq, nkc, nvc, nks, nvs, nkw, nvw, ngate, sq, sk, sv, mq, mkv, mkr, mgate) = seg
    d = w_in.shape[0]
    hd = HEAD_DIM

    def pad(a, left=0):
        return jnp.pad(a, ((0, 0), (left, LANE - left - a.shape[1])))

    def kv_blocks(k, v):
        return [jnp.concatenate([k[:, h * hd:(h + 1) * hd], v[:, h * hd:(h + 1) * hd]], axis=1)
                for h in range(N_KV)]

    cols = ([mq, pad(ngate), mkv, pad(mkr, MLA_NOPE), jnp.zeros((d, LANE), w_in.dtype),
             nq * QK_SCALE, nkc, nvc] + kv_blocks(nks, nvs) + kv_blocks(nkw, nvw)
            + [sq * QK_SCALE] + kv_blocks(sk, sv) + [mgate])
    w = jnp.concatenate(cols, axis=1)
    assert w.shape == (d, Z_WIDTH)
    return w.astype(BF16)


def _mla_weights(w_qb, w_kvb, w_c):
    def blocks(a):
        return jnp.pad(a, ((0, 0), (0, 0), (0, LANE - a.shape[2]))).reshape(a.shape[0], -1)

    wq = blocks(w_qb.reshape(MLA_Q_RANK, N_HEADS, MLA_QK))
    wkv = w_kvb.reshape(MLA_KV_RANK, N_HEADS, MLA_NOPE + MLA_V)
    wkt = blocks(wkv[:, :, :MLA_NOPE]).T
    wv = blocks(wkv[:, :, MLA_NOPE:])
    wc = jnp.pad(w_c.reshape(N_HEADS, MLA_V, -1), ((0, 0), (0, LANE - MLA_V), (0, 0)))
    return (wq.astype(BF16), wkt.astype(BF16), wv.astype(BF16),
            wc.reshape(N_HEADS * LANE, -1).astype(BF16))


def _gate_expand():
    e = np.zeros((2 * LANE, 3 * BRANCH_WIDTH), np.float32)
    for n in range(3):
        for h in range(N_HEADS):
            cols = slice(n * BRANCH_WIDTH + h * HEAD_DIM, n * BRANCH_WIDTH + (h + 1) * HEAD_DIM)
            e[h * 3 + n, cols] = 1.0
            e[LANE + h * 3 + n, cols] = 1.0
    return jnp.asarray(e, BF16)


def _rope_tables(pos):
    half = MLA_ROPE // 2
    inv_freq = ROPE_BASE ** (-jnp.arange(half, dtype=F32) / half)
    ang = pos.astype(F32)[:, None] * inv_freq[None, :]
    cos, sin = jnp.cos(ang), jnp.sin(ang)
    n = pos.shape[0]
    ones = jnp.ones((n, MLA_NOPE), F32)
    z16 = jnp.zeros((n, half), F32)
    tail = jnp.zeros((n, LANE - MLA_QK), F32)
    cos_t = jnp.concatenate([ones, cos, cos, tail], axis=1)
    sin_a = jnp.concatenate([0 * ones, -sin, z16, tail], axis=1)
    sin_b = jnp.concatenate([0 * ones, z16, sin, tail], axis=1)
    return cos_t, sin_a, sin_b


def kernel(x, rel_bias_table, ffn1_norm_pre, ffn1_w_gu, ffn1_w_down, ffn1_norm_post,
           mix_norm_pre, w_in, nsa_pe_k, nsa_w1_k, nsa_w2_k, nsa_pe_v, nsa_w1_v, nsa_w2_v,
           swa_sinks, mla_q_norm, mla_w_qb, mla_kv_norm, mla_w_kvb, w_branch, w_out,
           mix_norm_post, ffn2_norm_pre, ffn2_w_gu, ffn2_w_down, ffn2_norm_post):
    b, s, d = x.shape
    depth = w_in.shape[0]
    m = b * s
    tq = 256
    tq_band = min(256, s // 2)
    tq_sel = min(512, s)
    tk_sel = min(512, s)
    tq_mla = min(512, s)
    ncp = s // CMP_STRIDE
    nsel = s // SEL_LEN
    tm = min(512, m)
    tm_big = min(1024, s)

    pos = jnp.arange(s, dtype=jnp.int32)
    tab_a = rel_bias_table[:, :N_HEADS] * LOG2E
    tab_b = rel_bias_table[:, N_HEADS:] * LOG2E
    n_far = -(-(LAST_BUCKET_DIST + tk_sel - 1) // tq_sel)
    n_sel_tiles = min(n_far + 1, s // tq_sel)
    bias_sel = _head_tiles(_toeplitz(_bias_fn(tab_a, None),
                                     [u * tq_sel for u in range(n_sel_tiles)], tq_sel, tk_sel))
    tb = tq_band
    win_offs = sorted({min(i * tb, NSA_WINDOW) for i in range(-(-NSA_WINDOW // tb) + 1)})
    bias_win = _head_tiles(_toeplitz(_bias_fn(tab_a, NSA_WINDOW), win_offs, tb, NSA_WINDOW + tb))
    swa_offs = sorted({min(i * tb, SWA_WINDOW) for i in range(-(-SWA_WINDOW // tb) + 1)})
    bias_swa = _head_tiles(_toeplitz(_bias_fn(tab_b, SWA_WINDOW), swa_offs, tb, SWA_WINDOW + tb))

    def cmp_bias_fn(e):
        r = jnp.arange(CMP_STRIDE)
        dist = e[..., None] * CMP_STRIDE + r - (CMP_LEN - 1)
        return tab_a[_t5_bucket(dist)].astype(F32).reshape(*e.shape, CMP_STRIDE * N_HEADS)

    bias_c = _toeplitz(cmp_bias_fn, [0], ncp, ncp).reshape(CMP_STRIDE, N_HEADS, ncp, ncp)
    bias_c = bias_c.transpose(1, 2, 0, 3).reshape(N_KV, GROUP, s, ncp)
    ci = jnp.arange(ncp)[None, :] * CMP_STRIDE
    sj = jnp.arange(nsel)[:, None] * SEL_LEN
    ovl_t = (jnp.maximum(jnp.minimum(ci + CMP_LEN, sj + SEL_LEN) - jnp.maximum(ci, sj), 0)
             .astype(F32) / CMP_LEN)
    cos_t, sin_a, sin_b = _rope_tables(pos)
    gexp = _gate_expand()
    assert nsel <= HEAD_DIM
    sel_rows = jnp.where(jnp.arange(HEAD_DIM)[:, None] == (pos[None, :] >> SEL_SHIFT),
                         NEG_INF, 0.0).astype(BF16)
    chunk = CMP_STRIDE * HEAD_DIM

    xf = x.reshape(m, d)
    for l in range(depth):
        xf = _ffn(xf, ffn1_norm_pre[l][None], ffn1_w_gu[l].astype(BF16),
                  ffn1_w_down[l].astype(BF16), ffn1_norm_post[l][None], tm=tm)

        z = _inproj(xf, mix_norm_pre[l][None], _in_proj_weight(w_in[l]), tm=tm_big, tn=Z_WIDTH // 4)

        def chunks(blk):
            a = z[:, blk * LANE:(blk + 1) * LANE].reshape(b, s, N_KV, HEAD_DIM)
            return a.transpose(0, 2, 1, 3).reshape(b, N_KV, ncp, chunk)

        kc, vc = _cmp_mlp(chunks(ZB_CMP_K), chunks(ZB_CMP_V),
                          nsa_pe_k[l].reshape(2, chunk), nsa_w1_k[l].astype(BF16),
                          nsa_w2_k[l].astype(BF16),
                          nsa_pe_v[l].reshape(2, chunk), nsa_w1_v[l].astype(BF16),
                          nsa_w2_v[l].astype(BF16))
        o_cmp, dropped = _cmp_attn(z, jnp.tile(kc.transpose(0, 1, 3, 2), (1, 1, GROUP, 1)),
                                   jnp.tile(vc, (1, 1, 1, 2)),
                                   bias_c, ovl_t, tq=tq, seq=s)
        o_sel = _sel_attention(z, dropped, sel_rows, bias_sel, tq=tq_sel, tk=tk_sel, seq=s)
        o_win = _band_attention(z, bias_win, q_block=ZB_NQ, kv_block=ZB_WIN_KV, tq=tq_band, seq=s,
                                back=NSA_WINDOW, name="attn_win")
        o_b = _band_attention(z, bias_swa, q_block=ZB_SQ, kv_block=ZB_SWA_KV, tq=tq_band, seq=s,
                              back=SWA_WINDOW, name="attn_swa",
                              sinks=(swa_sinks[l] * LOG2E).reshape(N_KV, GROUP, 1, 1))
        wq, wkt, wv, wc = _mla_weights(mla_w_qb[l], mla_w_kvb[l], w_branch[l, 2])
        mq, mk, mv1 = _mla_proj(z, cos_t, sin_a, sin_b, mla_q_norm[l][None], wq,
                                mla_kv_norm[l][None], wkt, wv, tm=tm_big, seq=s)
        o_c = _mla_attention(mq, mk, mv1, tq=tq_mla, seq=s)

        xf = _merge(o_cmp, o_sel, o_win, o_b, o_c, z, xf, gexp, w_branch[l, :2].astype(BF16), wc,
                    w_out[l].astype(BF16), mix_norm_post[l][None], tm=tm)

        xf = _ffn(xf, ffn2_norm_pre[l][None], ffn2_w_gu[l].astype(BF16),
                  ffn2_w_down[l].astype(BF16), ffn2_norm_post[l][None], tm=tm)
    return xf.reshape(b, s, d)
```

```python
import functools
import math

import jax
import jax.numpy as jnp
import numpy as np
from jax import lax
from jax.experimental import pallas as pl
from jax.experimental.pallas import tpu as pltpu

F32 = jnp.float32
BF16 = jnp.bfloat16

D_MODEL = 1024
D_FF = 2816
HEAD_DIM = 64
NORM_EPS = 1e-6
NUM_BUCKETS = 32
MAX_DISTANCE = 1024
LAST_BUCKET_DIST = 1 + math.ceil(
    (NUM_BUCKETS // 2) * (MAX_DISTANCE / (NUM_BUCKETS // 2))
    ** ((NUM_BUCKETS // 2 - 1) / (NUM_BUCKETS - NUM_BUCKETS // 2)))
N_HEADS = 8
N_KV = 2
GROUP = N_HEADS // N_KV
CMP_LEN = 32
CMP_STRIDE = 16
CMP_HIDDEN = 128
SEL_LEN = 64
SEL_SHIFT = 6
SEL_TOPK = 16
NSA_WINDOW = 512
SWA_WINDOW = 128
MLA_Q_RANK = 384
MLA_KV_RANK = 256
MLA_NOPE = 64
MLA_ROPE = 32
MLA_V = 64
MLA_QK = MLA_NOPE + MLA_ROPE
ROPE_BASE = 10000.0
BRANCH_WIDTH = N_HEADS * HEAD_DIM
NEG_INF = -1e30
SEL_FORCE = 1e9
LOG2E = math.log2(math.e)
QK_SCALE = HEAD_DIM ** -0.5 * LOG2E

LANE = 128
SUBLANES = 8
VMEM_LIMIT = 56 * 1024 * 1024
HEAD_SHIFT = 6
GROUP_WIDTH = GROUP * HEAD_DIM
QK_SPLIT = 2
KV_SPLIT_CHUNK = 512
MLA_CHAINS = 4

ZB_MQ = 0
ZB_NGATE = 3
ZB_MKV = 4
ZB_KR = 6
ZB_NQ = 8
ZB_CMP_K = 12
ZB_CMP_V = 13
ZB_SEL_KV = 14
ZB_WIN_KV = 16
ZB_SQ = 18
ZB_SWA_KV = 22
ZB_MGATE = 24
Z_BLOCKS = 48
Z_WIDTH = Z_BLOCKS * LANE


def _cparams(sem):
    return pltpu.CompilerParams(dimension_semantics=sem, vmem_limit_bytes=VMEM_LIMIT)


def _rms(x, g):
    inv = lax.rsqrt(jnp.mean(x * x, axis=-1, keepdims=True) + NORM_EPS)
    return (x * inv) * g


def _dot(a, b):
    return jnp.dot(a, b, preferred_element_type=F32)


def _dot_nt(a, b, precision=None):
    return lax.dot_general(a, b, (((1,), (1,)), ((), ())), precision=precision,
                           preferred_element_type=F32)


def _ffn_kernel(x_ref, gpre_ref, wg_ref, wu_ref, wd_ref, gpost_ref, o_ref):
    x = x_ref[...]
    h = _rms(x, gpre_ref[...]).astype(BF16)
    g = _dot(h, wg_ref[...])
    u = _dot(h, wu_ref[...])
    act = (g * jax.nn.sigmoid(g)) * u
    y = _dot(act.astype(BF16), wd_ref[...])
    o_ref[...] = x + 0.5 * _rms(y, gpost_ref[...])


def _ffn(x, g_pre, w_gu, w_down, g_post, *, tm):
    m, d = x.shape
    once = pl.Buffered(1)
    return pl.pallas_call(
        _ffn_kernel,
        grid=(m // tm,),
        in_specs=[
            pl.BlockSpec((tm, d), lambda i: (i, 0)),
            pl.BlockSpec((1, d), lambda i: (0, 0)),
            pl.BlockSpec((d, D_FF), lambda i: (0, 0), pipeline_mode=once),
            pl.BlockSpec((d, D_FF), lambda i: (0, 1), pipeline_mode=once),
            pl.BlockSpec((D_FF, d), lambda i: (0, 0), pipeline_mode=once),
            pl.BlockSpec((1, d), lambda i: (0, 0)),
        ],
        out_specs=pl.BlockSpec((tm, d), lambda i: (i, 0)),
        out_shape=jax.ShapeDtypeStruct((m, d), F32),
        compiler_params=_cparams(("parallel",)),
        name="ffn",
    )(x, g_pre, w_gu, w_gu, w_down, g_post)


def _inproj_kernel(x_ref, g_ref, w_ref, o_ref, h_sc):
    @pl.when(pl.program_id(1) == 0)
    def _():
        h_sc[...] = _rms(x_ref[...], g_ref[...]).astype(BF16)

    o_ref[...] = _dot(h_sc[...], w_ref[...]).astype(BF16)


def _inproj(x, g, w, *, tm, tn):
    m, d = x.shape
    n = w.shape[1]
    return pl.pallas_call(
        _inproj_kernel,
        grid=(m // tm, n // tn),
        in_specs=[
            pl.BlockSpec((tm, d), lambda i, j: (i, 0)),
            pl.BlockSpec((1, d), lambda i, j: (0, 0)),
            pl.BlockSpec((d, tn), lambda i, j: (0, j)),
        ],
        out_specs=pl.BlockSpec((tm, tn), lambda i, j: (i, j)),
        out_shape=jax.ShapeDtypeStruct((m, n), BF16),
        scratch_shapes=[pltpu.VMEM((tm, d), BF16)],
        compiler_params=_cparams(("parallel", "arbitrary")),
        name="inproj",
    )(x, g, w)


def _cmp_mlp_kernel(k_ref, v_ref, pek_ref, w1k_ref, w2k_ref, pev_ref, w1v_ref, w2v_ref,
                    kc_ref, vc_ref):
    half = CMP_STRIDE * HEAD_DIM

    def one(x_ref, pe_ref, w1_ref, w2_ref, o_ref):
        x = x_ref[0, 0].astype(F32)
        ncp = x.shape[0]
        xa = (x + pe_ref[0:1, :]).astype(BF16)
        xb = (x + pe_ref[1:2, :]).astype(BF16)
        pa = _dot(xa, w1_ref[0:half, :])
        pb = _dot(xb, w1_ref[half:2 * half, :])
        hid = pa + pltpu.roll(pb, ncp - 1, 0)
        out = _dot(jax.nn.gelu(hid).astype(BF16), w2_ref[...])
        row = lax.broadcasted_iota(jnp.int32, out.shape, 0)
        o_ref[0, 0] = jnp.where(row < ncp - 1, out, 0.0).astype(BF16)

    one(k_ref, pek_ref, w1k_ref, w2k_ref, kc_ref)
    one(v_ref, pev_ref, w1v_ref, w2v_ref, vc_ref)


def _cmp_mlp(k4, v4, pek, w1k, w2k, pev, w1v, w2v):
    b, hkv, ncp, width = k4.shape
    blk = pl.BlockSpec((1, 1, ncp, width), lambda i, j: (i, j, 0, 0))
    oblk = pl.BlockSpec((1, 1, ncp, HEAD_DIM), lambda i, j: (i, j, 0, 0))

    def full(a):
        return pl.BlockSpec(a.shape, lambda i, j: (0,) * a.ndim)

    out = jax.ShapeDtypeStruct((b, hkv, ncp, HEAD_DIM), BF16)
    return pl.pallas_call(
        _cmp_mlp_kernel,
        grid=(b, hkv),
        in_specs=[blk, blk, full(pek), full(w1k), full(w2k), full(pev), full(w1v), full(w2v)],
        out_specs=[oblk, oblk],
        out_shape=[out, out],
        compiler_params=_cparams(("parallel", "parallel")),
        name="cmp_mlp",
    )(k4, v4, pek, w1k, w2k, pev, w1v, w2v)


def _keep_lanes(q, keep):
    group = lax.broadcasted_iota(jnp.int32, (1, q.shape[1]), 1) >> HEAD_SHIFT
    mask = functools.reduce(jnp.logical_or, [group == g for g, k in enumerate(keep) if k])
    return q * jnp.where(mask, 1.0, 0.0).astype(q.dtype)


def _head_rows(q_all):
    return [_keep_lanes(q_all, [g == h for g in range(GROUP)]) for h in range(GROUP)]


def _stack_heads(q_all, upper):
    q32 = q_all.astype(F32)
    lane = lax.broadcasted_iota(jnp.int32, upper.shape, 1)
    rows = []
    for pair in range(GROUP // 2):
        half = q32[:, pair * LANE:(pair + 1) * LANE]
        rows.append(jnp.where(lane < HEAD_DIM, half, upper))
        rows.append(jnp.where(lane < HEAD_DIM, pltpu.roll(half, HEAD_DIM, 1), upper))
    return jnp.concatenate(rows, axis=0).astype(BF16)


def _split_kv(kv_ref, kt_sc, v_sc, key_rows=None):
    seq = kv_ref.shape[0]
    chunk = min(KV_SPLIT_CHUNK, seq)
    copies = kt_sc.shape[0] // HEAD_DIM - (key_rows is not None)
    lane = lax.broadcasted_iota(jnp.int32, (chunk, LANE), 1)
    for c in range(seq // chunk):
        rows = slice(c * chunk, (c + 1) * chunk)
        x = kv_ref[rows, :].astype(F32)
        kt = x.T[0:HEAD_DIM, :].astype(BF16)
        for r in range(copies):
            kt_sc[r * HEAD_DIM:(r + 1) * HEAD_DIM, rows] = kt
        v_sc[rows, :] = jnp.where(lane < HEAD_DIM, pltpu.roll(x, HEAD_DIM, 1), 1.0).astype(BF16)
    if key_rows is not None:
        kt_sc[copies * HEAD_DIM:(copies + 1) * HEAD_DIM, :] = key_rows


def _group_logits(q_stack, kt, tq):
    per = GROUP // QK_SPLIT
    parts = [_dot(q_stack[p * per * tq:(p + 1) * per * tq], kt) for p in range(QK_SPLIT)]
    return [parts[h // per][(h % per) * tq:(h % per + 1) * tq] for h in range(GROUP)]


def _finish_pair_rolled(acc_even, acc_odd):
    lane = lax.broadcasted_iota(jnp.int32, acc_even.shape, 1)
    even = acc_even / pltpu.roll(acc_even, HEAD_DIM, 1)
    odd = pltpu.roll(acc_odd, HEAD_DIM, 1) / acc_odd
    return jnp.where(lane < HEAD_DIM, even, odd)


def _finish_group(accs):
    return jnp.concatenate([_finish_pair_rolled(accs[0], accs[1]),
                            _finish_pair_rolled(accs[2], accs[3])], axis=1)


def _online_update(s, v, m_ref, acc_ref, idx):
    m_old = m_ref[idx]
    m_new = jnp.maximum(m_old, jnp.max(s, axis=-1, keepdims=True))
    alpha = jnp.exp2(m_old - m_new)
    p = jnp.exp2(s - jnp.concatenate([m_new] * (s.shape[1] // LANE), axis=1))
    acc = acc_ref[idx]
    acc_ref[idx] = (jnp.concatenate([alpha] * (acc.shape[1] // LANE), axis=1) * acc
                    + _dot(p.astype(BF16), v))
    m_ref[idx] = m_new


def _cmp_attn_kernel(q_ref, kc_ref, vc_ref, bias_ref, ovl_ref, o_ref, sel_ref, *, tq, ncp, nsel):
    qi = pl.program_id(1)
    vc = vc_ref[0, 0]
    lane = lax.broadcasted_iota(jnp.int32, (tq, LANE), 1)
    row = lax.broadcasted_iota(jnp.int32, (tq, ncp), 0) + qi * tq
    col = lax.broadcasted_iota(jnp.int32, (tq, ncp), 1)
    valid = (col * CMP_STRIDE + (CMP_LEN - 1)) <= row
    p_sum = jnp.zeros((tq, ncp), F32)
    outs = []
    logits = _group_logits(jnp.concatenate(_head_rows(q_ref[...]), axis=0), kc_ref[0, 0], tq)
    for g in range(GROUP):
        s = logits[g] + bias_ref[0, g]
        s = jnp.where(valid, s, NEG_INF)
        e = jnp.where(valid, jnp.exp2(s - jnp.max(s, axis=-1, keepdims=True)), 0.0)
        den = jnp.sum(e, axis=-1, keepdims=True)
        p = e / jnp.where(den > 0, den, 1.0)
        outs.append(_dot(p.astype(BF16), vc))
        p_sum = p_sum + p
    o_ref[...] = jnp.concatenate([jnp.where(lane < HEAD_DIM, outs[0], outs[1]),
                                  jnp.where(lane < HEAD_DIM, outs[2], outs[3])],
                                 axis=1).astype(BF16)

    imp = _dot_nt(ovl_ref[...], p_sum, precision=lax.Precision.HIGHEST)
    jj = lax.broadcasted_iota(jnp.int32, (nsel, tq), 0)
    cur = (lax.broadcasted_iota(jnp.int32, (nsel, tq), 1) + qi * tq) >> SEL_SHIFT
    forced = (jj == 0) | (jj == cur) | (jj == cur - 1)
    score = jnp.where(forced, SEL_FORCE, jnp.where(jj <= cur, imp, -SEL_FORCE))
    sub = SUBLANES
    groups = [score[g * sub:(g + 1) * sub, :] for g in range(nsel // sub)]
    ranks = [jnp.zeros((sub, tq), F32) for _ in groups]
    jg = lax.broadcasted_iota(jnp.int32, (sub, tq), 0)
    for i in range(nsel):
        si = score[i:i + 1, :]
        for g, sg in enumerate(groups):
            if g * sub > i:
                ahead = jnp.where(si >= sg, 1.0, 0.0)
            elif (g + 1) * sub <= i:
                ahead = jnp.where(si > sg, 1.0, 0.0)
            else:
                ahead = jnp.where(jg + g * sub > i, jnp.where(si >= sg, 1.0, 0.0),
                                  jnp.where(si > sg, 1.0, 0.0))
            ranks[g] = ranks[g] + ahead
    rank = jnp.concatenate(ranks, axis=0)
    dropped = jnp.where(rank < min(SEL_TOPK, nsel), 0.0, 1.0)
    pieces = [jnp.zeros((HEAD_DIM, tq), F32), dropped]
    if nsel < LANE - HEAD_DIM:
        pieces.append(jnp.zeros((LANE - HEAD_DIM - nsel, tq), F32))
    sel_ref[0, 0] = jnp.concatenate(pieces, axis=0).T.astype(BF16)


def _cmp_attn(z, kc4, vc, bias_c, ovl_t, *, tq, seq):
    b, hkv, _, ncp = kc4.shape
    m = z.shape[0]
    nq = seq // tq
    nsel = seq // SEL_LEN
    kern = functools.partial(_cmp_attn_kernel, tq=tq, ncp=ncp, nsel=nsel)
    qblk = ZB_NQ * LANE // GROUP_WIDTH
    return pl.pallas_call(
        kern,
        grid=(hkv, nq, b),
        in_specs=[
            pl.BlockSpec((tq, GROUP_WIDTH), lambda h, i, bb: (bb * nq + i, qblk + h)),
            pl.BlockSpec((1, 1, GROUP_WIDTH, ncp), lambda h, i, bb: (bb, h, 0, 0)),
            pl.BlockSpec((1, 1, ncp, LANE), lambda h, i, bb: (bb, h, 0, 0)),
            pl.BlockSpec((1, GROUP, tq, ncp), lambda h, i, bb: (h, 0, i, 0)),
            pl.BlockSpec((nsel, ncp), lambda h, i, bb: (0, 0)),
        ],
        out_specs=[
            pl.BlockSpec((tq, GROUP_WIDTH), lambda h, i, bb: (bb * nq + i, h)),
            pl.BlockSpec((1, 1, tq, LANE), lambda h, i, bb: (bb, h, i, 0)),
        ],
        out_shape=[
            jax.ShapeDtypeStruct((m, BRANCH_WIDTH), BF16),
            jax.ShapeDtypeStruct((b, hkv, seq, LANE), BF16),
        ],
        compiler_params=_cparams(("parallel", "parallel", "parallel")),
        name="cmp_attn",
    )(z, kc4, vc, bias_c, ovl_t)


def _band_attn_kernel(*refs, tq, w, back, n_off, has_sink):
    if has_sink:
        q_ref, kv_ref, bias_ref, sink_ref, o_ref, kt_sc, v_sc = refs
    else:
        q_ref, kv_ref, bias_ref, o_ref, kt_sc, v_sc = refs
    qi = pl.program_id(2)

    @pl.when(qi == 0)
    def _():
        _split_kv(kv_ref, kt_sc, v_sc)

    off = jnp.minimum(qi * tq, back)
    start = pl.multiple_of(qi * tq - off, LANE)
    q = _stack_heads(q_ref[...], jnp.zeros((tq, LANE), F32))
    logits = _group_logits(q, kt_sc[:, pl.ds(start, w)], tq)
    v1 = v_sc[pl.ds(start, w), :]
    tile = jnp.minimum(qi, n_off - 1)
    sum_lane = lax.broadcasted_iota(jnp.int32, (tq, LANE), 1) >= HEAD_DIM
    accs = []
    for h in range(GROUP):
        s = logits[h] + bias_ref[0, tile, h]
        m = jnp.broadcast_to(jnp.max(s, axis=-1, keepdims=True), (tq, LANE))
        if has_sink:
            sink = sink_ref[0, h]
            m = jnp.maximum(m, sink)
        p = jnp.exp2(s - jnp.concatenate([m] * (w // LANE), axis=1))
        acc = _dot(p.astype(BF16), v1)
        if has_sink:
            acc = acc + jnp.where(sum_lane, jnp.exp2(sink - m), 0.0)
        accs.append(acc)
    o_ref[...] = _finish_group(accs).astype(BF16)


def _sel_attn_kernel(q_ref, kv_ref, drop_ref, rows_ref, bias_ref, o_ref, kt_sc, v_sc, m_sc, acc_sc,
                     *, tq, tk, n_bias):
    qi = pl.program_id(2)

    @pl.when(qi == 0)
    def _():
        _split_kv(kv_ref, kt_sc, v_sc, rows_ref[...])

    q = _stack_heads(q_ref[...], drop_ref[0, 0].astype(F32))
    hi = ((qi + 1) * tq + tk - 1) // tk
    m_sc[...] = jnp.full(m_sc.shape, NEG_INF, F32)
    acc_sc[...] = jnp.zeros(acc_sc.shape, F32)

    def step(kj):
        start = pl.multiple_of(kj * tk, tk)
        logits = _group_logits(q, kt_sc[:, pl.ds(start, tk)], tq)
        v1 = v_sc[pl.ds(start, tk), :]
        u = jnp.minimum((qi * tq - kj * tk) // tq, n_bias - 1)
        for h in range(GROUP):
            s = logits[h] + bias_ref[0, u, h]
            _online_update(s, v1, m_sc, acc_sc, h)

    odd = hi % 2

    @pl.when(odd == 1)
    def _():
        step(hi - 1)

    def body(i, carry):
        kj = hi - odd - 1 - 2 * i
        step(kj)
        step(kj - 1)
        return carry

    lax.fori_loop(0, hi // 2, body, 0)
    o_ref[...] = _finish_group([acc_sc[h] for h in range(GROUP)]).astype(BF16)


def _gqa_specs(z, q_block, kv_block, tq, seq):
    nq = seq // tq
    qblk = q_block * LANE // GROUP_WIDTH
    return [
        pl.BlockSpec((tq, GROUP_WIDTH), lambda h, bb, i: (bb * nq + i, qblk + h)),
        pl.BlockSpec((seq, LANE), lambda h, bb, i: (bb, kv_block + h)),
    ]


def _resident(a):
    return pl.BlockSpec((1,) + a.shape[1:], lambda h, bb, i: (h,) + (0,) * (a.ndim - 1),
                        pipeline_mode=pl.Buffered(1))


def _gqa_out(m, tq, seq):
    nq = seq // tq
    return (pl.BlockSpec((tq, GROUP_WIDTH), lambda h, bb, i: (bb * nq + i, h)),
            jax.ShapeDtypeStruct((m, BRANCH_WIDTH), BF16))


def _kv_scratch(seq, key_rows):
    return [pltpu.VMEM((key_rows, seq), BF16), pltpu.VMEM((seq, LANE), BF16)]


def _band_attention(z, bias, *, q_block, kv_block, tq, seq, back, name, sinks=None):
    m = z.shape[0]
    b = m // seq
    n_off, w = bias.shape[1], bias.shape[4]
    assert w == back + tq and w <= seq and back % LANE == 0
    kern = functools.partial(_band_attn_kernel, tq=tq, w=w, back=back, n_off=n_off,
                             has_sink=sinks is not None)
    in_specs = _gqa_specs(z, q_block, kv_block, tq, seq) + [_resident(bias)]
    args = [z, z, bias]
    if sinks is not None:
        in_specs.append(pl.BlockSpec((1, GROUP, 1, 1), lambda h, bb, i: (h, 0, 0, 0)))
        args.append(sinks)
    out_spec, out_shape = _gqa_out(m, tq, seq)
    return pl.pallas_call(
        kern,
        grid=(N_KV, b, seq // tq),
        in_specs=in_specs,
        out_specs=out_spec,
        out_shape=out_shape,
        scratch_shapes=_kv_scratch(seq, LANE),
        compiler_params=_cparams(("parallel", "parallel", "arbitrary")),
        name=name,
    )(*args)


def _sel_attention(z, dropped, sel_rows, bias, *, tq, tk, seq):
    m = z.shape[0]
    b = m // seq
    kern = functools.partial(_sel_attn_kernel, tq=tq, tk=tk, n_bias=bias.shape[1])
    in_specs = _gqa_specs(z, ZB_NQ, ZB_SEL_KV, tq, seq) + [
        pl.BlockSpec((1, 1, tq, LANE), lambda h, bb, i: (bb, h, i, 0)),
        pl.BlockSpec((HEAD_DIM, seq), lambda h, bb, i: (0, 0), pipeline_mode=pl.Buffered(1)),
        _resident(bias),
    ]
    out_spec, out_shape = _gqa_out(m, tq, seq)
    return pl.pallas_call(
        kern,
        grid=(N_KV, b, seq // tq),
        in_specs=in_specs,
        out_specs=out_spec,
        out_shape=out_shape,
        scratch_shapes=_kv_scratch(seq, LANE) + [pltpu.VMEM((GROUP, tq, LANE), F32),
                                                 pltpu.VMEM((GROUP, tq, LANE), F32)],
        compiler_params=_cparams(("parallel", "parallel", "arbitrary")),
        name="attn_sel",
    )(z, z, dropped, sel_rows, bias)


def _rope_lanes(x, cos_t, sin_a, sin_b):
    half = MLA_ROPE // 2
    return x * cos_t + pltpu.roll(x, LANE - half, 1) * sin_a + pltpu.roll(x, half, 1) * sin_b


def _mla_proj_kernel(mq_ref, mkv_ref, kr_ref, cos_ref, sina_ref, sinb_ref, qn_ref, wq_ref,
                     kvn_ref, wkt_ref, wv_ref, oq_ref, okt_ref, ov_ref):
    cos_t, sin_a, sin_b = cos_ref[...], sina_ref[...], sinb_ref[...]
    ql = _rms(mq_ref[...].astype(F32), qn_ref[...]).astype(BF16)
    q = _dot(ql, wq_ref[...]) * (MLA_QK ** -0.5 * LOG2E)
    kvl = _rms(mkv_ref[...].astype(F32), kvn_ref[...]).astype(BF16)
    kt = _dot_nt(wkt_ref[...], kvl)
    v = _dot(kvl, wv_ref[...])
    kr_t = _rope_lanes(kr_ref[...].astype(F32), cos_t, sin_a, sin_b).T
    lane = lax.broadcasted_iota(jnp.int32, (v.shape[0], LANE), 1)
    for h in range(N_HEADS):
        blk = slice(h * LANE, (h + 1) * LANE)
        oq_ref[:, blk] = _rope_lanes(q[:, blk], cos_t, sin_a, sin_b).astype(BF16)
        okt_ref[0, blk, :] = (kt[blk, :] + kr_t).astype(BF16)
        ov_ref[:, blk] = jnp.where(lane < MLA_V, v[:, blk], 1.0).astype(BF16)


def _mla_proj(z, cos_t, sin_a, sin_b, q_norm, w_qb, kv_norm, w_kt, w_v, *, tm, seq):
    m = z.shape[0]
    nt = seq // tm
    width = N_HEADS * LANE

    def full(a):
        return pl.BlockSpec(a.shape, lambda i: (0,) * a.ndim)

    tab = pl.BlockSpec((tm, LANE), lambda i: (i % nt, 0))
    out = jax.ShapeDtypeStruct((m, width), BF16)
    oblk = pl.BlockSpec((tm, width), lambda i: (i, 0))
    return pl.pallas_call(
        _mla_proj_kernel,
        grid=(m // tm,),
        in_specs=[
            pl.BlockSpec((tm, MLA_Q_RANK), lambda i: (i, ZB_MQ * LANE // MLA_Q_RANK)),
            pl.BlockSpec((tm, MLA_KV_RANK), lambda i: (i, ZB_MKV * LANE // MLA_KV_RANK)),
            pl.BlockSpec((tm, LANE), lambda i: (i, ZB_KR)),
            tab, tab, tab,
            full(q_norm), full(w_qb), full(kv_norm), full(w_kt), full(w_v),
        ],
        out_specs=[oblk, pl.BlockSpec((1, width, tm), lambda i: (i // nt, 0, i % nt)), oblk],
        out_shape=[out, jax.ShapeDtypeStruct((m // seq, width, seq), BF16), out],
        compiler_params=_cparams(("parallel",)),
        name="mla_proj",
    )(z, z, z, cos_t, sin_a, sin_b, q_norm, w_qb, kv_norm, w_kt, w_v)


def _mla_attn_kernel(q_ref, k_ref, v_ref, o_ref, m_sc, acc_sc, *, tq):
    qi = pl.program_id(2)
    n_chain = m_sc.shape[0]
    qs = [q_ref[:, c * LANE:(c + 1) * LANE] for c in range(n_chain)]
    m_sc[...] = jnp.full(m_sc.shape, NEG_INF, F32)
    acc_sc[...] = jnp.zeros(acc_sc.shape, F32)

    def step(tile, width, diagonal):
        start = pl.multiple_of(tile * tq, tq)
        for c in range(n_chain):
            blk = slice(c * LANE, (c + 1) * LANE)
            s = _dot(qs[c], k_ref[0, blk, pl.ds(start, width * tq)])
            if diagonal:
                row = lax.broadcasted_iota(jnp.int32, s.shape, 0)
                col = lax.broadcasted_iota(jnp.int32, s.shape, 1)
                s = jnp.where(col <= row, s, NEG_INF)
            _online_update(s, v_ref[pl.ds(start, width * tq), blk], m_sc, acc_sc, c)

    step(qi, 1, True)
    odd = qi % 2

    @pl.when(odd == 1)
    def _():
        step(qi - 1, 1, False)

    def body(i, carry):
        step(qi - odd - 2 * (i + 1), 2, False)
        return carry

    lax.fori_loop(0, qi // 2, body, 0)
    for c in range(n_chain):
        acc = acc_sc[c]
        lane = lax.broadcasted_iota(jnp.int32, acc.shape, 1)
        den = jnp.where(lane < MLA_V, pltpu.roll(acc, MLA_V, 1), acc)
        o_ref[:, c * LANE:(c + 1) * LANE] = (acc / den).astype(BF16)


def _mla_attention(q, k, v1, *, tq, seq):
    m = q.shape[0]
    b = m // seq
    nq = seq // tq
    width = MLA_CHAINS * LANE
    slab = pl.BlockSpec((seq, width), lambda h, bb, i: (bb, h))
    slab_t = pl.BlockSpec((1, width, seq), lambda h, bb, i: (bb, h, 0))
    tile = pl.BlockSpec((tq, width), lambda h, bb, i: (bb * nq + i, h))
    return pl.pallas_call(
        functools.partial(_mla_attn_kernel, tq=tq),
        grid=(N_HEADS // MLA_CHAINS, b, nq),
        in_specs=[tile, slab_t, slab],
        out_specs=tile,
        out_shape=jax.ShapeDtypeStruct(q.shape, BF16),
        scratch_shapes=[pltpu.VMEM((MLA_CHAINS, tq, LANE), F32),
                        pltpu.VMEM((MLA_CHAINS, tq, LANE), F32)],
        compiler_params=_cparams(("parallel", "parallel", "arbitrary")),
        name="attn_mla",
    )(q, k, v1)


def _merge_kernel(ocmp_ref, osel_ref, owin_ref, ob_ref, oc_ref, ng_ref, mg_ref, x_ref,
                  gexp_ref, wab_ref, wc_ref, wo_ref, gpost_ref, o_ref):
    ng = jax.nn.sigmoid(ng_ref[...].astype(F32))
    ng_hi = ng.astype(BF16)
    ng_lo = (ng - ng_hi.astype(F32)).astype(BF16)
    gates = _dot(jnp.concatenate([ng_hi, ng_lo], axis=1), gexp_ref[...])
    o_a = jnp.zeros(ocmp_ref.shape, F32)
    for n, ref in enumerate((ocmp_ref, osel_ref, owin_ref)):
        gate = gates[:, n * BRANCH_WIDTH:(n + 1) * BRANCH_WIDTH]
        o_a = o_a + gate * ref[...].astype(F32)
    ys = (_dot(o_a.astype(BF16), wab_ref[0]), _dot(ob_ref[...], wab_ref[1]),
          _dot(oc_ref[...], wc_ref[...]))
    mixed = jnp.zeros(x_ref.shape, F32)
    for n, y in enumerate(ys):
        gate = jax.nn.sigmoid(mg_ref[:, n * D_MODEL:(n + 1) * D_MODEL].astype(F32))
        mixed = mixed + gate * y
    out = _dot(mixed.astype(BF16), wo_ref[...])
    o_ref[...] = x_ref[...] + _rms(out, gpost_ref[...])


def _merge(o_cmp, o_sel, o_win, o_b, o_c, z, x, gexp, w_ab, w_c, w_out, g_post, *, tm):
    m, d = x.shape
    br = pl.BlockSpec((tm, o_cmp.shape[1]), lambda i: (i, 0))

    def full(a):
        return pl.BlockSpec(a.shape, lambda i: (0,) * a.ndim)

    return pl.pallas_call(
        _merge_kernel,
        grid=(m // tm,),
        in_specs=[
            br, br, br, br,
            pl.BlockSpec((tm, o_c.shape[1]), lambda i: (i, 0)),
            pl.BlockSpec((tm, LANE), lambda i: (i, ZB_NGATE)),
            pl.BlockSpec((tm, 3 * d), lambda i: (i, ZB_MGATE * LANE // (3 * d))),
            pl.BlockSpec((tm, d), lambda i: (i, 0)),
            full(gexp), full(w_ab), full(w_c), full(w_out), full(g_post),
        ],
        out_specs=pl.BlockSpec((tm, d), lambda i: (i, 0)),
        out_shape=jax.ShapeDtypeStruct((m, d), F32),
        compiler_params=_cparams(("parallel",)),
        name="merge",
    )(o_cmp, o_sel, o_win, o_b, o_c, z, z, x, gexp, w_ab, w_c, w_out, g_post)


def _t5_bucket(dist):
    max_exact = NUM_BUCKETS // 2
    d = jnp.maximum(dist, 0)
    df = jnp.maximum(d, 1).astype(F32)
    large = max_exact + (jnp.log(df / max_exact) / math.log(MAX_DISTANCE / max_exact)
                         * (NUM_BUCKETS - max_exact)).astype(jnp.int32)
    large = jnp.minimum(large, NUM_BUCKETS - 1)
    return jnp.where(d < max_exact, d, large)


def _toeplitz(fn, offs, tq, tk):
    ln = tq + tk - 1
    i = jnp.arange(ln + 1)
    shift = jnp.where(i < tk, -i, ln + 1 - i)
    ext = jnp.moveaxis(fn(jnp.asarray(offs)[:, None] + shift[None, :]), -1, 0)
    flat = jnp.tile(ext, (1, 1, tq))[:, :, :tq * ln]
    return flat.reshape(ext.shape[0], len(offs), tq, ln)[:, :, :, :tk]


def _bias_fn(table, window):
    def fn(dist):
        ok = dist >= 0
        if window is not None:
            ok = ok & (dist < window)
        return jnp.where(ok[..., None], table[_t5_bucket(dist)].astype(F32), NEG_INF)
    return fn


def _head_tiles(t):
    return t.reshape(N_KV, GROUP, *t.shape[1:]).transpose(0, 2, 1, 3, 4)


def _in_proj_weight(w_in):
    sizes = (512, 128, 128, 128, 128, 128, 128, 24, 512, 128, 128, MLA_Q_RANK, MLA_KV_RANK,
             MLA_ROPE, 3 * D_MODEL)
    offs = np.concatenate([[0], np.cumsum(sizes)])
    seg = [w_in[:, offs[i]:offs[i + 1]] for i in range(len(sizes))]
    (nq, nkc, nvc, nks, nvs, nkw, nvw, ngate, sq, sk, sv, mq, mkv, mkr, mgate) = seg
    d = w_in.shape[0]
    hd = HEAD_DIM

    def pad(a, left=0):
        return jnp.pad(a, ((0, 0), (left, LANE - left - a.shape[1])))

    def kv_blocks(k, v):
        return [jnp.concatenate([k[:, h * hd:(h + 1) * hd], v[:, h * hd:(h + 1) * hd]], axis=1)
                for h in range(N_KV)]

    cols = ([mq, pad(ngate), mkv, pad(mkr, MLA_NOPE), jnp.zeros((d, LANE), w_in.dtype),
             nq * QK_SCALE, nkc, nvc] + kv_blocks(nks, nvs) + kv_blocks(nkw, nvw)
            + [sq * QK_SCALE] + kv_blocks(sk, sv) + [mgate])
    w = jnp.concatenate(cols, axis=1)
    assert w.shape == (d, Z_WIDTH)
    return w.astype(BF16)


def _mla_weights(w_qb, w_kvb, w_c):
    def blocks(a):
        return jnp.pad(a, ((0, 0), (0, 0), (0, LANE - a.shape[2]))).reshape(a.shape[0], -1)

    wq = blocks(w_qb.reshape(MLA_Q_RANK, N_HEADS, MLA_QK))
    wkv = w_kvb.reshape(MLA_KV_RANK, N_HEADS, MLA_NOPE + MLA_V)
    wkt = blocks(wkv[:, :, :MLA_NOPE]).T
    wv = blocks(wkv[:, :, MLA_NOPE:])
    wc = jnp.pad(w_c.reshape(N_HEADS, MLA_V, -1), ((0, 0), (0, LANE - MLA_V), (0, 0)))
    return (wq.astype(BF16), wkt.astype(BF16), wv.astype(BF16),
            wc.reshape(N_HEADS * LANE, -1).astype(BF16))


def _gate_expand():
    e = np.zeros((2 * LANE, 3 * BRANCH_WIDTH), np.float32)
    for n in range(3):
        for h in range(N_HEADS):
            cols = slice(n * BRANCH_WIDTH + h * HEAD_DIM, n * BRANCH_WIDTH + (h + 1) * HEAD_DIM)
            e[h * 3 + n, cols] = 1.0
            e[LANE + h * 3 + n, cols] = 1.0
    return jnp.asarray(e, BF16)


def _rope_tables(pos):
    half = MLA_ROPE // 2
    inv_freq = ROPE_BASE ** (-jnp.arange(half, dtype=F32) / half)
    ang = pos.astype(F32)[:, None] * inv_freq[None, :]
    cos, sin = jnp.cos(ang), jnp.sin(ang)
    n = pos.shape[0]
    ones = jnp.ones((n, MLA_NOPE), F32)
    z16 = jnp.zeros((n, half), F32)
    tail = jnp.zeros((n, LANE - MLA_QK), F32)
    cos_t = jnp.concatenate([ones, cos, cos, tail], axis=1)
    sin_a = jnp.concatenate([0 * ones, -sin, z16, tail], axis=1)
    sin_b = jnp.concatenate([0 * ones, z16, sin, tail], axis=1)
    return cos_t, sin_a, sin_b


def kernel(x, rel_bias_table, ffn1_norm_pre, ffn1_w_gu, ffn1_w_down, ffn1_norm_post,
           mix_norm_pre, w_in, nsa_pe_k, nsa_w1_k, nsa_w2_k, nsa_pe_v, nsa_w1_v, nsa_w2_v,
           swa_sinks, mla_q_norm, mla_w_qb, mla_kv_norm, mla_w_kvb, w_branch, w_out,
           mix_norm_post, ffn2_norm_pre, ffn2_w_gu, ffn2_w_down, ffn2_norm_post):
    b, s, d = x.shape
    depth = w_in.shape[0]
    m = b * s
    tq = 256
    tq_band = min(256, s // 2)
    tq_sel = min(512, s)
    tk_sel = min(512, s)
    tq_mla = min(512, s)
    ncp = s // CMP_STRIDE
    nsel = s // SEL_LEN
    tm = min(512, m)
    tm_big = min(1024, s)

    pos = jnp.arange(s, dtype=jnp.int32)
    tab_a = rel_bias_table[:, :N_HEADS] * LOG2E
    tab_b = rel_bias_table[:, N_HEADS:] * LOG2E
    n_far = -(-(LAST_BUCKET_DIST + tk_sel - 1) // tq_sel)
    n_sel_tiles = min(n_far + 1, s // tq_sel)
    bias_sel = _head_tiles(_toeplitz(_bias_fn(tab_a, None),
                                     [u * tq_sel for u in range(n_sel_tiles)], tq_sel, tk_sel))
    tb = tq_band
    win_offs = sorted({min(i * tb, NSA_WINDOW) for i in range(-(-NSA_WINDOW // tb) + 1)})
    bias_win = _head_tiles(_toeplitz(_bias_fn(tab_a, NSA_WINDOW), win_offs, tb, NSA_WINDOW + tb))
    swa_offs = sorted({min(i * tb, SWA_WINDOW) for i in range(-(-SWA_WINDOW // tb) + 1)})
    bias_swa = _head_tiles(_toeplitz(_bias_fn(tab_b, SWA_WINDOW), swa_offs, tb, SWA_WINDOW + tb))

    def cmp_bias_fn(e):
        r = jnp.arange(CMP_STRIDE)
        dist = e[..., None] * CMP_STRIDE + r - (CMP_LEN - 1)
        return tab_a[_t5_bucket(dist)].astype(F32).reshape(*e.shape, CMP_STRIDE * N_HEADS)

    bias_c = _toeplitz(cmp_bias_fn, [0], ncp, ncp).reshape(CMP_STRIDE, N_HEADS, ncp, ncp)
    bias_c = bias_c.transpose(1, 2, 0, 3).reshape(N_KV, GROUP, s, ncp)
    ci = jnp.arange(ncp)[None, :] * CMP_STRIDE
    sj = jnp.arange(nsel)[:, None] * SEL_LEN
    ovl_t = (jnp.maximum(jnp.minimum(ci + CMP_LEN, sj + SEL_LEN) - jnp.maximum(ci, sj), 0)
             .astype(F32) / CMP_LEN)
    cos_t, sin_a, sin_b = _rope_tables(pos)
    gexp = _gate_expand()
    assert nsel <= HEAD_DIM
    sel_rows = jnp.where(jnp.arange(HEAD_DIM)[:, None] == (pos[None, :] >> SEL_SHIFT),
                         NEG_INF, 0.0).astype(BF16)
    chunk = CMP_STRIDE * HEAD_DIM

    xf = x.reshape(m, d)
    for l in range(depth):
        xf = _ffn(xf, ffn1_norm_pre[l][None], ffn1_w_gu[l].astype(BF16),
                  ffn1_w_down[l].astype(BF16), ffn1_norm_post[l][None], tm=tm)

        z = _inproj(xf, mix_norm_pre[l][None], _in_proj_weight(w_in[l]), tm=tm_big, tn=Z_WIDTH // 4)

        def chunks(blk):
            a = z[:, blk * LANE:(blk + 1) * LANE].reshape(b, s, N_KV, HEAD_DIM)
            return a.transpose(0, 2, 1, 3).reshape(b, N_KV, ncp, chunk)

        kc, vc = _cmp_mlp(chunks(ZB_CMP_K), chunks(ZB_CMP_V),
                          nsa_pe_k[l].reshape(2, chunk), nsa_w1_k[l].astype(BF16),
                          nsa_w2_k[l].astype(BF16),
                          nsa_pe_v[l].reshape(2, chunk), nsa_w1_v[l].astype(BF16),
                          nsa_w2_v[l].astype(BF16))
        o_cmp, dropped = _cmp_attn(z, jnp.tile(kc.transpose(0, 1, 3, 2), (1, 1, GROUP, 1)),
                                   jnp.tile(vc, (1, 1, 1, 2)),
                                   bias_c, ovl_t, tq=tq, seq=s)
        o_sel = _sel_attention(z, dropped, sel_rows, bias_sel, tq=tq_sel, tk=tk_sel, seq=s)
        o_win = _band_attention(z, bias_win, q_block=ZB_NQ, kv_block=ZB_WIN_KV, tq=tq_band, seq=s,
                                back=NSA_WINDOW, name="attn_win")
        o_b = _band_attention(z, bias_swa, q_block=ZB_SQ, kv_block=ZB_SWA_KV, tq=tq_band, seq=s,
                              back=SWA_WINDOW, name="attn_swa",
                              sinks=(swa_sinks[l] * LOG2E).reshape(N_KV, GROUP, 1, 1))
        wq, wkt, wv, wc = _mla_weights(mla_w_qb[l], mla_w_kvb[l], w_branch[l, 2])
        mq, mk, mv1 = _mla_proj(z, cos_t, sin_a, sin_b, mla_q_norm[l][None], wq,
                                mla_kv_norm[l][None], wkt, wv, tm=tm_big, seq=s)
        o_c = _mla_attention(mq, mk, mv1, tq=tq_mla, seq=s)

        xf = _merge(o_cmp, o_sel, o_win, o_b, o_c, z, xf, gexp, w_branch[l, :2].astype(BF16), wc,
                    w_out[l].astype(BF16), mix_norm_post[l][None], tm=tm)

        xf = _ffn(xf, ffn2_norm_pre[l][None], ffn2_w_gu[l].astype(BF16),
                  ffn2_w_down[l].astype(BF16), ffn2_norm_post[l][None], tm=tm)
    return xf.reshape(b, s, d)
```

```python
import functools
import math

import jax
import jax.numpy as jnp
import numpy as np
from jax import lax
from jax.experimental import pallas as pl
from jax.experimental.pallas import tpu as pltpu

F32 = jnp.float32
BF16 = jnp.bfloat16

D_MODEL = 1024
D_FF = 2816
HEAD_DIM = 64
NORM_EPS = 1e-6
NUM_BUCKETS = 32
MAX_DISTANCE = 1024
LAST_BUCKET_DIST = 1 + math.ceil(
    (NUM_BUCKETS // 2) * (MAX_DISTANCE / (NUM_BUCKETS // 2))
    ** ((NUM_BUCKETS // 2 - 1) / (NUM_BUCKETS - NUM_BUCKETS // 2)))
N_HEADS = 8
N_KV = 2
GROUP = N_HEADS // N_KV
CMP_LEN = 32
CMP_STRIDE = 16
CMP_HIDDEN = 128
SEL_LEN = 64
SEL_SHIFT = 6
SEL_TOPK = 16
NSA_WINDOW = 512
SWA_WINDOW = 128
MLA_Q_RANK = 384
MLA_KV_RANK = 256
MLA_NOPE = 64
MLA_ROPE = 32
MLA_V = 64
MLA_QK = MLA_NOPE + MLA_ROPE
ROPE_BASE = 10000.0
BRANCH_WIDTH = N_HEADS * HEAD_DIM
NEG_INF = -1e30
SEL_FORCE = 1e9
LOG2E = math.log2(math.e)
QK_SCALE = HEAD_DIM ** -0.5 * LOG2E

LANE = 128
SUBLANES = 8
VMEM_LIMIT = 56 * 1024 * 1024
GROUP_WIDTH = GROUP * HEAD_DIM
CMP_PARTS = 4
QK_SPLIT = 2
KV_SPLIT_CHUNK = 512
MLA_CHAINS = 4

ZB_MQ = 0
ZB_NGATE = 3
ZB_MKV = 4
ZB_KR = 6
ZB_NQ = 8
ZB_CMP_K = 12
ZB_CMP_V = 13
ZB_SEL_KV = 14
ZB_WIN_KV = 16
ZB_SQ = 18
ZB_SWA_KV = 22
ZB_MGATE = 24
Z_BLOCKS = 48
Z_WIDTH = Z_BLOCKS * LANE


def _cparams(sem):
    return pltpu.CompilerParams(dimension_semantics=sem, vmem_limit_bytes=VMEM_LIMIT)


def _rms(x, g):
    inv = lax.rsqrt(jnp.mean(x * x, axis=-1, keepdims=True) + NORM_EPS)
    return (x * inv) * g


def _dot(a, b):
    return jnp.dot(a, b, preferred_element_type=F32)


def _dot_nt(a, b, precision=None):
    return lax.dot_general(a, b, (((1,), (1,)), ((), ())), precision=precision,
                           preferred_element_type=F32)


def _ffn_kernel(x_ref, gpre_ref, wg_ref, wu_ref, wd_ref, gpost_ref, o_ref):
    x = x_ref[...]
    h = _rms(x, gpre_ref[...]).astype(BF16)
    g = _dot(h, wg_ref[...])
    u = _dot(h, wu_ref[...])
    act = (g * jax.nn.sigmoid(g)) * u
    y = _dot(act.astype(BF16), wd_ref[...])
    o_ref[...] = x + 0.5 * _rms(y, gpost_ref[...])


def _ffn(x, g_pre, w_gu, w_down, g_post, *, tm):
    m, d = x.shape
    once = pl.Buffered(1)
    return pl.pallas_call(
        _ffn_kernel,
        grid=(m // tm,),
        in_specs=[
            pl.BlockSpec((tm, d), lambda i: (i, 0)),
            pl.BlockSpec((1, d), lambda i: (0, 0)),
            pl.BlockSpec((d, D_FF), lambda i: (0, 0), pipeline_mode=once),
            pl.BlockSpec((d, D_FF), lambda i: (0, 1), pipeline_mode=once),
            pl.BlockSpec((D_FF, d), lambda i: (0, 0), pipeline_mode=once),
            pl.BlockSpec((1, d), lambda i: (0, 0)),
        ],
        out_specs=pl.BlockSpec((tm, d), lambda i: (i, 0)),
        out_shape=jax.ShapeDtypeStruct((m, d), F32),
        compiler_params=_cparams(("parallel",)),
        name="ffn",
    )(x, g_pre, w_gu, w_gu, w_down, g_post)


def _inproj_kernel(x_ref, g_ref, w_ref, o_ref, h_sc):
    @pl.when(pl.program_id(1) == 0)
    def _():
        h_sc[...] = _rms(x_ref[...], g_ref[...]).astype(BF16)

    o_ref[...] = _dot(h_sc[...], w_ref[...]).astype(BF16)


def _inproj(x, g, w, *, tm, tn):
    m, d = x.shape
    n = w.shape[1]
    return pl.pallas_call(
        _inproj_kernel,
        grid=(m // tm, n // tn),
        in_specs=[
            pl.BlockSpec((tm, d), lambda i, j: (i, 0)),
            pl.BlockSpec((1, d), lambda i, j: (0, 0)),
            pl.BlockSpec((d, tn), lambda i, j: (0, j)),
        ],
        out_specs=pl.BlockSpec((tm, tn), lambda i, j: (i, j)),
        out_shape=jax.ShapeDtypeStruct((m, n), BF16),
        scratch_shapes=[pltpu.VMEM((tm, d), BF16)],
        compiler_params=_cparams(("parallel", "arbitrary")),
        name="inproj",
    )(x, g, w)


def _cmp_mlp_kernel(k_ref, v_ref, pek_ref, w1k_ref, w2k_ref, pev_ref, w1v_ref, w2v_ref,
                    kc_ref, vc_ref):
    half = CMP_STRIDE * HEAD_DIM

    def one(x_ref, pe_ref, w1_ref, w2_ref, o_ref):
        x = x_ref[0, 0].astype(F32)
        ncp = x.shape[0]
        xa = (x + pe_ref[0:1, :]).astype(BF16)
        xb = (x + pe_ref[1:2, :]).astype(BF16)
        pa = _dot(xa, w1_ref[0:half, :])
        pb = _dot(xb, w1_ref[half:2 * half, :])
        hid = pa + pltpu.roll(pb, ncp - 1, 0)
        out = _dot(jax.nn.gelu(hid).astype(BF16), w2_ref[...])
        row = lax.broadcasted_iota(jnp.int32, out.shape, 0)
        o_ref[0, 0] = jnp.where(row < ncp - 1, out, 0.0).astype(BF16)

    one(k_ref, pek_ref, w1k_ref, w2k_ref, kc_ref)
    one(v_ref, pev_ref, w1v_ref, w2v_ref, vc_ref)


def _cmp_mlp(k4, v4, pek, w1k, w2k, pev, w1v, w2v):
    b, hkv, ncp, width = k4.shape
    blk = pl.BlockSpec((1, 1, ncp, width), lambda i, j: (i, j, 0, 0))
    oblk = pl.BlockSpec((1, 1, ncp, HEAD_DIM), lambda i, j: (i, j, 0, 0))

    def full(a):
        return pl.BlockSpec(a.shape, lambda i, j: (0,) * a.ndim)

    out = jax.ShapeDtypeStruct((b, hkv, ncp, HEAD_DIM), BF16)
    return pl.pallas_call(
        _cmp_mlp_kernel,
        grid=(b, hkv),
        in_specs=[blk, blk, full(pek), full(w1k), full(w2k), full(pev), full(w1v), full(w2v)],
        out_specs=[oblk, oblk],
        out_shape=[out, out],
        compiler_params=_cparams(("parallel", "parallel")),
        name="cmp_mlp",
    )(k4, v4, pek, w1k, w2k, pev, w1v, w2v)


def _stack_heads(q_all, upper):
    q32 = q_all.astype(F32)
    lane = lax.broadcasted_iota(jnp.int32, upper.shape, 1)
    rows = []
    for pair in range(GROUP // 2):
        half = q32[:, pair * LANE:(pair + 1) * LANE]
        rows.append(jnp.where(lane < HEAD_DIM, half, upper))
        rows.append(jnp.where(lane < HEAD_DIM, pltpu.roll(half, HEAD_DIM, 1), upper))
    return jnp.concatenate(rows, axis=0).astype(BF16)


def _split_kv(kv_ref, kt_sc, v_sc, key_rows=None):
    seq = kv_ref.shape[0]
    chunk = min(KV_SPLIT_CHUNK, seq)
    copies = kt_sc.shape[0] // HEAD_DIM - (key_rows is not None)
    lane = lax.broadcasted_iota(jnp.int32, (chunk, LANE), 1)
    for c in range(seq // chunk):
        rows = slice(c * chunk, (c + 1) * chunk)
        x = kv_ref[rows, :].astype(F32)
        kt = x.T[0:HEAD_DIM, :].astype(BF16)
        for r in range(copies):
            kt_sc[r * HEAD_DIM:(r + 1) * HEAD_DIM, rows] = kt
        v_sc[rows, :] = jnp.where(lane < HEAD_DIM, pltpu.roll(x, HEAD_DIM, 1), 1.0).astype(BF16)
    if key_rows is not None:
        kt_sc[copies * HEAD_DIM:(copies + 1) * HEAD_DIM, :] = key_rows


def _group_logits(q_stack, kt, tq):
    per = GROUP // QK_SPLIT
    parts = [_dot(q_stack[p * per * tq:(p + 1) * per * tq], kt) for p in range(QK_SPLIT)]
    return [parts[h // per][(h % per) * tq:(h % per + 1) * tq] for h in range(GROUP)]


def _finish_pair_rolled(acc_even, acc_odd):
    lane = lax.broadcasted_iota(jnp.int32, acc_even.shape, 1)
    even = acc_even / pltpu.roll(acc_even, HEAD_DIM, 1)
    odd = pltpu.roll(acc_odd, HEAD_DIM, 1) / acc_odd
    return jnp.where(lane < HEAD_DIM, even, odd)


def _finish_group(accs):
    return jnp.concatenate([_finish_pair_rolled(accs[0], accs[1]),
                            _finish_pair_rolled(accs[2], accs[3])], axis=1)


def _online_update(s, v, m_ref, acc_ref, idx):
    m_old = m_ref[idx]
    m_new = jnp.maximum(m_old, jnp.max(s, axis=-1, keepdims=True))
    alpha = jnp.exp2(m_old - m_new)
    p = jnp.exp2(s - jnp.concatenate([m_new] * (s.shape[1] // LANE), axis=1))
    acc = acc_ref[idx]
    acc_ref[idx] = (jnp.concatenate([alpha] * (acc.shape[1] // LANE), axis=1) * acc
                    + _dot(p.astype(BF16), v))
    m_ref[idx] = m_new


def _cmp_attn_kernel(q_ref, kc_ref, vc_ref, bias_ref, ovl_ref, o_ref, sel_ref, *, tq, ncp, nsel, nq):
    qi = pl.program_id(1)
    parts = CMP_PARTS if nq % CMP_PARTS == 0 and nsel % (CMP_PARTS * SUBLANES) == 0 else 1
    for k in range(parts):
        ncols = min(ncp, -(-((k + 1) * ncp // parts) // LANE) * LANE)
        nblk = (k + 1) * nsel // parts
        pl.when(qi // (nq // parts) == k)(functools.partial(
            _cmp_attn_body, q_ref, kc_ref, vc_ref, bias_ref, ovl_ref, o_ref, sel_ref,
            tq=tq, ncols=ncols, nblk=nblk))


def _cmp_attn_body(q_ref, kc_ref, vc_ref, bias_ref, ovl_ref, o_ref, sel_ref, *, tq, ncols, nblk):
    qi = pl.program_id(1)
    vc = vc_ref[0, 0, 0:ncols, :]
    lane = lax.broadcasted_iota(jnp.int32, (tq, LANE), 1)
    row = lax.broadcasted_iota(jnp.int32, (tq, ncols), 0) + qi * tq
    col = lax.broadcasted_iota(jnp.int32, (tq, ncols), 1)
    valid = (col * CMP_STRIDE + (CMP_LEN - 1)) <= row
    p_sum = jnp.zeros((tq, ncols), F32)
    outs = []
    q = _stack_heads(q_ref[...], jnp.zeros((tq, LANE), F32))
    logits = _group_logits(q, kc_ref[0, 0, :, 0:ncols], tq)
    for g in range(GROUP):
        s = logits[g] + bias_ref[0, g, :, 0:ncols]
        s = jnp.where(valid, s, NEG_INF)
        e = jnp.where(valid, jnp.exp2(s - jnp.max(s, axis=-1, keepdims=True)), 0.0)
        den = jnp.sum(e, axis=-1, keepdims=True)
        p = e / jnp.where(den > 0, den, 1.0)
        outs.append(_dot(p.astype(BF16), vc))
        p_sum = p_sum + p
    o_ref[...] = jnp.concatenate([jnp.where(lane < HEAD_DIM, outs[0], outs[1]),
                                  jnp.where(lane < HEAD_DIM, outs[2], outs[3])],
                                 axis=1).astype(BF16)

    imp = _dot_nt(ovl_ref[0:nblk, 0:ncols], p_sum, precision=lax.Precision.HIGHEST)
    jj = lax.broadcasted_iota(jnp.int32, (nblk, tq), 0)
    cur = (lax.broadcasted_iota(jnp.int32, (nblk, tq), 1) + qi * tq) >> SEL_SHIFT
    forced = (jj == 0) | (jj == cur) | (jj == cur - 1)
    score = jnp.where(forced, SEL_FORCE, jnp.where(jj <= cur, imp, -SEL_FORCE))
    sub = SUBLANES
    groups = [score[g * sub:(g + 1) * sub, :] for g in range(nblk // sub)]
    ranks = [jnp.zeros((sub, tq), F32) for _ in groups]
    jg = lax.broadcasted_iota(jnp.int32, (sub, tq), 0)
    for i in range(nblk):
        si = score[i:i + 1, :]
        for g, sg in enumerate(groups):
            if g * sub > i:
                ahead = jnp.where(si >= sg, 1.0, 0.0)
            elif (g + 1) * sub <= i:
                ahead = jnp.where(si > sg, 1.0, 0.0)
            else:
                ahead = jnp.where(jg + g * sub > i, jnp.where(si >= sg, 1.0, 0.0),
                                  jnp.where(si > sg, 1.0, 0.0))
            ranks[g] = ranks[g] + ahead
    rank = jnp.concatenate(ranks, axis=0)
    dropped = jnp.where(rank < SEL_TOPK, 0.0, 1.0)
    pieces = [jnp.zeros((HEAD_DIM, tq), F32), dropped]
    if nblk < LANE - HEAD_DIM:
        pieces.append(jnp.zeros((LANE - HEAD_DIM - nblk, tq), F32))
    sel_ref[0, 0] = jnp.concatenate(pieces, axis=0).T.astype(BF16)


def _cmp_attn(z, kc4, vc, bias_c, ovl_t, *, tq, seq):
    b, hkv, _, ncp = kc4.shape
    m = z.shape[0]
    nq = seq // tq
    nsel = seq // SEL_LEN
    kern = functools.partial(_cmp_attn_kernel, tq=tq, ncp=ncp, nsel=nsel, nq=nq)
    qblk = ZB_NQ * LANE // GROUP_WIDTH
    return pl.pallas_call(
        kern,
        grid=(hkv, nq, b),
        in_specs=[
            pl.BlockSpec((tq, GROUP_WIDTH), lambda h, i, bb: (bb * nq + i, qblk + h)),
            pl.BlockSpec((1, 1, LANE, ncp), lambda h, i, bb: (bb, h, 0, 0)),
            pl.BlockSpec((1, 1, ncp, LANE), lambda h, i, bb: (bb, h, 0, 0)),
            pl.BlockSpec((1, GROUP, tq, ncp), lambda h, i, bb: (h, 0, i, 0)),
            pl.BlockSpec((nsel, ncp), lambda h, i, bb: (0, 0)),
        ],
        out_specs=[
            pl.BlockSpec((tq, GROUP_WIDTH), lambda h, i, bb: (bb * nq + i, h)),
            pl.BlockSpec((1, 1, tq, LANE), lambda h, i, bb: (bb, h, i, 0)),
        ],
        out_shape=[
            jax.ShapeDtypeStruct((m, BRANCH_WIDTH), BF16),
            jax.ShapeDtypeStruct((b, hkv, seq, LANE), BF16),
        ],
        compiler_params=_cparams(("parallel", "parallel", "parallel")),
        name="cmp_attn",
    )(z, kc4, vc, bias_c, ovl_t)


def _band_attn_kernel(*refs, tq, w, back, n_off, has_sink):
    if has_sink:
        q_ref, kv_ref, bias_ref, sink_ref, o_ref, kt_sc, v_sc = refs
    else:
        q_ref, kv_ref, bias_ref, o_ref, kt_sc, v_sc = refs
    qi = pl.program_id(2)

    @pl.when(qi == 0)
    def _():
        _split_kv(kv_ref, kt_sc, v_sc)

    off = jnp.minimum(qi * tq, back)
    start = pl.multiple_of(qi * tq - off, LANE)
    q = _stack_heads(q_ref[...], jnp.zeros((tq, LANE), F32))
    logits = _group_logits(q, kt_sc[:, pl.ds(start, w)], tq)
    v1 = v_sc[pl.ds(start, w), :]
    tile = jnp.minimum(qi, n_off - 1)
    sum_lane = lax.broadcasted_iota(jnp.int32, (tq, LANE), 1) >= HEAD_DIM
    accs = []
    for h in range(GROUP):
        s = logits[h] + bias_ref[0, tile, h]
        m = jnp.broadcast_to(jnp.max(s, axis=-1, keepdims=True), (tq, LANE))
        if has_sink:
            sink = sink_ref[0, h]
            m = jnp.maximum(m, sink)
        p = jnp.exp2(s - jnp.concatenate([m] * (w // LANE), axis=1))
        acc = _dot(p.astype(BF16), v1)
        if has_sink:
            acc = acc + jnp.where(sum_lane, jnp.exp2(sink - m), 0.0)
        accs.append(acc)
    o_ref[...] = _finish_group(accs).astype(BF16)


def _sel_attn_kernel(q_ref, kv_ref, drop_ref, rows_ref, bias_ref, o_ref, kt_sc, v_sc, m_sc, acc_sc,
                     *, tq, tk, n_bias):
    qi = pl.program_id(2)

    @pl.when(qi == 0)
    def _():
        _split_kv(kv_ref, kt_sc, v_sc, rows_ref[...])

    q = _stack_heads(q_ref[...], drop_ref[0, 0].astype(F32))
    hi = ((qi + 1) * tq + tk - 1) // tk
    m_sc[...] = jnp.full(m_sc.shape, NEG_INF, F32)
    acc_sc[...] = jnp.zeros(acc_sc.shape, F32)

    def step(kj):
        start = pl.multiple_of(kj * tk, tk)
        logits = _group_logits(q, kt_sc[:, pl.ds(start, tk)], tq)
        v1 = v_sc[pl.ds(start, tk), :]
        u = jnp.minimum((qi * tq - kj * tk) // tq, n_bias - 1)
        for h in range(GROUP):
            s = logits[h] + bias_ref[0, u, h]
            _online_update(s, v1, m_sc, acc_sc, h)

    odd = hi % 2

    @pl.when(odd == 1)
    def _():
        step(hi - 1)

    def body(i, carry):
        kj = hi - odd - 1 - 2 * i
        step(kj)
        step(kj - 1)
        return carry

    lax.fori_loop(0, hi // 2, body, 0)
    o_ref[...] = _finish_group([acc_sc[h] for h in range(GROUP)]).astype(BF16)


def _gqa_specs(z, q_block, kv_block, tq, seq):
    nq = seq // tq
    qblk = q_block * LANE // GROUP_WIDTH
    return [
        pl.BlockSpec((tq, GROUP_WIDTH), lambda h, bb, i: (bb * nq + i, qblk + h)),
        pl.BlockSpec((seq, LANE), lambda h, bb, i: (bb, kv_block + h)),
    ]


def _resident(a):
    return pl.BlockSpec((1,) + a.shape[1:], lambda h, bb, i: (h,) + (0,) * (a.ndim - 1),
                        pipeline_mode=pl.Buffered(1))


def _gqa_out(m, tq, seq):
    nq = seq // tq
    return (pl.BlockSpec((tq, GROUP_WIDTH), lambda h, bb, i: (bb * nq + i, h)),
            jax.ShapeDtypeStruct((m, BRANCH_WIDTH), BF16))


def _kv_scratch(seq, key_rows):
    return [pltpu.VMEM((key_rows, seq), BF16), pltpu.VMEM((seq, LANE), BF16)]


def _band_attention(z, bias, *, q_block, kv_block, tq, seq, back, name, sinks=None):
    m = z.shape[0]
    b = m // seq
    n_off, w = bias.shape[1], bias.shape[4]
    assert w == back + tq and w <= seq and back % LANE == 0
    kern = functools.partial(_band_attn_kernel, tq=tq, w=w, back=back, n_off=n_off,
                             has_sink=sinks is not None)
    in_specs = _gqa_specs(z, q_block, kv_block, tq, seq) + [_resident(bias)]
    args = [z, z, bias]
    if sinks is not None:
        in_specs.append(pl.BlockSpec((1, GROUP, 1, 1), lambda h, bb, i: (h, 0, 0, 0)))
        args.append(sinks)
    out_spec, out_shape = _gqa_out(m, tq, seq)
    return pl.pallas_call(
        kern,
        grid=(N_KV, b, seq // tq),
        in_specs=in_specs,
        out_specs=out_spec,
        out_shape=out_shape,
        scratch_shapes=_kv_scratch(seq, LANE),
        compiler_params=_cparams(("parallel", "parallel", "arbitrary")),
        name=name,
    )(*args)


def _sel_attention(z, dropped, sel_rows, bias, *, tq, tk, seq):
    m = z.shape[0]
    b = m // seq
    kern = functools.partial(_sel_attn_kernel, tq=tq, tk=tk, n_bias=bias.shape[1])
    in_specs = _gqa_specs(z, ZB_NQ, ZB_SEL_KV, tq, seq) + [
        pl.BlockSpec((1, 1, tq, LANE), lambda h, bb, i: (bb, h, i, 0)),
        pl.BlockSpec((HEAD_DIM, seq), lambda h, bb, i: (0, 0), pipeline_mode=pl.Buffered(1)),
        _resident(bias),
    ]
    out_spec, out_shape = _gqa_out(m, tq, seq)
    return pl.pallas_call(
        kern,
        grid=(N_KV, b, seq // tq),
        in_specs=in_specs,
        out_specs=out_spec,
        out_shape=out_shape,
        scratch_shapes=_kv_scratch(seq, LANE) + [pltpu.VMEM((GROUP, tq, LANE), F32),
                                                 pltpu.VMEM((GROUP, tq, LANE), F32)],
        compiler_params=_cparams(("parallel", "parallel", "arbitrary")),
        name="attn_sel",
    )(z, z, dropped, sel_rows, bias)


def _rope_lanes(x, cos_t, sin_a, sin_b):
    half = MLA_ROPE // 2
    return x * cos_t + pltpu.roll(x, LANE - half, 1) * sin_a + pltpu.roll(x, half, 1) * sin_b


def _mla_proj_kernel(mq_ref, mkv_ref, kr_ref, cos_ref, sina_ref, sinb_ref, qn_ref, wq_ref,
                     kvn_ref, wkt_ref, wv_ref, oq_ref, okt_ref, ov_ref):
    cos_t, sin_a, sin_b = cos_ref[...], sina_ref[...], sinb_ref[...]
    ql = _rms(mq_ref[...].astype(F32), qn_ref[...]).astype(BF16)
    q = _dot(ql, wq_ref[...]) * (MLA_QK ** -0.5 * LOG2E)
    kvl = _rms(mkv_ref[...].astype(F32), kvn_ref[...]).astype(BF16)
    kt = _dot_nt(wkt_ref[...], kvl)
    v = _dot(kvl, wv_ref[...])
    kr_t = _rope_lanes(kr_ref[...].astype(F32), cos_t, sin_a, sin_b).T
    lane = lax.broadcasted_iota(jnp.int32, (v.shape[0], LANE), 1)
    for h in range(N_HEADS):
        blk = slice(h * LANE, (h + 1) * LANE)
        oq_ref[:, blk] = _rope_lanes(q[:, blk], cos_t, sin_a, sin_b).astype(BF16)
        okt_ref[0, blk, :] = (kt[blk, :] + kr_t).astype(BF16)
        ov_ref[:, blk] = jnp.where(lane < MLA_V, v[:, blk], 1.0).astype(BF16)


def _mla_proj(z, cos_t, sin_a, sin_b, q_norm, w_qb, kv_norm, w_kt, w_v, *, tm, seq):
    m = z.shape[0]
    nt = seq // tm
    width = N_HEADS * LANE

    def full(a):
        return pl.BlockSpec(a.shape, lambda i: (0,) * a.ndim)

    tab = pl.BlockSpec((tm, LANE), lambda i: (i % nt, 0))
    out = jax.ShapeDtypeStruct((m, width), BF16)
    oblk = pl.BlockSpec((tm, width), lambda i: (i, 0))
    return pl.pallas_call(
        _mla_proj_kernel,
        grid=(m // tm,),
        in_specs=[
            pl.BlockSpec((tm, MLA_Q_RANK), lambda i: (i, ZB_MQ * LANE // MLA_Q_RANK)),
            pl.BlockSpec((tm, MLA_KV_RANK), lambda i: (i, ZB_MKV * LANE // MLA_KV_RANK)),
            pl.BlockSpec((tm, LANE), lambda i: (i, ZB_KR)),
            tab, tab, tab,
            full(q_norm), full(w_qb), full(kv_norm), full(w_kt), full(w_v),
        ],
        out_specs=[oblk, pl.BlockSpec((1, width, tm), lambda i: (i // nt, 0, i % nt)), oblk],
        out_shape=[out, jax.ShapeDtypeStruct((m // seq, width, seq), BF16), out],
        compiler_params=_cparams(("parallel",)),
        name="mla_proj",
    )(z, z, z, cos_t, sin_a, sin_b, q_norm, w_qb, kv_norm, w_kt, w_v)


def _mla_attn_kernel(q_ref, k_ref, v_ref, o_ref, m_sc, acc_sc, *, tq):
    qi = pl.program_id(2)
    n_chain = m_sc.shape[0]
    qs = [q_ref[:, c * LANE:(c + 1) * LANE] for c in range(n_chain)]
    m_sc[...] = jnp.full(m_sc.shape, NEG_INF, F32)
    acc_sc[...] = jnp.zeros(acc_sc.shape, F32)

    def step(tile, width, diagonal):
        start = pl.multiple_of(tile * tq, tq)
        for c in range(n_chain):
            blk = slice(c * LANE, (c + 1) * LANE)
            s = _dot(qs[c], k_ref[0, blk, pl.ds(start, width * tq)])
            if diagonal:
                row = lax.broadcasted_iota(jnp.int32, s.shape, 0)
                col = lax.broadcasted_iota(jnp.int32, s.shape, 1)
                s = jnp.where(col <= row, s, NEG_INF)
            _online_update(s, v_ref[pl.ds(start, width * tq), blk], m_sc, acc_sc, c)

    step(qi, 1, True)
    odd = qi % 2

    @pl.when(odd == 1)
    def _():
        step(qi - 1, 1, False)

    def body(i, carry):
        step(qi - odd - 2 * (i + 1), 2, False)
        return carry

    lax.fori_loop(0, qi // 2, body, 0)
    for c in range(n_chain):
        acc = acc_sc[c]
        lane = lax.broadcasted_iota(jnp.int32, acc.shape, 1)
        den = jnp.where(lane < MLA_V, pltpu.roll(acc, MLA_V, 1), acc)
        o_ref[:, c * LANE:(c + 1) * LANE] = (acc / den).astype(BF16)


def _mla_attention(q, k, v1, *, tq, seq):
    m = q.shape[0]
    b = m // seq
    nq = seq // tq
    width = MLA_CHAINS * LANE
    slab = pl.BlockSpec((seq, width), lambda h, bb, i: (bb, h))
    slab_t = pl.BlockSpec((1, width, seq), lambda h, bb, i: (bb, h, 0))
    tile = pl.BlockSpec((tq, width), lambda h, bb, i: (bb * nq + i, h))
    return pl.pallas_call(
        functools.partial(_mla_attn_kernel, tq=tq),
        grid=(N_HEADS // MLA_CHAINS, b, nq),
        in_specs=[tile, slab_t, slab],
        out_specs=tile,
        out_shape=jax.ShapeDtypeStruct(q.shape, BF16),
        scratch_shapes=[pltpu.VMEM((MLA_CHAINS, tq, LANE), F32),
                        pltpu.VMEM((MLA_CHAINS, tq, LANE), F32)],
        compiler_params=_cparams(("parallel", "parallel", "arbitrary")),
        name="attn_mla",
    )(q, k, v1)


def _merge_kernel(ocmp_ref, osel_ref, owin_ref, ob_ref, oc_ref, ng_ref, mg_ref, x_ref,
                  gexp_ref, wab_ref, wc_ref, wo_ref, gpost_ref, o_ref):
    ng = jax.nn.sigmoid(ng_ref[...].astype(F32))
    ng_hi = ng.astype(BF16)
    ng_lo = (ng - ng_hi.astype(F32)).astype(BF16)
    gates = _dot(jnp.concatenate([ng_hi, ng_lo], axis=1), gexp_ref[...])
    o_a = jnp.zeros(ocmp_ref.shape, F32)
    for n, ref in enumerate((ocmp_ref, osel_ref, owin_ref)):
        gate = gates[:, n * BRANCH_WIDTH:(n + 1) * BRANCH_WIDTH]
        o_a = o_a + gate * ref[...].astype(F32)
    ys = (_dot(o_a.astype(BF16), wab_ref[0]), _dot(ob_ref[...], wab_ref[1]),
          _dot(oc_ref[...], wc_ref[...]))
    mixed = jnp.zeros(x_ref.shape, F32)
    for n, y in enumerate(ys):
        gate = jax.nn.sigmoid(mg_ref[:, n * D_MODEL:(n + 1) * D_MODEL].astype(F32))
        mixed = mixed + gate * y
    out = _dot(mixed.astype(BF16), wo_ref[...])
    o_ref[...] = x_ref[...] + _rms(out, gpost_ref[...])


def _merge(o_cmp, o_sel, o_win, o_b, o_c, z, x, gexp, w_ab, w_c, w_out, g_post, *, tm):
    m, d = x.shape
    br = pl.BlockSpec((tm, o_cmp.shape[1]), lambda i: (i, 0))

    def full(a):
        return pl.BlockSpec(a.shape, lambda i: (0,) * a.ndim)

    return pl.pallas_call(
        _merge_kernel,
        grid=(m // tm,),
        in_specs=[
            br, br, br, br,
            pl.BlockSpec((tm, o_c.shape[1]), lambda i: (i, 0)),
            pl.BlockSpec((tm, LANE), lambda i: (i, ZB_NGATE)),
            pl.BlockSpec((tm, 3 * d), lambda i: (i, ZB_MGATE * LANE // (3 * d))),
            pl.BlockSpec((tm, d), lambda i: (i, 0)),
            full(gexp), full(w_ab), full(w_c), full(w_out), full(g_post),
        ],
        out_specs=pl.BlockSpec((tm, d), lambda i: (i, 0)),
        out_shape=jax.ShapeDtypeStruct((m, d), F32),
        compiler_params=_cparams(("parallel",)),
        name="merge",
    )(o_cmp, o_sel, o_win, o_b, o_c, z, z, x, gexp, w_ab, w_c, w_out, g_post)


def _t5_bucket(dist):
    max_exact = NUM_BUCKETS // 2
    d = jnp.maximum(dist, 0)
    df = jnp.maximum(d, 1).astype(F32)
    large = max_exact + (jnp.log(df / max_exact) / math.log(MAX_DISTANCE / max_exact)
                         * (NUM_BUCKETS - max_exact)).astype(jnp.int32)
    large = jnp.minimum(large, NUM_BUCKETS - 1)
    return jnp.where(d < max_exact, d, large)


def _toeplitz(fn, offs, tq, tk):
    ln = tq + tk - 1
    i = jnp.arange(ln + 1)
    shift = jnp.where(i < tk, -i, ln + 1 - i)
    ext = jnp.moveaxis(fn(jnp.asarray(offs)[:, None] + shift[None, :]), -1, 0)
    flat = jnp.tile(ext, (1, 1, tq))[:, :, :tq * ln]
    return flat.reshape(ext.shape[0], len(offs), tq, ln)[:, :, :, :tk]


def _bias_fn(table, window):
    def fn(dist):
        ok = dist >= 0
        if window is not None:
            ok = ok & (dist < window)
        return jnp.where(ok[..., None], table[_t5_bucket(dist)].astype(F32), NEG_INF)
    return fn


def _head_tiles(t):
    return t.reshape(N_KV, GROUP, *t.shape[1:]).transpose(0, 2, 1, 3, 4)


def _in_proj_weight(w_in):
    sizes = (512, 128, 128, 128, 128, 128, 128, 24, 512, 128, 128, MLA_Q_RANK, MLA_KV_RANK,
             MLA_ROPE, 3 * D_MODEL)
    offs = np.concatenate([[0], np.cumsum(sizes)])
    seg = [w_in[:, offs[i]:offs[i + 1]] for i in range(len(sizes))]
    (nq, nkc, nvc, nks, nvs, nkw, nvw, ngate, sq, sk, sv, mq, mkv, mkr, mgate) = seg
    d = w_in.shape[0]
    hd = HEAD_DIM

    def pad(a, left=0):
        return jnp.pad(a, ((0, 0), (left, LANE - left - a.shape[1])))

    def kv_blocks(k, v):
        return [jnp.concatenate([k[:, h * hd:(h + 1) * hd], v[:, h * hd:(h + 1) * hd]], axis=1)
                for h in range(N_KV)]

    cols = ([mq, pad(ngate), mkv, pad(mkr, MLA_NOPE), jnp.zeros((d, LANE), w_in.dtype),
             nq * QK_SCALE, nkc, nvc] + kv_blocks(nks, nvs) + kv_blocks(nkw, nvw)
            + [sq * QK_SCALE] + kv_blocks(sk, sv) + [mgate])
    w = jnp.concatenate(cols, axis=1)
    assert w.shape == (d, Z_WIDTH)
    return w.astype(BF16)


def _mla_weights(w_qb, w_kvb, w_c):
    def blocks(a):
        return jnp.pad(a, ((0, 0), (0, 0), (0, LANE - a.shape[2]))).reshape(a.shape[0], -1)

    wq = blocks(w_qb.reshape(MLA_Q_RANK, N_HEADS, MLA_QK))
    wkv = w_kvb.reshape(MLA_KV_RANK, N_HEADS, MLA_NOPE + MLA_V)
    wkt = blocks(wkv[:, :, :MLA_NOPE]).T
    wv = blocks(wkv[:, :, MLA_NOPE:])
    wc = jnp.pad(w_c.reshape(N_HEADS, MLA_V, -1), ((0, 0), (0, LANE - MLA_V), (0, 0)))
    return (wq.astype(BF16), wkt.astype(BF16), wv.astype(BF16),
            wc.reshape(N_HEADS * LANE, -1).astype(BF16))


def _gate_expand():
    e = np.zeros((2 * LANE, 3 * BRANCH_WIDTH), np.float32)
    for n in range(3):
        for h in range(N_HEADS):
            cols = slice(n * BRANCH_WIDTH + h * HEAD_DIM, n * BRANCH_WIDTH + (h + 1) * HEAD_DIM)
            e[h * 3 + n, cols] = 1.0
            e[LANE + h * 3 + n, cols] = 1.0
    return jnp.asarray(e, BF16)


def _rope_tables(pos):
    half = MLA_ROPE // 2
    inv_freq = ROPE_BASE ** (-jnp.arange(half, dtype=F32) / half)
    ang = pos.astype(F32)[:, None] * inv_freq[None, :]
    cos, sin = jnp.cos(ang), jnp.sin(ang)
    n = pos.shape[0]
    ones = jnp.ones((n, MLA_NOPE), F32)
    z16 = jnp.zeros((n, half), F32)
    tail = jnp.zeros((n, LANE - MLA_QK), F32)
    cos_t = jnp.concatenate([ones, cos, cos, tail], axis=1)
    sin_a = jnp.concatenate([0 * ones, -sin, z16, tail], axis=1)
    sin_b = jnp.concatenate([0 * ones, z16, sin, tail], axis=1)
    return cos_t, sin_a, sin_b


def kernel(x, rel_bias_table, ffn1_norm_pre, ffn1_w_gu, ffn1_w_down, ffn1_norm_post,
           mix_norm_pre, w_in, nsa_pe_k, nsa_w1_k, nsa_w2_k, nsa_pe_v, nsa_w1_v, nsa_w2_v,
           swa_sinks, mla_q_norm, mla_w_qb, mla_kv_norm, mla_w_kvb, w_branch, w_out,
           mix_norm_post, ffn2_norm_pre, ffn2_w_gu, ffn2_w_down, ffn2_norm_post):
    b, s, d = x.shape
    depth = w_in.shape[0]
    m = b * s
    tq = 256
    tq_band = min(256, s // 2)
    tq_sel = min(512, s)
    tk_sel = min(512, s)
    tq_mla = min(512, s)
    ncp = s // CMP_STRIDE
    nsel = s // SEL_LEN
    tm = min(512, m)
    tm_big = min(1024, s)

    pos = jnp.arange(s, dtype=jnp.int32)
    tab_a = rel_bias_table[:, :N_HEADS] * LOG2E
    tab_b = rel_bias_table[:, N_HEADS:] * LOG2E
    n_far = -(-(LAST_BUCKET_DIST + tk_sel - 1) // tq_sel)
    n_sel_tiles = min(n_far + 1, s // tq_sel)
    bias_sel = _head_tiles(_toeplitz(_bias_fn(tab_a, None),
                                     [u * tq_sel for u in range(n_sel_tiles)], tq_sel, tk_sel))
    tb = tq_band
    win_offs = sorted({min(i * tb, NSA_WINDOW) for i in range(-(-NSA_WINDOW // tb) + 1)})
    bias_win = _head_tiles(_toeplitz(_bias_fn(tab_a, NSA_WINDOW), win_offs, tb, NSA_WINDOW + tb))
    swa_offs = sorted({min(i * tb, SWA_WINDOW) for i in range(-(-SWA_WINDOW // tb) + 1)})
    bias_swa = _head_tiles(_toeplitz(_bias_fn(tab_b, SWA_WINDOW), swa_offs, tb, SWA_WINDOW + tb))

    def cmp_bias_fn(e):
        r = jnp.arange(CMP_STRIDE)
        dist = e[..., None] * CMP_STRIDE + r - (CMP_LEN - 1)
        return tab_a[_t5_bucket(dist)].astype(F32).reshape(*e.shape, CMP_STRIDE * N_HEADS)

    bias_c = _toeplitz(cmp_bias_fn, [0], ncp, ncp).reshape(CMP_STRIDE, N_HEADS, ncp, ncp)
    bias_c = bias_c.transpose(1, 2, 0, 3).reshape(N_KV, GROUP, s, ncp)
    ci = jnp.arange(ncp)[None, :] * CMP_STRIDE
    sj = jnp.arange(nsel)[:, None] * SEL_LEN
    ovl_t = (jnp.maximum(jnp.minimum(ci + CMP_LEN, sj + SEL_LEN) - jnp.maximum(ci, sj), 0)
             .astype(F32) / CMP_LEN)
    cos_t, sin_a, sin_b = _rope_tables(pos)
    gexp = _gate_expand()
    assert nsel <= HEAD_DIM
    sel_rows = jnp.where(jnp.arange(HEAD_DIM)[:, None] == (pos[None, :] >> SEL_SHIFT),
                         NEG_INF, 0.0).astype(BF16)
    chunk = CMP_STRIDE * HEAD_DIM

    xf = x.reshape(m, d)
    for l in range(depth):
        xf = _ffn(xf, ffn1_norm_pre[l][None], ffn1_w_gu[l].astype(BF16),
                  ffn1_w_down[l].astype(BF16), ffn1_norm_post[l][None], tm=tm)

        z = _inproj(xf, mix_norm_pre[l][None], _in_proj_weight(w_in[l]), tm=tm_big, tn=Z_WIDTH // 4)

        def chunks(blk):
            a = z[:, blk * LANE:(blk + 1) * LANE].reshape(b, s, N_KV, HEAD_DIM)
            return a.transpose(0, 2, 1, 3).reshape(b, N_KV, ncp, chunk)

        kc, vc = _cmp_mlp(chunks(ZB_CMP_K), chunks(ZB_CMP_V),
                          nsa_pe_k[l].reshape(2, chunk), nsa_w1_k[l].astype(BF16),
                          nsa_w2_k[l].astype(BF16),
                          nsa_pe_v[l].reshape(2, chunk), nsa_w1_v[l].astype(BF16),
                          nsa_w2_v[l].astype(BF16))
        o_cmp, dropped = _cmp_attn(z, jnp.tile(kc.transpose(0, 1, 3, 2), (1, 1, 2, 1)),
                                   jnp.tile(vc, (1, 1, 1, 2)),
                                   bias_c, ovl_t, tq=tq, seq=s)
        o_sel = _sel_attention(z, dropped, sel_rows, bias_sel, tq=tq_sel, tk=tk_sel, seq=s)
        o_win = _band_attention(z, bias_win, q_block=ZB_NQ, kv_block=ZB_WIN_KV, tq=tq_band, seq=s,
                                back=NSA_WINDOW, name="attn_win")
        o_b = _band_attention(z, bias_swa, q_block=ZB_SQ, kv_block=ZB_SWA_KV, tq=tq_band, seq=s,
                              back=SWA_WINDOW, name="attn_swa",
                              sinks=(swa_sinks[l] * LOG2E).reshape(N_KV, GROUP, 1, 1))
        wq, wkt, wv, wc = _mla_weights(mla_w_qb[l], mla_w_kvb[l], w_branch[l, 2])
        mq, mk, mv1 = _mla_proj(z, cos_t, sin_a, sin_b, mla_q_norm[l][None], wq,
                                mla_kv_norm[l][None], wkt, wv, tm=tm_big, seq=s)
        o_c = _mla_attention(mq, mk, mv1, tq=tq_mla, seq=s)

        xf = _merge(o_cmp, o_sel, o_win, o_b, o_c, z, xf, gexp, w_branch[l, :2].astype(BF16), wc,
                    w_out[l].astype(BF16), mix_norm_post[l][None], tm=tm)

        xf = _ffn(xf, ffn2_norm_pre[l][None], ffn2_w_gu[l].astype(BF16),
                  ffn2_w_down[l].astype(BF16), ffn2_norm_post[l][None], tm=tm)
    return xf.reshape(b, s, d)
```

```python
import functools
import math

import jax
import jax.numpy as jnp
import numpy as np
from jax import lax
from jax.experimental import pallas as pl
from jax.experimental.pallas import tpu as pltpu

F32 = jnp.float32
BF16 = jnp.bfloat16

D_MODEL = 1024
D_FF = 2816
HEAD_DIM = 64
NORM_EPS = 1e-6
NUM_BUCKETS = 32
MAX_DISTANCE = 1024
LAST_BUCKET_DIST = 1 + math.ceil(
    (NUM_BUCKETS // 2) * (MAX_DISTANCE / (NUM_BUCKETS // 2))
    ** ((NUM_BUCKETS // 2 - 1) / (NUM_BUCKETS - NUM_BUCKETS // 2)))
N_HEADS = 8
N_KV = 2
GROUP = N_HEADS // N_KV
CMP_LEN = 32
CMP_STRIDE = 16
CMP_HIDDEN = 128
SEL_LEN = 64
SEL_SHIFT = 6
SEL_TOPK = 16
NSA_WINDOW = 512
SWA_WINDOW = 128
MLA_Q_RANK = 384
MLA_KV_RANK = 256
MLA_NOPE = 64
MLA_ROPE = 32
MLA_V = 64
MLA_QK = MLA_NOPE + MLA_ROPE
ROPE_BASE = 10000.0
BRANCH_WIDTH = N_HEADS * HEAD_DIM
NEG_INF = -1e30
SEL_FORCE = 1e9
LOG2E = math.log2(math.e)
QK_SCALE = HEAD_DIM ** -0.5 * LOG2E

LANE = 128
SUBLANES = 8
VMEM_LIMIT = 56 * 1024 * 1024
GROUP_WIDTH = GROUP * HEAD_DIM
FFN_CHAINS = 2
FFN_CHUNK = 256
CMP_PARTS = 4
QK_SPLIT = 2
KV_SPLIT_CHUNK = 512
MLA_CHAINS = 4

ZB_MQ = 0
ZB_NGATE = 3
ZB_MKV = 4
ZB_KR = 6
ZB_NQ = 8
ZB_CMP_K = 12
ZB_CMP_V = 13
ZB_SEL_KV = 14
ZB_WIN_KV = 16
ZB_SQ = 18
ZB_SWA_KV = 22
ZB_MGATE = 24
Z_BLOCKS = 48
Z_WIDTH = Z_BLOCKS * LANE


def _cparams(sem):
    return pltpu.CompilerParams(dimension_semantics=sem, vmem_limit_bytes=VMEM_LIMIT)


def _rms(x, g):
    inv = lax.rsqrt(jnp.mean(x * x, axis=-1, keepdims=True) + NORM_EPS)
    return (x * inv) * g


def _dot(a, b):
    return jnp.dot(a, b, preferred_element_type=F32)


def _dot_nt(a, b, precision=None):
    return lax.dot_general(a, b, (((1,), (1,)), ((), ())), precision=precision,
                           preferred_element_type=F32)


def _ffn_kernel(x_ref, gpre_ref, wg_ref, wu_ref, wd_ref, gpost_ref, o_ref):
    rows = x_ref.shape[0] // FFN_CHAINS
    for c in range(FFN_CHAINS):
        sl = slice(c * rows, (c + 1) * rows)
        x = x_ref[sl, :]
        h = _rms(x, gpre_ref[...]).astype(BF16)
        y = jnp.zeros(x.shape, F32)
        for j in range(D_FF // FFN_CHUNK):
            cols = slice(j * FFN_CHUNK, (j + 1) * FFN_CHUNK)
            g = _dot(h, wg_ref[:, cols])
            u = _dot(h, wu_ref[:, cols])
            act = (g * jax.nn.sigmoid(g)) * u
            y = y + _dot(act.astype(BF16), wd_ref[cols, :])
        o_ref[sl, :] = x + 0.5 * _rms(y, gpost_ref[...])


def _ffn(x, g_pre, w_gu, w_down, g_post, *, tm):
    m, d = x.shape
    once = pl.Buffered(1)
    return pl.pallas_call(
        _ffn_kernel,
        grid=(m // tm,),
        in_specs=[
            pl.BlockSpec((tm, d), lambda i: (i, 0)),
            pl.BlockSpec((1, d), lambda i: (0, 0)),
            pl.BlockSpec((d, D_FF), lambda i: (0, 0), pipeline_mode=once),
            pl.BlockSpec((d, D_FF), lambda i: (0, 1), pipeline_mode=once),
            pl.BlockSpec((D_FF, d), lambda i: (0, 0), pipeline_mode=once),
            pl.BlockSpec((1, d), lambda i: (0, 0)),
        ],
        out_specs=pl.BlockSpec((tm, d), lambda i: (i, 0)),
        out_shape=jax.ShapeDtypeStruct((m, d), F32),
        compiler_params=_cparams(("parallel",)),
        name="ffn",
    )(x, g_pre, w_gu, w_gu, w_down, g_post)


def _inproj_kernel(x_ref, g_ref, w_ref, o_ref, h_sc):
    @pl.when(pl.program_id(1) == 0)
    def _():
        h_sc[...] = _rms(x_ref[...], g_ref[...]).astype(BF16)

    o_ref[...] = _dot(h_sc[...], w_ref[...]).astype(BF16)


def _inproj(x, g, w, *, tm, tn):
    m, d = x.shape
    n = w.shape[1]
    return pl.pallas_call(
        _inproj_kernel,
        grid=(m // tm, n // tn),
        in_specs=[
            pl.BlockSpec((tm, d), lambda i, j: (i, 0)),
            pl.BlockSpec((1, d), lambda i, j: (0, 0)),
            pl.BlockSpec((d, tn), lambda i, j: (0, j)),
        ],
        out_specs=pl.BlockSpec((tm, tn), lambda i, j: (i, j)),
        out_shape=jax.ShapeDtypeStruct((m, n), BF16),
        scratch_shapes=[pltpu.VMEM((tm, d), BF16)],
        compiler_params=_cparams(("parallel", "arbitrary")),
        name="inproj",
    )(x, g, w)


def _cmp_mlp_kernel(k_ref, v_ref, pek_ref, w1k_ref, w2k_ref, pev_ref, w1v_ref, w2v_ref,
                    kc_ref, vc_ref):
    half = CMP_STRIDE * HEAD_DIM

    def one(x_ref, pe_ref, w1_ref, w2_ref, o_ref):
        x = x_ref[0, 0].astype(F32)
        ncp = x.shape[0]
        xa = (x + pe_ref[0:1, :]).astype(BF16)
        xb = (x + pe_ref[1:2, :]).astype(BF16)
        pa = _dot(xa, w1_ref[0:half, :])
        pb = _dot(xb, w1_ref[half:2 * half, :])
        hid = pa + pltpu.roll(pb, ncp - 1, 0)
        out = _dot(jax.nn.gelu(hid).astype(BF16), w2_ref[...])
        row = lax.broadcasted_iota(jnp.int32, out.shape, 0)
        o_ref[0, 0] = jnp.where(row < ncp - 1, out, 0.0).astype(BF16)

    one(k_ref, pek_ref, w1k_ref, w2k_ref, kc_ref)
    one(v_ref, pev_ref, w1v_ref, w2v_ref, vc_ref)


def _cmp_mlp(k4, v4, pek, w1k, w2k, pev, w1v, w2v):
    b, hkv, ncp, width = k4.shape
    blk = pl.BlockSpec((1, 1, ncp, width), lambda i, j: (i, j, 0, 0))
    oblk = pl.BlockSpec((1, 1, ncp, HEAD_DIM), lambda i, j: (i, j, 0, 0))

    def full(a):
        return pl.BlockSpec(a.shape, lambda i, j: (0,) * a.ndim)

    out = jax.ShapeDtypeStruct((b, hkv, ncp, HEAD_DIM), BF16)
    return pl.pallas_call(
        _cmp_mlp_kernel,
        grid=(b, hkv),
        in_specs=[blk, blk, full(pek), full(w1k), full(w2k), full(pev), full(w1v), full(w2v)],
        out_specs=[oblk, oblk],
        out_shape=[out, out],
        compiler_params=_cparams(("parallel", "parallel")),
        name="cmp_mlp",
    )(k4, v4, pek, w1k, w2k, pev, w1v, w2v)


def _stack_heads(q_all, upper):
    q32 = q_all.astype(F32)
    lane = lax.broadcasted_iota(jnp.int32, upper.shape, 1)
    rows = []
    for pair in range(GROUP // 2):
        half = q32[:, pair * LANE:(pair + 1) * LANE]
        rows.append(jnp.where(lane < HEAD_DIM, half, upper))
        rows.append(jnp.where(lane < HEAD_DIM, pltpu.roll(half, HEAD_DIM, 1), upper))
    return jnp.concatenate(rows, axis=0).astype(BF16)


def _split_kv(kv_ref, kt_sc, v_sc, key_rows=None):
    seq = kv_ref.shape[0]
    chunk = min(KV_SPLIT_CHUNK, seq)
    copies = kt_sc.shape[0] // HEAD_DIM - (key_rows is not None)
    lane = lax.broadcasted_iota(jnp.int32, (chunk, LANE), 1)
    for c in range(seq // chunk):
        rows = slice(c * chunk, (c + 1) * chunk)
        x = kv_ref[rows, :].astype(F32)
        kt = x.T[0:HEAD_DIM, :].astype(BF16)
        for r in range(copies):
            kt_sc[r * HEAD_DIM:(r + 1) * HEAD_DIM, rows] = kt
        v_sc[rows, :] = jnp.where(lane < HEAD_DIM, pltpu.roll(x, HEAD_DIM, 1), 1.0).astype(BF16)
    if key_rows is not None:
        kt_sc[copies * HEAD_DIM:(copies + 1) * HEAD_DIM, :] = key_rows


def _group_logits(q_stack, kt, tq):
    per = GROUP // QK_SPLIT
    parts = [_dot(q_stack[p * per * tq:(p + 1) * per * tq], kt) for p in range(QK_SPLIT)]
    return [parts[h // per][(h % per) * tq:(h % per + 1) * tq] for h in range(GROUP)]


def _finish_pair_rolled(acc_even, acc_odd):
    lane = lax.broadcasted_iota(jnp.int32, acc_even.shape, 1)
    even = acc_even / pltpu.roll(acc_even, HEAD_DIM, 1)
    odd = pltpu.roll(acc_odd, HEAD_DIM, 1) / acc_odd
    return jnp.where(lane < HEAD_DIM, even, odd)


def _finish_group(accs):
    return jnp.concatenate([_finish_pair_rolled(accs[0], accs[1]),
                            _finish_pair_rolled(accs[2], accs[3])], axis=1)


def _online_update(s, v, m_ref, acc_ref, idx):
    m_old = m_ref[idx]
    m_new = jnp.maximum(m_old, jnp.max(s, axis=-1, keepdims=True))
    alpha = jnp.exp2(m_old - m_new)
    p = jnp.exp2(s - jnp.concatenate([m_new] * (s.shape[1] // LANE), axis=1))
    acc = acc_ref[idx]
    acc_ref[idx] = (jnp.concatenate([alpha] * (acc.shape[1] // LANE), axis=1) * acc
                    + _dot(p.astype(BF16), v))
    m_ref[idx] = m_new


def _cmp_attn_kernel(q_ref, kc_ref, vc_ref, bias_ref, ovl_ref, o_ref, sel_ref, *, tq, ncp, nsel, nq):
    qi = pl.program_id(1)
    parts = CMP_PARTS if nq % CMP_PARTS == 0 and nsel % (CMP_PARTS * SUBLANES) == 0 else 1
    for k in range(parts):
        ncols = min(ncp, -(-((k + 1) * ncp // parts) // LANE) * LANE)
        nblk = (k + 1) * nsel // parts
        pl.when(qi // (nq // parts) == k)(functools.partial(
            _cmp_attn_body, q_ref, kc_ref, vc_ref, bias_ref, ovl_ref, o_ref, sel_ref,
            tq=tq, ncols=ncols, nblk=nblk))


def _cmp_attn_body(q_ref, kc_ref, vc_ref, bias_ref, ovl_ref, o_ref, sel_ref, *, tq, ncols, nblk):
    qi = pl.program_id(1)
    vc = vc_ref[0, 0, 0:ncols, :]
    lane = lax.broadcasted_iota(jnp.int32, (tq, LANE), 1)
    row = lax.broadcasted_iota(jnp.int32, (tq, ncols), 0) + qi * tq
    col = lax.broadcasted_iota(jnp.int32, (tq, ncols), 1)
    valid = (col * CMP_STRIDE + (CMP_LEN - 1)) <= row
    p_sum = jnp.zeros((tq, ncols), F32)
    outs = []
    q = _stack_heads(q_ref[...], jnp.zeros((tq, LANE), F32))
    logits = _group_logits(q, kc_ref[0, 0, :, 0:ncols], tq)
    for g in range(GROUP):
        s = logits[g] + bias_ref[0, g, :, 0:ncols]
        s = jnp.where(valid, s, NEG_INF)
        e = jnp.where(valid, jnp.exp2(s - jnp.max(s, axis=-1, keepdims=True)), 0.0)
        den = jnp.sum(e, axis=-1, keepdims=True)
        p = e / jnp.where(den > 0, den, 1.0)
        outs.append(_dot(p.astype(BF16), vc))
        p_sum = p_sum + p
    o_ref[...] = jnp.concatenate([jnp.where(lane < HEAD_DIM, outs[0], outs[1]),
                                  jnp.where(lane < HEAD_DIM, outs[2], outs[3])],
                                 axis=1).astype(BF16)

    imp = _dot_nt(ovl_ref[0:nblk, 0:ncols], p_sum, precision=lax.Precision.HIGHEST)
    jj = lax.broadcasted_iota(jnp.int32, (nblk, tq), 0)
    cur = (lax.broadcasted_iota(jnp.int32, (nblk, tq), 1) + qi * tq) >> SEL_SHIFT
    forced = (jj == 0) | (jj == cur) | (jj == cur - 1)
    score = jnp.where(forced, SEL_FORCE, jnp.where(jj <= cur, imp, -SEL_FORCE))
    sub = SUBLANES
    groups = [score[g * sub:(g + 1) * sub, :] for g in range(nblk // sub)]
    ranks = [jnp.zeros((sub, tq), F32) for _ in groups]
    jg = lax.broadcasted_iota(jnp.int32, (sub, tq), 0)
    for i in range(nblk):
        si = score[i:i + 1, :]
        for g, sg in enumerate(groups):
            if g * sub > i:
                ahead = jnp.where(si >= sg, 1.0, 0.0)
            elif (g + 1) * sub <= i:
                ahead = jnp.where(si > sg, 1.0, 0.0)
            else:
                ahead = jnp.where(jg + g * sub > i, jnp.where(si >= sg, 1.0, 0.0),
                                  jnp.where(si > sg, 1.0, 0.0))
            ranks[g] = ranks[g] + ahead
    rank = jnp.concatenate(ranks, axis=0)
    dropped = jnp.where(rank < SEL_TOPK, 0.0, 1.0)
    pieces = [jnp.zeros((HEAD_DIM, tq), F32), dropped]
    if nblk < LANE - HEAD_DIM:
        pieces.append(jnp.zeros((LANE - HEAD_DIM - nblk, tq), F32))
    sel_ref[0, 0] = jnp.concatenate(pieces, axis=0).T.astype(BF16)


def _cmp_attn(z, kc4, vc, bias_c, ovl_t, *, tq, seq):
    b, hkv, _, ncp = kc4.shape
    m = z.shape[0]
    nq = seq // tq
    nsel = seq // SEL_LEN
    kern = functools.partial(_cmp_attn_kernel, tq=tq, ncp=ncp, nsel=nsel, nq=nq)
    qblk = ZB_NQ * LANE // GROUP_WIDTH
    return pl.pallas_call(
        kern,
        grid=(hkv, nq, b),
        in_specs=[
            pl.BlockSpec((tq, GROUP_WIDTH), lambda h, i, bb: (bb * nq + i, qblk + h)),
            pl.BlockSpec((1, 1, LANE, ncp), lambda h, i, bb: (bb, h, 0, 0)),
            pl.BlockSpec((1, 1, ncp, LANE), lambda h, i, bb: (bb, h, 0, 0)),
            pl.BlockSpec((1, GROUP, tq, ncp), lambda h, i, bb: (h, 0, i, 0)),
            pl.BlockSpec((nsel, ncp), lambda h, i, bb: (0, 0)),
        ],
        out_specs=[
            pl.BlockSpec((tq, GROUP_WIDTH), lambda h, i, bb: (bb * nq + i, h)),
            pl.BlockSpec((1, 1, tq, LANE), lambda h, i, bb: (bb, h, i, 0)),
        ],
        out_shape=[
            jax.ShapeDtypeStruct((m, BRANCH_WIDTH), BF16),
            jax.ShapeDtypeStruct((b, hkv, seq, LANE), BF16),
        ],
        compiler_params=_cparams(("parallel", "parallel", "parallel")),
        name="cmp_attn",
    )(z, kc4, vc, bias_c, ovl_t)


def _band_attn_kernel(*refs, tq, w, back, n_off, has_sink):
    if has_sink:
        q_ref, kv_ref, bias_ref, sink_ref, o_ref, kt_sc, v_sc = refs
    else:
        q_ref, kv_ref, bias_ref, o_ref, kt_sc, v_sc = refs
    qi = pl.program_id(2)

    @pl.when(qi == 0)
    def _():
        _split_kv(kv_ref, kt_sc, v_sc)

    off = jnp.minimum(qi * tq, back)
    start = pl.multiple_of(qi * tq - off, LANE)
    q = _stack_heads(q_ref[...], jnp.zeros((tq, LANE), F32))
    logits = _group_logits(q, kt_sc[:, pl.ds(start, w)], tq)
    v1 = v_sc[pl.ds(start, w), :]
    tile = jnp.minimum(qi, n_off - 1)
    sum_lane = lax.broadcasted_iota(jnp.int32, (tq, LANE), 1) >= HEAD_DIM
    accs = []
    for h in range(GROUP):
        s = logits[h] + bias_ref[0, tile, h]
        m = jnp.broadcast_to(jnp.max(s, axis=-1, keepdims=True), (tq, LANE))
        if has_sink:
            sink = sink_ref[0, h]
            m = jnp.maximum(m, sink)
        p = jnp.exp2(s - jnp.concatenate([m] * (w // LANE), axis=1))
        acc = _dot(p.astype(BF16), v1)
        if has_sink:
            acc = acc + jnp.where(sum_lane, jnp.exp2(sink - m), 0.0)
        accs.append(acc)
    o_ref[...] = _finish_group(accs).astype(BF16)


def _sel_attn_kernel(q_ref, kv_ref, drop_ref, rows_ref, bias_ref, o_ref, kt_sc, v_sc, m_sc, acc_sc,
                     *, tq, tk, n_bias):
    qi = pl.program_id(2)

    @pl.when(qi == 0)
    def _():
        _split_kv(kv_ref, kt_sc, v_sc, rows_ref[...])

    q = _stack_heads(q_ref[...], drop_ref[0, 0].astype(F32))
    hi = ((qi + 1) * tq + tk - 1) // tk
    m_sc[...] = jnp.full(m_sc.shape, NEG_INF, F32)
    acc_sc[...] = jnp.zeros(acc_sc.shape, F32)

    def step(kj):
        start = pl.multiple_of(kj * tk, tk)
        logits = _group_logits(q, kt_sc[:, pl.ds(start, tk)], tq)
        v1 = v_sc[pl.ds(start, tk), :]
        u = jnp.minimum((qi * tq - kj * tk) // tq, n_bias - 1)
        for h in range(GROUP):
            s = logits[h] + bias_ref[0, u, h]
            _online_update(s, v1, m_sc, acc_sc, h)

    odd = hi % 2

    @pl.when(odd == 1)
    def _():
        step(hi - 1)

    def body(i, carry):
        kj = hi - odd - 1 - 2 * i
        step(kj)
        step(kj - 1)
        return carry

    lax.fori_loop(0, hi // 2, body, 0)
    o_ref[...] = _finish_group([acc_sc[h] for h in range(GROUP)]).astype(BF16)


def _gqa_specs(z, q_block, kv_block, tq, seq):
    nq = seq // tq
    qblk = q_block * LANE // GROUP_WIDTH
    return [
        pl.BlockSpec((tq, GROUP_WIDTH), lambda h, bb, i: (bb * nq + i, qblk + h)),
        pl.BlockSpec((seq, LANE), lambda h, bb, i: (bb, kv_block + h)),
    ]


def _resident(a):
    return pl.BlockSpec((1,) + a.shape[1:], lambda h, bb, i: (h,) + (0,) * (a.ndim - 1),
                        pipeline_mode=pl.Buffered(1))


def _gqa_out(m, tq, seq):
    nq = seq // tq
    return (pl.BlockSpec((tq, GROUP_WIDTH), lambda h, bb, i: (bb * nq + i, h)),
            jax.ShapeDtypeStruct((m, BRANCH_WIDTH), BF16))


def _kv_scratch(seq, key_rows):
    return [pltpu.VMEM((key_rows, seq), BF16), pltpu.VMEM((seq, LANE), BF16)]


def _band_attention(z, bias, *, q_block, kv_block, tq, seq, back, name, sinks=None):
    m = z.shape[0]
    b = m // seq
    n_off, w = bias.shape[1], bias.shape[4]
    assert w == back + tq and w <= seq and back % LANE == 0
    kern = functools.partial(_band_attn_kernel, tq=tq, w=w, back=back, n_off=n_off,
                             has_sink=sinks is not None)
    in_specs = _gqa_specs(z, q_block, kv_block, tq, seq) + [_resident(bias)]
    args = [z, z, bias]
    if sinks is not None:
        in_specs.append(pl.BlockSpec((1, GROUP, 1, 1), lambda h, bb, i: (h, 0, 0, 0)))
        args.append(sinks)
    out_spec, out_shape = _gqa_out(m, tq, seq)
    return pl.pallas_call(
        kern,
        grid=(N_KV, b, seq // tq),
        in_specs=in_specs,
        out_specs=out_spec,
        out_shape=out_shape,
        scratch_shapes=_kv_scratch(seq, LANE),
        compiler_params=_cparams(("parallel", "parallel", "arbitrary")),
        name=name,
    )(*args)


def _sel_attention(z, dropped, sel_rows, bias, *, tq, tk, seq):
    m = z.shape[0]
    b = m // seq
    kern = functools.partial(_sel_attn_kernel, tq=tq, tk=tk, n_bias=bias.shape[1])
    in_specs = _gqa_specs(z, ZB_NQ, ZB_SEL_KV, tq, seq) + [
        pl.BlockSpec((1, 1, tq, LANE), lambda h, bb, i: (bb, h, i, 0)),
        pl.BlockSpec((HEAD_DIM, seq), lambda h, bb, i: (0, 0), pipeline_mode=pl.Buffered(1)),
        _resident(bias),
    ]
    out_spec, out_shape = _gqa_out(m, tq, seq)
    return pl.pallas_call(
        kern,
        grid=(N_KV, b, seq // tq),
        in_specs=in_specs,
        out_specs=out_spec,
        out_shape=out_shape,
        scratch_shapes=_kv_scratch(seq, LANE) + [pltpu.VMEM((GROUP, tq, LANE), F32),
                                                 pltpu.VMEM((GROUP, tq, LANE), F32)],
        compiler_params=_cparams(("parallel", "parallel", "arbitrary")),
        name="attn_sel",
    )(z, z, dropped, sel_rows, bias)


def _rope_lanes(x, cos_t, sin_a, sin_b):
    half = MLA_ROPE // 2
    return x * cos_t + pltpu.roll(x, LANE - half, 1) * sin_a + pltpu.roll(x, half, 1) * sin_b


def _mla_proj_kernel(mq_ref, mkv_ref, kr_ref, cos_ref, sina_ref, sinb_ref, qn_ref, wq_ref,
                     kvn_ref, wkt_ref, wv_ref, oq_ref, okt_ref, ov_ref):
    cos_t, sin_a, sin_b = cos_ref[...], sina_ref[...], sinb_ref[...]
    ql = _rms(mq_ref[...].astype(F32), qn_ref[...]).astype(BF16)
    q = _dot(ql, wq_ref[...]) * (MLA_QK ** -0.5 * LOG2E)
    kvl = _rms(mkv_ref[...].astype(F32), kvn_ref[...]).astype(BF16)
    kt = _dot_nt(wkt_ref[...], kvl)
    v = _dot(kvl, wv_ref[...])
    kr_t = _rope_lanes(kr_ref[...].astype(F32), cos_t, sin_a, sin_b).T
    lane = lax.broadcasted_iota(jnp.int32, (v.shape[0], LANE), 1)
    for h in range(N_HEADS):
        blk = slice(h * LANE, (h + 1) * LANE)
        oq_ref[:, blk] = _rope_lanes(q[:, blk], cos_t, sin_a, sin_b).astype(BF16)
        okt_ref[0, blk, :] = (kt[blk, :] + kr_t).astype(BF16)
        ov_ref[:, blk] = jnp.where(lane < MLA_V, v[:, blk], 1.0).astype(BF16)


def _mla_proj(z, cos_t, sin_a, sin_b, q_norm, w_qb, kv_norm, w_kt, w_v, *, tm, seq):
    m = z.shape[0]
    nt = seq // tm
    width = N_HEADS * LANE

    def full(a):
        return pl.BlockSpec(a.shape, lambda i: (0,) * a.ndim)

    tab = pl.BlockSpec((tm, LANE), lambda i: (i % nt, 0))
    out = jax.ShapeDtypeStruct((m, width), BF16)
    oblk = pl.BlockSpec((tm, width), lambda i: (i, 0))
    return pl.pallas_call(
        _mla_proj_kernel,
        grid=(m // tm,),
        in_specs=[
            pl.BlockSpec((tm, MLA_Q_RANK), lambda i: (i, ZB_MQ * LANE // MLA_Q_RANK)),
            pl.BlockSpec((tm, MLA_KV_RANK), lambda i: (i, ZB_MKV * LANE // MLA_KV_RANK)),
            pl.BlockSpec((tm, LANE), lambda i: (i, ZB_KR)),
            tab, tab, tab,
            full(q_norm), full(w_qb), full(kv_norm), full(w_kt), full(w_v),
        ],
        out_specs=[oblk, pl.BlockSpec((1, width, tm), lambda i: (i // nt, 0, i % nt)), oblk],
        out_shape=[out, jax.ShapeDtypeStruct((m // seq, width, seq), BF16), out],
        compiler_params=_cparams(("parallel",)),
        name="mla_proj",
    )(z, z, z, cos_t, sin_a, sin_b, q_norm, w_qb, kv_norm, w_kt, w_v)


def _mla_attn_kernel(q_ref, k_ref, v_ref, o_ref, m_sc, acc_sc, *, tq):
    qi = pl.program_id(2)
    n_chain = m_sc.shape[0]
    qs = [q_ref[:, c * LANE:(c + 1) * LANE] for c in range(n_chain)]
    m_sc[...] = jnp.full(m_sc.shape, NEG_INF, F32)
    acc_sc[...] = jnp.zeros(acc_sc.shape, F32)

    def step(tile, width, diagonal):
        start = pl.multiple_of(tile * tq, tq)
        for c in range(n_chain):
            blk = slice(c * LANE, (c + 1) * LANE)
            s = _dot(qs[c], k_ref[0, blk, pl.ds(start, width * tq)])
            if diagonal:
                row = lax.broadcasted_iota(jnp.int32, s.shape, 0) + (width - 1) * tq
                col = lax.broadcasted_iota(jnp.int32, s.shape, 1)
                s = jnp.where(col <= row, s, NEG_INF)
            _online_update(s, v_ref[pl.ds(start, width * tq), blk], m_sc, acc_sc, c)

    odd = qi % 2

    @pl.when(odd == 0)
    def _():
        step(qi, 1, True)

    @pl.when(odd == 1)
    def _():
        step(qi - 1, 2, True)

    def body(i, carry):
        step(qi - odd - 2 * (i + 1), 2, False)
        return carry

    lax.fori_loop(0, qi // 2, body, 0)
    for c in range(n_chain):
        acc = acc_sc[c]
        lane = lax.broadcasted_iota(jnp.int32, acc.shape, 1)
        den = jnp.where(lane < MLA_V, pltpu.roll(acc, MLA_V, 1), acc)
        o_ref[:, c * LANE:(c + 1) * LANE] = (acc / den).astype(BF16)


def _mla_attention(q, k, v1, *, tq, seq):
    m = q.shape[0]
    b = m // seq
    nq = seq // tq
    width = MLA_CHAINS * LANE
    slab = pl.BlockSpec((seq, width), lambda h, bb, i: (bb, h))
    slab_t = pl.BlockSpec((1, width, seq), lambda h, bb, i: (bb, h, 0))
    tile = pl.BlockSpec((tq, width), lambda h, bb, i: (bb * nq + i, h))
    return pl.pallas_call(
        functools.partial(_mla_attn_kernel, tq=tq),
        grid=(N_HEADS // MLA_CHAINS, b, nq),
        in_specs=[tile, slab_t, slab],
        out_specs=tile,
        out_shape=jax.ShapeDtypeStruct(q.shape, BF16),
        scratch_shapes=[pltpu.VMEM((MLA_CHAINS, tq, LANE), F32),
                        pltpu.VMEM((MLA_CHAINS, tq, LANE), F32)],
        compiler_params=_cparams(("parallel", "parallel", "arbitrary")),
        name="attn_mla",
    )(q, k, v1)


def _merge_kernel(ocmp_ref, osel_ref, owin_ref, ob_ref, oc_ref, ng_ref, mg_ref, x_ref,
                  gexp_ref, wab_ref, wc_ref, wo_ref, gpost_ref, o_ref):
    ng = jax.nn.sigmoid(ng_ref[...].astype(F32))
    ng_hi = ng.astype(BF16)
    ng_lo = (ng - ng_hi.astype(F32)).astype(BF16)
    gates = _dot(jnp.concatenate([ng_hi, ng_lo], axis=1), gexp_ref[...])
    o_a = jnp.zeros(ocmp_ref.shape, F32)
    for n, ref in enumerate((ocmp_ref, osel_ref, owin_ref)):
        gate = gates[:, n * BRANCH_WIDTH:(n + 1) * BRANCH_WIDTH]
        o_a = o_a + gate * ref[...].astype(F32)
    ys = (_dot(o_a.astype(BF16), wab_ref[0]), _dot(ob_ref[...], wab_ref[1]),
          _dot(oc_ref[...], wc_ref[...]))
    mixed = jnp.zeros(x_ref.shape, F32)
    for n, y in enumerate(ys):
        gate = jax.nn.sigmoid(mg_ref[:, n * D_MODEL:(n + 1) * D_MODEL].astype(F32))
        mixed = mixed + gate * y
    out = _dot(mixed.astype(BF16), wo_ref[...])
    o_ref[...] = x_ref[...] + _rms(out, gpost_ref[...])


def _merge(o_cmp, o_sel, o_win, o_b, o_c, z, x, gexp, w_ab, w_c, w_out, g_post, *, tm):
    m, d = x.shape
    br = pl.BlockSpec((tm, o_cmp.shape[1]), lambda i: (i, 0))

    def full(a):
        return pl.BlockSpec(a.shape, lambda i: (0,) * a.ndim)

    return pl.pallas_call(
        _merge_kernel,
        grid=(m // tm,),
        in_specs=[
            br, br, br, br,
            pl.BlockSpec((tm, o_c.shape[1]), lambda i: (i, 0)),
            pl.BlockSpec((tm, LANE), lambda i: (i, ZB_NGATE)),
            pl.BlockSpec((tm, 3 * d), lambda i: (i, ZB_MGATE * LANE // (3 * d))),
            pl.BlockSpec((tm, d), lambda i: (i, 0)),
            full(gexp), full(w_ab), full(w_c), full(w_out), full(g_post),
        ],
        out_specs=pl.BlockSpec((tm, d), lambda i: (i, 0)),
        out_shape=jax.ShapeDtypeStruct((m, d), F32),
        compiler_params=_cparams(("parallel",)),
        name="merge",
    )(o_cmp, o_sel, o_win, o_b, o_c, z, z, x, gexp, w_ab, w_c, w_out, g_post)


def _t5_bucket(dist):
    max_exact = NUM_BUCKETS // 2
    d = jnp.maximum(dist, 0)
    df = jnp.maximum(d, 1).astype(F32)
    large = max_exact + (jnp.log(df / max_exact) / math.log(MAX_DISTANCE / max_exact)
                         * (NUM_BUCKETS - max_exact)).astype(jnp.int32)
    large = jnp.minimum(large, NUM_BUCKETS - 1)
    return jnp.where(d < max_exact, d, large)


def _toeplitz(fn, offs, tq, tk):
    ln = tq + tk - 1
    i = jnp.arange(ln + 1)
    shift = jnp.where(i < tk, -i, ln + 1 - i)
    ext = jnp.moveaxis(fn(jnp.asarray(offs)[:, None] + shift[None, :]), -1, 0)
    flat = jnp.tile(ext, (1, 1, tq))[:, :, :tq * ln]
    return flat.reshape(ext.shape[0], len(offs), tq, ln)[:, :, :, :tk]


def _bias_fn(table, window):
    def fn(dist):
        ok = dist >= 0
        if window is not None:
            ok = ok & (dist < window)
        return jnp.where(ok[..., None], table[_t5_bucket(dist)].astype(F32), NEG_INF)
    return fn


def _head_tiles(t):
    return t.reshape(N_KV, GROUP, *t.shape[1:]).transpose(0, 2, 1, 3, 4)


def _in_proj_weight(w_in):
    sizes = (512, 128, 128, 128, 128, 128, 128, 24, 512, 128, 128, MLA_Q_RANK, MLA_KV_RANK,
             MLA_ROPE, 3 * D_MODEL)
    offs = np.concatenate([[0], np.cumsum(sizes)])
    seg = [w_in[:, offs[i]:offs[i + 1]] for i in range(len(sizes))]
    (nq, nkc, nvc, nks, nvs, nkw, nvw, ngate, sq, sk, sv, mq, mkv, mkr, mgate) = seg
    d = w_in.shape[0]
    hd = HEAD_DIM

    def pad(a, left=0):
        return jnp.pad(a, ((0, 0), (left, LANE - left - a.shape[1])))

    def kv_blocks(k, v):
        return [jnp.concatenate([k[:, h * hd:(h + 1) * hd], v[:, h * hd:(h + 1) * hd]], axis=1)
                for h in range(N_KV)]

    cols = ([mq, pad(ngate), mkv, pad(mkr, MLA_NOPE), jnp.zeros((d, LANE), w_in.dtype),
             nq * QK_SCALE, nkc, nvc] + kv_blocks(nks, nvs) + kv_blocks(nkw, nvw)
            + [sq * QK_SCALE] + kv_blocks(sk, sv) + [mgate])
    w = jnp.concatenate(cols, axis=1)
    assert w.shape == (d, Z_WIDTH)
    return w.astype(BF16)


def _mla_weights(w_qb, w_kvb, w_c):
    def blocks(a):
        return jnp.pad(a, ((0, 0), (0, 0), (0, LANE - a.shape[2]))).reshape(a.shape[0], -1)

    wq = blocks(w_qb.reshape(MLA_Q_RANK, N_HEADS, MLA_QK))
    wkv = w_kvb.reshape(MLA_KV_RANK, N_HEADS, MLA_NOPE + MLA_V)
    wkt = blocks(wkv[:, :, :MLA_NOPE]).T
    wv = blocks(wkv[:, :, MLA_NOPE:])
    wc = jnp.pad(w_c.reshape(N_HEADS, MLA_V, -1), ((0, 0), (0, LANE - MLA_V), (0, 0)))
    return (wq.astype(BF16), wkt.astype(BF16), wv.astype(BF16),
            wc.reshape(N_HEADS * LANE, -1).astype(BF16))


def _gate_expand():
    e = np.zeros((2 * LANE, 3 * BRANCH_WIDTH), np.float32)
    for n in range(3):
        for h in range(N_HEADS):
            cols = slice(n * BRANCH_WIDTH + h * HEAD_DIM, n * BRANCH_WIDTH + (h + 1) * HEAD_DIM)
            e[h * 3 + n, cols] = 1.0
            e[LANE + h * 3 + n, cols] = 1.0
    return jnp.asarray(e, BF16)


def _rope_tables(pos):
    half = MLA_ROPE // 2
    inv_freq = ROPE_BASE ** (-jnp.arange(half, dtype=F32) / half)
    ang = pos.astype(F32)[:, None] * inv_freq[None, :]
    cos, sin = jnp.cos(ang), jnp.sin(ang)
    n = pos.shape[0]
    ones = jnp.ones((n, MLA_NOPE), F32)
    z16 = jnp.zeros((n, half), F32)
    tail = jnp.zeros((n, LANE - MLA_QK), F32)
    cos_t = jnp.concatenate([ones, cos, cos, tail], axis=1)
    sin_a = jnp.concatenate([0 * ones, -sin, z16, tail], axis=1)
    sin_b = jnp.concatenate([0 * ones, z16, sin, tail], axis=1)
    return cos_t, sin_a, sin_b


def kernel(x, rel_bias_table, ffn1_norm_pre, ffn1_w_gu, ffn1_w_down, ffn1_norm_post,
           mix_norm_pre, w_in, nsa_pe_k, nsa_w1_k, nsa_w2_k, nsa_pe_v, nsa_w1_v, nsa_w2_v,
           swa_sinks, mla_q_norm, mla_w_qb, mla_kv_norm, mla_w_kvb, w_branch, w_out,
           mix_norm_post, ffn2_norm_pre, ffn2_w_gu, ffn2_w_down, ffn2_norm_post):
    b, s, d = x.shape
    depth = w_in.shape[0]
    m = b * s
    tq = 256
    tq_band = min(256, s // 2)
    tq_sel = min(512, s)
    tk_sel = min(512, s)
    tq_mla = min(512, s)
    ncp = s // CMP_STRIDE
    nsel = s // SEL_LEN
    tm = min(512, m)
    tm_ffn = min(512 * FFN_CHAINS, m)
    tm_big = min(1024, s)

    pos = jnp.arange(s, dtype=jnp.int32)
    tab_a = rel_bias_table[:, :N_HEADS] * LOG2E
    tab_b = rel_bias_table[:, N_HEADS:] * LOG2E
    n_far = -(-(LAST_BUCKET_DIST + tk_sel - 1) // tq_sel)
    n_sel_tiles = min(n_far + 1, s // tq_sel)
    bias_sel = _head_tiles(_toeplitz(_bias_fn(tab_a, None),
                                     [u * tq_sel for u in range(n_sel_tiles)], tq_sel, tk_sel))
    tb = tq_band
    win_offs = sorted({min(i * tb, NSA_WINDOW) for i in range(-(-NSA_WINDOW // tb) + 1)})
    bias_win = _head_tiles(_toeplitz(_bias_fn(tab_a, NSA_WINDOW), win_offs, tb, NSA_WINDOW + tb))
    swa_offs = sorted({min(i * tb, SWA_WINDOW) for i in range(-(-SWA_WINDOW // tb) + 1)})
    bias_swa = _head_tiles(_toeplitz(_bias_fn(tab_b, SWA_WINDOW), swa_offs, tb, SWA_WINDOW + tb))

    def cmp_bias_fn(e):
        r = jnp.arange(CMP_STRIDE)
        dist = e[..., None] * CMP_STRIDE + r - (CMP_LEN - 1)
        return tab_a[_t5_bucket(dist)].astype(F32).reshape(*e.shape, CMP_STRIDE * N_HEADS)

    bias_c = _toeplitz(cmp_bias_fn, [0], ncp, ncp).reshape(CMP_STRIDE, N_HEADS, ncp, ncp)
    bias_c = bias_c.transpose(1, 2, 0, 3).reshape(N_KV, GROUP, s, ncp)
    ci = jnp.arange(ncp)[None, :] * CMP_STRIDE
    sj = jnp.arange(nsel)[:, None] * SEL_LEN
    ovl_t = (jnp.maximum(jnp.minimum(ci + CMP_LEN, sj + SEL_LEN) - jnp.maximum(ci, sj), 0)
             .astype(F32) / CMP_LEN)
    cos_t, sin_a, sin_b = _rope_tables(pos)
    gexp = _gate_expand()
    assert nsel <= HEAD_DIM
    sel_rows = jnp.where(jnp.arange(HEAD_DIM)[:, None] == (pos[None, :] >> SEL_SHIFT),
                         NEG_INF, 0.0).astype(BF16)
    chunk = CMP_STRIDE * HEAD_DIM

    xf = x.reshape(m, d)
    for l in range(depth):
        xf = _ffn(xf, ffn1_norm_pre[l][None], ffn1_w_gu[l].astype(BF16),
                  ffn1_w_down[l].astype(BF16), ffn1_norm_post[l][None], tm=tm_ffn)

        z = _inproj(xf, mix_norm_pre[l][None], _in_proj_weight(w_in[l]), tm=tm_big, tn=Z_WIDTH // 4)

        def chunks(blk):
            a = z[:, blk * LANE:(blk + 1) * LANE].reshape(b, s, N_KV, HEAD_DIM)
            return a.transpose(0, 2, 1, 3).reshape(b, N_KV, ncp, chunk)

        kc, vc = _cmp_mlp(chunks(ZB_CMP_K), chunks(ZB_CMP_V),
                          nsa_pe_k[l].reshape(2, chunk), nsa_w1_k[l].astype(BF16),
                          nsa_w2_k[l].astype(BF16),
                          nsa_pe_v[l].reshape(2, chunk), nsa_w1_v[l].astype(BF16),
                          nsa_w2_v[l].astype(BF16))
        o_cmp, dropped = _cmp_attn(z, jnp.tile(kc.transpose(0, 1, 3, 2), (1, 1, 2, 1)),
                                   jnp.tile(vc, (1, 1, 1, 2)),
                                   bias_c, ovl_t, tq=tq, seq=s)
        o_sel = _sel_attention(z, dropped, sel_rows, bias_sel, tq=tq_sel, tk=tk_sel, seq=s)
        o_win = _band_attention(z, bias_win, q_block=ZB_NQ, kv_block=ZB_WIN_KV, tq=tq_band, seq=s,
                                back=NSA_WINDOW, name="attn_win")
        o_b = _band_attention(z, bias_swa, q_block=ZB_SQ, kv_block=ZB_SWA_KV, tq=tq_band, seq=s,
                              back=SWA_WINDOW, name="attn_swa",
                              sinks=(swa_sinks[l] * LOG2E).reshape(N_KV, GROUP, 1, 1))
        wq, wkt, wv, wc = _mla_weights(mla_w_qb[l], mla_w_kvb[l], w_branch[l, 2])
        mq, mk, mv1 = _mla_proj(z, cos_t, sin_a, sin_b, mla_q_norm[l][None], wq,
                                mla_kv_norm[l][None], wkt, wv, tm=tm_big, seq=s)
        o_c = _mla_attention(mq, mk, mv1, tq=tq_mla, seq=s)

        xf = _merge(o_cmp, o_sel, o_win, o_b, o_c, z, xf, gexp, w_branch[l, :2].astype(BF16), wc,
                    w_out[l].astype(BF16), mix_norm_post[l][None], tm=tm)

        xf = _ffn(xf, ffn2_norm_pre[l][None], ffn2_w_gu[l].astype(BF16),
                  ffn2_w_down[l].astype(BF16), ffn2_norm_post[l][None], tm=tm_ffn)
    return xf.reshape(b, s, d)
```

```python
import functools
import math

import jax
import jax.numpy as jnp
import numpy as np
from jax import lax
from jax.experimental import pallas as pl
from jax.experimental.pallas import tpu as pltpu

F32 = jnp.float32
BF16 = jnp.bfloat16

D_MODEL = 1024
D_FF = 2816
HEAD_DIM = 64
NORM_EPS = 1e-6
NUM_BUCKETS = 32
MAX_DISTANCE = 1024
LAST_BUCKET_DIST = 1 + math.ceil(
    (NUM_BUCKETS // 2) * (MAX_DISTANCE / (NUM_BUCKETS // 2))
    ** ((NUM_BUCKETS // 2 - 1) / (NUM_BUCKETS - NUM_BUCKETS // 2)))
N_HEADS = 8
N_KV = 2
GROUP = N_HEADS // N_KV
CMP_LEN = 32
CMP_STRIDE = 16
CMP_HIDDEN = 128
SEL_LEN = 64
SEL_SHIFT = 6
SEL_TOPK = 16
NSA_WINDOW = 512
SWA_WINDOW = 128
MLA_Q_RANK = 384
MLA_KV_RANK = 256
MLA_NOPE = 64
MLA_ROPE = 32
MLA_V = 64
MLA_QK = MLA_NOPE + MLA_ROPE
ROPE_BASE = 10000.0
BRANCH_WIDTH = N_HEADS * HEAD_DIM
NEG_INF = -1e30
SEL_FORCE = 1e9
LOG2E = math.log2(math.e)
QK_SCALE = HEAD_DIM ** -0.5 * LOG2E

LANE = 128
SUBLANES = 8
VMEM_LIMIT = 56 * 1024 * 1024
GROUP_WIDTH = GROUP * HEAD_DIM
FFN_CHAINS = 2
FFN_CHUNK = 256
CMP_PARTS = 4
QK_SPLIT = 2
KV_SPLIT_CHUNK = 512
MLA_CHAINS = 8

ZB_MQ = 0
ZB_NGATE = 3
ZB_MKV = 4
ZB_KR = 6
ZB_NQ = 8
ZB_CMP_K = 12
ZB_CMP_V = 13
ZB_SEL_KV = 14
ZB_WIN_KV = 16
ZB_SQ = 18
ZB_SWA_KV = 22
ZB_MGATE = 24
Z_BLOCKS = 48
Z_WIDTH = Z_BLOCKS * LANE


def _cparams(sem):
    return pltpu.CompilerParams(dimension_semantics=sem, vmem_limit_bytes=VMEM_LIMIT)


def _rms(x, g):
    inv = lax.rsqrt(jnp.mean(x * x, axis=-1, keepdims=True) + NORM_EPS)
    return (x * inv) * g


def _dot(a, b):
    return jnp.dot(a, b, preferred_element_type=F32)


def _dot_nt(a, b, precision=None):
    return lax.dot_general(a, b, (((1,), (1,)), ((), ())), precision=precision,
                           preferred_element_type=F32)


def _ffn_kernel(x_ref, gpre_ref, wg_ref, wu_ref, wd_ref, gpost_ref, o_ref):
    rows = x_ref.shape[0] // FFN_CHAINS
    for c in range(FFN_CHAINS):
        sl = slice(c * rows, (c + 1) * rows)
        x = x_ref[sl, :]
        h = _rms(x, gpre_ref[...]).astype(BF16)
        y = jnp.zeros(x.shape, F32)
        for j in range(D_FF // FFN_CHUNK):
            cols = slice(j * FFN_CHUNK, (j + 1) * FFN_CHUNK)
            g = _dot(h, wg_ref[:, cols])
            u = _dot(h, wu_ref[:, cols])
            act = (g * jax.nn.sigmoid(g)) * u
            y = y + _dot(act.astype(BF16), wd_ref[cols, :])
        o_ref[sl, :] = x + 0.5 * _rms(y, gpost_ref[...])


def _ffn(x, g_pre, w_gu, w_down, g_post, *, tm):
    m, d = x.shape
    once = pl.Buffered(1)
    return pl.pallas_call(
        _ffn_kernel,
        grid=(m // tm,),
        in_specs=[
            pl.BlockSpec((tm, d), lambda i: (i, 0)),
            pl.BlockSpec((1, d), lambda i: (0, 0)),
            pl.BlockSpec((d, D_FF), lambda i: (0, 0), pipeline_mode=once),
            pl.BlockSpec((d, D_FF), lambda i: (0, 1), pipeline_mode=once),
            pl.BlockSpec((D_FF, d), lambda i: (0, 0), pipeline_mode=once),
            pl.BlockSpec((1, d), lambda i: (0, 0)),
        ],
        out_specs=pl.BlockSpec((tm, d), lambda i: (i, 0)),
        out_shape=jax.ShapeDtypeStruct((m, d), F32),
        compiler_params=_cparams(("parallel",)),
        name="ffn",
    )(x, g_pre, w_gu, w_gu, w_down, g_post)


def _inproj_kernel(x_ref, g_ref, w_ref, o_ref, h_sc):
    @pl.when(pl.program_id(1) == 0)
    def _():
        h_sc[...] = _rms(x_ref[...], g_ref[...]).astype(BF16)

    o_ref[...] = _dot(h_sc[...], w_ref[...]).astype(BF16)


def _inproj(x, g, w, *, tm, tn):
    m, d = x.shape
    n = w.shape[1]
    return pl.pallas_call(
        _inproj_kernel,
        grid=(m // tm, n // tn),
        in_specs=[
            pl.BlockSpec((tm, d), lambda i, j: (i, 0)),
            pl.BlockSpec((1, d), lambda i, j: (0, 0)),
            pl.BlockSpec((d, tn), lambda i, j: (0, j)),
        ],
        out_specs=pl.BlockSpec((tm, tn), lambda i, j: (i, j)),
        out_shape=jax.ShapeDtypeStruct((m, n), BF16),
        scratch_shapes=[pltpu.VMEM((tm, d), BF16)],
        compiler_params=_cparams(("parallel", "arbitrary")),
        name="inproj",
    )(x, g, w)


def _cmp_mlp_kernel(k_ref, v_ref, pek_ref, w1k_ref, w2k_ref, pev_ref, w1v_ref, w2v_ref,
                    kc_ref, vc_ref):
    half = CMP_STRIDE * HEAD_DIM

    def one(x_ref, pe_ref, w1_ref, w2_ref, o_ref):
        x = x_ref[0, 0].astype(F32)
        ncp = x.shape[0]
        xa = (x + pe_ref[0:1, :]).astype(BF16)
        xb = (x + pe_ref[1:2, :]).astype(BF16)
        pa = _dot(xa, w1_ref[0:half, :])
        pb = _dot(xb, w1_ref[half:2 * half, :])
        hid = pa + pltpu.roll(pb, ncp - 1, 0)
        out = _dot(jax.nn.gelu(hid).astype(BF16), w2_ref[...])
        row = lax.broadcasted_iota(jnp.int32, out.shape, 0)
        o_ref[0, 0] = jnp.where(row < ncp - 1, out, 0.0).astype(BF16)

    one(k_ref, pek_ref, w1k_ref, w2k_ref, kc_ref)
    one(v_ref, pev_ref, w1v_ref, w2v_ref, vc_ref)


def _cmp_mlp(k4, v4, pek, w1k, w2k, pev, w1v, w2v):
    b, hkv, ncp, width = k4.shape
    blk = pl.BlockSpec((1, 1, ncp, width), lambda i, j: (i, j, 0, 0))
    oblk = pl.BlockSpec((1, 1, ncp, HEAD_DIM), lambda i, j: (i, j, 0, 0))

    def full(a):
        return pl.BlockSpec(a.shape, lambda i, j: (0,) * a.ndim)

    out = jax.ShapeDtypeStruct((b, hkv, ncp, HEAD_DIM), BF16)
    return pl.pallas_call(
        _cmp_mlp_kernel,
        grid=(b, hkv),
        in_specs=[blk, blk, full(pek), full(w1k), full(w2k), full(pev), full(w1v), full(w2v)],
        out_specs=[oblk, oblk],
        out_shape=[out, out],
        compiler_params=_cparams(("parallel", "parallel")),
        name="cmp_mlp",
    )(k4, v4, pek, w1k, w2k, pev, w1v, w2v)


def _stack_heads(q_all, upper):
    q32 = q_all.astype(F32)
    lane = lax.broadcasted_iota(jnp.int32, upper.shape, 1)
    rows = []
    for pair in range(GROUP // 2):
        half = q32[:, pair * LANE:(pair + 1) * LANE]
        rows.append(jnp.where(lane < HEAD_DIM, half, upper))
        rows.append(jnp.where(lane < HEAD_DIM, pltpu.roll(half, HEAD_DIM, 1), upper))
    return jnp.concatenate(rows, axis=0).astype(BF16)


def _split_kv(kv_ref, kt_sc, v_sc, key_rows=None):
    seq = kv_ref.shape[0]
    chunk = min(KV_SPLIT_CHUNK, seq)
    copies = kt_sc.shape[0] // HEAD_DIM - (key_rows is not None)
    lane = lax.broadcasted_iota(jnp.int32, (chunk, LANE), 1)
    for c in range(seq // chunk):
        rows = slice(c * chunk, (c + 1) * chunk)
        x = kv_ref[rows, :].astype(F32)
        kt = x.T[0:HEAD_DIM, :].astype(BF16)
        for r in range(copies):
            kt_sc[r * HEAD_DIM:(r + 1) * HEAD_DIM, rows] = kt
        v_sc[rows, :] = jnp.where(lane < HEAD_DIM, pltpu.roll(x, HEAD_DIM, 1), 1.0).astype(BF16)
    if key_rows is not None:
        kt_sc[copies * HEAD_DIM:(copies + 1) * HEAD_DIM, :] = key_rows


def _group_logits(q_stack, kt, tq):
    per = GROUP // QK_SPLIT
    parts = [_dot(q_stack[p * per * tq:(p + 1) * per * tq], kt) for p in range(QK_SPLIT)]
    return [parts[h // per][(h % per) * tq:(h % per + 1) * tq] for h in range(GROUP)]


def _finish_pair_rolled(acc_even, acc_odd):
    lane = lax.broadcasted_iota(jnp.int32, acc_even.shape, 1)
    even = acc_even / pltpu.roll(acc_even, HEAD_DIM, 1)
    odd = pltpu.roll(acc_odd, HEAD_DIM, 1) / acc_odd
    return jnp.where(lane < HEAD_DIM, even, odd)


def _finish_group(accs):
    return jnp.concatenate([_finish_pair_rolled(accs[0], accs[1]),
                            _finish_pair_rolled(accs[2], accs[3])], axis=1)


def _online_update(s, v, m_ref, acc_ref, idx):
    m_old = m_ref[idx]
    m_new = jnp.maximum(m_old, jnp.max(s, axis=-1, keepdims=True))
    alpha = jnp.exp2(m_old - m_new)
    p = jnp.exp2(s - jnp.concatenate([m_new] * (s.shape[1] // LANE), axis=1))
    acc = acc_ref[idx]
    acc_ref[idx] = (jnp.concatenate([alpha] * (acc.shape[1] // LANE), axis=1) * acc
                    + _dot(p.astype(BF16), v))
    m_ref[idx] = m_new


def _cmp_attn_kernel(q_ref, kc_ref, vc_ref, bias_ref, ovl_ref, o_ref, sel_ref, *, tq, ncp, nsel, nq):
    qi = pl.program_id(1)
    parts = CMP_PARTS if nq % CMP_PARTS == 0 and nsel % (CMP_PARTS * SUBLANES) == 0 else 1
    for k in range(parts):
        ncols = min(ncp, -(-((k + 1) * ncp // parts) // LANE) * LANE)
        nblk = (k + 1) * nsel // parts
        pl.when(qi // (nq // parts) == k)(functools.partial(
            _cmp_attn_body, q_ref, kc_ref, vc_ref, bias_ref, ovl_ref, o_ref, sel_ref,
            tq=tq, ncols=ncols, nblk=nblk))


def _cmp_attn_body(q_ref, kc_ref, vc_ref, bias_ref, ovl_ref, o_ref, sel_ref, *, tq, ncols, nblk):
    qi = pl.program_id(1)
    vc = vc_ref[0, 0, 0:ncols, :]
    lane = lax.broadcasted_iota(jnp.int32, (tq, LANE), 1)
    row = lax.broadcasted_iota(jnp.int32, (tq, ncols), 0) + qi * tq
    col = lax.broadcasted_iota(jnp.int32, (tq, ncols), 1)
    valid = (col * CMP_STRIDE + (CMP_LEN - 1)) <= row
    p_sum = jnp.zeros((tq, ncols), F32)
    outs = []
    q = _stack_heads(q_ref[...], jnp.zeros((tq, LANE), F32))
    logits = _group_logits(q, kc_ref[0, 0, :, 0:ncols], tq)
    for g in range(GROUP):
        s = logits[g] + bias_ref[0, g, :, 0:ncols]
        s = jnp.where(valid, s, NEG_INF)
        e = jnp.where(valid, jnp.exp2(s - jnp.max(s, axis=-1, keepdims=True)), 0.0)
        den = jnp.sum(e, axis=-1, keepdims=True)
        p = e / jnp.where(den > 0, den, 1.0)
        outs.append(_dot(p.astype(BF16), vc))
        p_sum = p_sum + p
    o_ref[...] = jnp.concatenate([jnp.where(lane < HEAD_DIM, outs[0], outs[1]),
                                  jnp.where(lane < HEAD_DIM, outs[2], outs[3])],
                                 axis=1).astype(BF16)

    imp = _dot_nt(ovl_ref[0:nblk, 0:ncols], p_sum, precision=lax.Precision.HIGHEST)
    jj = lax.broadcasted_iota(jnp.int32, (nblk, tq), 0)
    cur = (lax.broadcasted_iota(jnp.int32, (nblk, tq), 1) + qi * tq) >> SEL_SHIFT
    forced = (jj == 0) | (jj == cur) | (jj == cur - 1)
    score = jnp.where(forced, SEL_FORCE, jnp.where(jj <= cur, imp, -SEL_FORCE))
    sub = SUBLANES
    groups = [score[g * sub:(g + 1) * sub, :] for g in range(nblk // sub)]
    ranks = [jnp.zeros((sub, tq), F32) for _ in groups]
    jg = lax.broadcasted_iota(jnp.int32, (sub, tq), 0)
    for i in range(nblk):
        si = score[i:i + 1, :]
        for g, sg in enumerate(groups):
            if g * sub > i:
                ahead = jnp.where(si >= sg, 1.0, 0.0)
            elif (g + 1) * sub <= i:
                ahead = jnp.where(si > sg, 1.0, 0.0)
            else:
                ahead = jnp.where(jg + g * sub > i, jnp.where(si >= sg, 1.0, 0.0),
                                  jnp.where(si > sg, 1.0, 0.0))
            ranks[g] = ranks[g] + ahead
    rank = jnp.concatenate(ranks, axis=0)
    dropped = jnp.where(rank < SEL_TOPK, 0.0, 1.0)
    pieces = [jnp.zeros((HEAD_DIM, tq), F32), dropped]
    if nblk < LANE - HEAD_DIM:
        pieces.append(jnp.zeros((LANE - HEAD_DIM - nblk, tq), F32))
    sel_ref[0, 0] = jnp.concatenate(pieces, axis=0).T.astype(BF16)


def _cmp_attn(z, kc4, vc, bias_c, ovl_t, *, tq, seq):
    b, hkv, _, ncp = kc4.shape
    m = z.shape[0]
    nq = seq // tq
    nsel = seq // SEL_LEN
    kern = functools.partial(_cmp_attn_kernel, tq=tq, ncp=ncp, nsel=nsel, nq=nq)
    qblk = ZB_NQ * LANE // GROUP_WIDTH
    return pl.pallas_call(
        kern,
        grid=(hkv, nq, b),
        in_specs=[
            pl.BlockSpec((tq, GROUP_WIDTH), lambda h, i, bb: (bb * nq + i, qblk + h)),
            pl.BlockSpec((1, 1, LANE, ncp), lambda h, i, bb: (bb, h, 0, 0)),
            pl.BlockSpec((1, 1, ncp, LANE), lambda h, i, bb: (bb, h, 0, 0)),
            pl.BlockSpec((1, GROUP, tq, ncp), lambda h, i, bb: (h, 0, i, 0)),
            pl.BlockSpec((nsel, ncp), lambda h, i, bb: (0, 0)),
        ],
        out_specs=[
            pl.BlockSpec((tq, GROUP_WIDTH), lambda h, i, bb: (bb * nq + i, h)),
            pl.BlockSpec((1, 1, tq, LANE), lambda h, i, bb: (bb, h, i, 0)),
        ],
        out_shape=[
            jax.ShapeDtypeStruct((m, BRANCH_WIDTH), BF16),
            jax.ShapeDtypeStruct((b, hkv, seq, LANE), BF16),
        ],
        compiler_params=_cparams(("parallel", "parallel", "parallel")),
        name="cmp_attn",
    )(z, kc4, vc, bias_c, ovl_t)


def _band_attn_kernel(*refs, tq, w, back, n_off, has_sink):
    if has_sink:
        q_ref, kv_ref, bias_ref, sink_ref, o_ref, kt_sc, v_sc = refs
    else:
        q_ref, kv_ref, bias_ref, o_ref, kt_sc, v_sc = refs
    qi = pl.program_id(2)

    @pl.when(qi == 0)
    def _():
        _split_kv(kv_ref, kt_sc, v_sc)

    off = jnp.minimum(qi * tq, back)
    start = pl.multiple_of(qi * tq - off, LANE)
    q = _stack_heads(q_ref[...], jnp.zeros((tq, LANE), F32))
    logits = _group_logits(q, kt_sc[:, pl.ds(start, w)], tq)
    v1 = v_sc[pl.ds(start, w), :]
    tile = jnp.minimum(qi, n_off - 1)
    sum_lane = lax.broadcasted_iota(jnp.int32, (tq, LANE), 1) >= HEAD_DIM
    accs = []
    for h in range(GROUP):
        s = logits[h] + bias_ref[0, tile, h]
        m = jnp.broadcast_to(jnp.max(s, axis=-1, keepdims=True), (tq, LANE))
        if has_sink:
            sink = sink_ref[0, h]
            m = jnp.maximum(m, sink)
        p = jnp.exp2(s - jnp.concatenate([m] * (w // LANE), axis=1))
        acc = _dot(p.astype(BF16), v1)
        if has_sink:
            acc = acc + jnp.where(sum_lane, jnp.exp2(sink - m), 0.0)
        accs.append(acc)
    o_ref[...] = _finish_group(accs).astype(BF16)


def _sel_attn_kernel(q_ref, kv_ref, drop_ref, rows_ref, bias_ref, o_ref, kt_sc, v_sc, m_sc, acc_sc,
                     *, tq, tk, n_bias):
    qi = pl.program_id(2)

    @pl.when(qi == 0)
    def _():
        _split_kv(kv_ref, kt_sc, v_sc, rows_ref[...])

    q = _stack_heads(q_ref[...], drop_ref[0, 0].astype(F32))
    hi = ((qi + 1) * tq + tk - 1) // tk
    m_sc[...] = jnp.full(m_sc.shape, NEG_INF, F32)
    acc_sc[...] = jnp.zeros(acc_sc.shape, F32)

    def step(kj):
        start = pl.multiple_of(kj * tk, tk)
        logits = _group_logits(q, kt_sc[:, pl.ds(start, tk)], tq)
        v1 = v_sc[pl.ds(start, tk), :]
        u = jnp.minimum((qi * tq - kj * tk) // tq, n_bias - 1)
        for h in range(GROUP):
            s = logits[h] + bias_ref[0, u, h]
            _online_update(s, v1, m_sc, acc_sc, h)

    odd = hi % 2

    @pl.when(odd == 1)
    def _():
        step(hi - 1)

    def body(i, carry):
        kj = hi - odd - 1 - 2 * i
        step(kj)
        step(kj - 1)
        return carry

    lax.fori_loop(0, hi // 2, body, 0)
    o_ref[...] = _finish_group([acc_sc[h] for h in range(GROUP)]).astype(BF16)


def _gqa_specs(z, q_block, kv_block, tq, seq):
    nq = seq // tq
    qblk = q_block * LANE // GROUP_WIDTH
    return [
        pl.BlockSpec((tq, GROUP_WIDTH), lambda h, bb, i: (bb * nq + i, qblk + h)),
        pl.BlockSpec((seq, LANE), lambda h, bb, i: (bb, kv_block + h)),
    ]


def _resident(a):
    return pl.BlockSpec((1,) + a.shape[1:], lambda h, bb, i: (h,) + (0,) * (a.ndim - 1),
                        pipeline_mode=pl.Buffered(1))


def _gqa_out(m, tq, seq):
    nq = seq // tq
    return (pl.BlockSpec((tq, GROUP_WIDTH), lambda h, bb, i: (bb * nq + i, h)),
            jax.ShapeDtypeStruct((m, BRANCH_WIDTH), BF16))


def _kv_scratch(seq, key_rows):
    return [pltpu.VMEM((key_rows, seq), BF16), pltpu.VMEM((seq, LANE), BF16)]


def _band_attention(z, bias, *, q_block, kv_block, tq, seq, back, name, sinks=None):
    m = z.shape[0]
    b = m // seq
    n_off, w = bias.shape[1], bias.shape[4]
    assert w == back + tq and w <= seq and back % LANE == 0
    kern = functools.partial(_band_attn_kernel, tq=tq, w=w, back=back, n_off=n_off,
                             has_sink=sinks is not None)
    in_specs = _gqa_specs(z, q_block, kv_block, tq, seq) + [_resident(bias)]
    args = [z, z, bias]
    if sinks is not None:
        in_specs.append(pl.BlockSpec((1, GROUP, 1, 1), lambda h, bb, i: (h, 0, 0, 0)))
        args.append(sinks)
    out_spec, out_shape = _gqa_out(m, tq, seq)
    return pl.pallas_call(
        kern,
        grid=(N_KV, b, seq // tq),
        in_specs=in_specs,
        out_specs=out_spec,
        out_shape=out_shape,
        scratch_shapes=_kv_scratch(seq, LANE),
        compiler_params=_cparams(("parallel", "parallel", "arbitrary")),
        name=name,
    )(*args)


def _sel_attention(z, dropped, sel_rows, bias, *, tq, tk, seq):
    m = z.shape[0]
    b = m // seq
    kern = functools.partial(_sel_attn_kernel, tq=tq, tk=tk, n_bias=bias.shape[1])
    in_specs = _gqa_specs(z, ZB_NQ, ZB_SEL_KV, tq, seq) + [
        pl.BlockSpec((1, 1, tq, LANE), lambda h, bb, i: (bb, h, i, 0)),
        pl.BlockSpec((HEAD_DIM, seq), lambda h, bb, i: (0, 0), pipeline_mode=pl.Buffered(1)),
        _resident(bias),
    ]
    out_spec, out_shape = _gqa_out(m, tq, seq)
    return pl.pallas_call(
        kern,
        grid=(N_KV, b, seq // tq),
        in_specs=in_specs,
        out_specs=out_spec,
        out_shape=out_shape,
        scratch_shapes=_kv_scratch(seq, LANE) + [pltpu.VMEM((GROUP, tq, LANE), F32),
                                                 pltpu.VMEM((GROUP, tq, LANE), F32)],
        compiler_params=_cparams(("parallel", "parallel", "arbitrary")),
        name="attn_sel",
    )(z, z, dropped, sel_rows, bias)


def _rope_lanes(x, cos_t, sin_t):
    half = MLA_ROPE // 2
    return x * cos_t + pltpu.roll(x, LANE - half, 1) * sin_t


def _mla_proj_kernel(mq_ref, mkv_ref, kr_ref, cos_ref, sin_ref, qn_ref, wq_ref,
                     kvn_ref, wkt_ref, wv_ref, oq_ref, okt_ref, ov_ref):
    cos_t, sin_t = cos_ref[...], sin_ref[...]
    ql = _rms(mq_ref[...].astype(F32), qn_ref[...]).astype(BF16)
    q = _dot(ql, wq_ref[...]) * (MLA_QK ** -0.5 * LOG2E)
    kvl = _rms(mkv_ref[...].astype(F32), kvn_ref[...]).astype(BF16)
    kt = _dot_nt(wkt_ref[...], kvl)
    v = _dot(kvl, wv_ref[...])
    kr_t = _rope_lanes(kr_ref[...].astype(F32), cos_t, sin_t).T
    lane = lax.broadcasted_iota(jnp.int32, (v.shape[0], LANE), 1)
    for h in range(N_HEADS):
        blk = slice(h * LANE, (h + 1) * LANE)
        oq_ref[:, blk] = _rope_lanes(q[:, blk], cos_t, sin_t).astype(BF16)
        okt_ref[0, blk, :] = (kt[blk, :] + kr_t).astype(BF16)
        ov_ref[:, blk] = jnp.where(lane < MLA_V, v[:, blk], 1.0).astype(BF16)


def _mla_proj(z, cos_t, sin_t, q_norm, w_qb, kv_norm, w_kt, w_v, *, tm, seq):
    m = z.shape[0]
    nt = seq // tm
    width = N_HEADS * LANE

    def full(a):
        return pl.BlockSpec(a.shape, lambda i: (0,) * a.ndim)

    tab = pl.BlockSpec((tm, LANE), lambda i: (i % nt, 0))
    out = jax.ShapeDtypeStruct((m, width), BF16)
    oblk = pl.BlockSpec((tm, width), lambda i: (i, 0))
    return pl.pallas_call(
        _mla_proj_kernel,
        grid=(m // tm,),
        in_specs=[
            pl.BlockSpec((tm, MLA_Q_RANK), lambda i: (i, ZB_MQ * LANE // MLA_Q_RANK)),
            pl.BlockSpec((tm, MLA_KV_RANK), lambda i: (i, ZB_MKV * LANE // MLA_KV_RANK)),
            pl.BlockSpec((tm, LANE), lambda i: (i, ZB_KR)),
            tab, tab,
            full(q_norm), full(w_qb), full(kv_norm), full(w_kt), full(w_v),
        ],
        out_specs=[oblk, pl.BlockSpec((1, width, tm), lambda i: (i // nt, 0, i % nt)), oblk],
        out_shape=[out, jax.ShapeDtypeStruct((m // seq, width, seq), BF16), out],
        compiler_params=_cparams(("parallel",)),
        name="mla_proj",
    )(z, z, z, cos_t, sin_t, q_norm, w_qb, kv_norm, w_kt, w_v)


def _mla_attn_kernel(q_ref, k_ref, v_ref, o_ref, m_sc, acc_sc, *, tq):
    qi = pl.program_id(2)
    n_chain = m_sc.shape[0]
    qs = [q_ref[:, c * LANE:(c + 1) * LANE] for c in range(n_chain)]
    m_sc[...] = jnp.full(m_sc.shape, NEG_INF, F32)
    acc_sc[...] = jnp.zeros(acc_sc.shape, F32)

    def step(tile, width, diagonal):
        start = pl.multiple_of(tile * tq, tq)
        for c in range(n_chain):
            blk = slice(c * LANE, (c + 1) * LANE)
            s = _dot(qs[c], k_ref[0, blk, pl.ds(start, width * tq)])
            if diagonal:
                row = lax.broadcasted_iota(jnp.int32, s.shape, 0) + (width - 1) * tq
                col = lax.broadcasted_iota(jnp.int32, s.shape, 1)
                s = jnp.where(col <= row, s, NEG_INF)
            _online_update(s, v_ref[pl.ds(start, width * tq), blk], m_sc, acc_sc, c)

    odd = qi % 2

    @pl.when(odd == 0)
    def _():
        step(qi, 1, True)

    @pl.when(odd == 1)
    def _():
        step(qi - 1, 2, True)

    def body(i, carry):
        step(qi - odd - 2 * (i + 1), 2, False)
        return carry

    lax.fori_loop(0, qi // 2, body, 0)
    for c in range(n_chain):
        acc = acc_sc[c]
        lane = lax.broadcasted_iota(jnp.int32, acc.shape, 1)
        den = jnp.where(lane < MLA_V, pltpu.roll(acc, MLA_V, 1), acc)
        o_ref[:, c * LANE:(c + 1) * LANE] = (acc / den).astype(BF16)


def _mla_attention(q, k, v1, *, tq, seq):
    m = q.shape[0]
    b = m // seq
    nq = seq // tq
    width = MLA_CHAINS * LANE
    slab = pl.BlockSpec((seq, width), lambda h, bb, i: (bb, h))
    slab_t = pl.BlockSpec((1, width, seq), lambda h, bb, i: (bb, h, 0))
    tile = pl.BlockSpec((tq, width), lambda h, bb, i: (bb * nq + i, h))
    return pl.pallas_call(
        functools.partial(_mla_attn_kernel, tq=tq),
        grid=(N_HEADS // MLA_CHAINS, b, nq),
        in_specs=[tile, slab_t, slab],
        out_specs=tile,
        out_shape=jax.ShapeDtypeStruct(q.shape, BF16),
        scratch_shapes=[pltpu.VMEM((MLA_CHAINS, tq, LANE), F32),
                        pltpu.VMEM((MLA_CHAINS, tq, LANE), F32)],
        compiler_params=_cparams(("parallel", "parallel", "arbitrary")),
        name="attn_mla",
    )(q, k, v1)


def _merge_kernel(ocmp_ref, osel_ref, owin_ref, ob_ref, oc_ref, ng_ref, mg_ref, x_ref,
                  gexp_ref, wab_ref, wc_ref, wo_ref, gpost_ref, o_ref):
    ng = jax.nn.sigmoid(ng_ref[...].astype(F32))
    ng_hi = ng.astype(BF16)
    ng_lo = (ng - ng_hi.astype(F32)).astype(BF16)
    gates = _dot(jnp.concatenate([ng_hi, ng_lo], axis=1), gexp_ref[...])
    o_a = jnp.zeros(ocmp_ref.shape, F32)
    for n, ref in enumerate((ocmp_ref, osel_ref, owin_ref)):
        gate = gates[:, n * BRANCH_WIDTH:(n + 1) * BRANCH_WIDTH]
        o_a = o_a + gate * ref[...].astype(F32)
    ys = (_dot(o_a.astype(BF16), wab_ref[0]), _dot(ob_ref[...], wab_ref[1]),
          _dot(oc_ref[...], wc_ref[...]))
    mixed = jnp.zeros(x_ref.shape, F32)
    for n, y in enumerate(ys):
        gate = jax.nn.sigmoid(mg_ref[:, n * D_MODEL:(n + 1) * D_MODEL].astype(F32))
        mixed = mixed + gate * y
    out = _dot(mixed.astype(BF16), wo_ref[...])
    o_ref[...] = x_ref[...] + _rms(out, gpost_ref[...])


def _merge(o_cmp, o_sel, o_win, o_b, o_c, z, x, gexp, w_ab, w_c, w_out, g_post, *, tm):
    m, d = x.shape
    br = pl.BlockSpec((tm, o_cmp.shape[1]), lambda i: (i, 0))

    def full(a):
        return pl.BlockSpec(a.shape, lambda i: (0,) * a.ndim)

    return pl.pallas_call(
        _merge_kernel,
        grid=(m // tm,),
        in_specs=[
            br, br, br, br,
            pl.BlockSpec((tm, o_c.shape[1]), lambda i: (i, 0)),
            pl.BlockSpec((tm, LANE), lambda i: (i, ZB_NGATE)),
            pl.BlockSpec((tm, 3 * d), lambda i: (i, ZB_MGATE * LANE // (3 * d))),
            pl.BlockSpec((tm, d), lambda i: (i, 0)),
            full(gexp), full(w_ab), full(w_c), full(w_out), full(g_post),
        ],
        out_specs=pl.BlockSpec((tm, d), lambda i: (i, 0)),
        out_shape=jax.ShapeDtypeStruct((m, d), F32),
        compiler_params=_cparams(("parallel",)),
        name="merge",
    )(o_cmp, o_sel, o_win, o_b, o_c, z, z, x, gexp, w_ab, w_c, w_out, g_post)


def _t5_bucket(dist):
    max_exact = NUM_BUCKETS // 2
    d = jnp.maximum(dist, 0)
    df = jnp.maximum(d, 1).astype(F32)
    large = max_exact + (jnp.log(df / max_exact) / math.log(MAX_DISTANCE / max_exact)
                         * (NUM_BUCKETS - max_exact)).astype(jnp.int32)
    large = jnp.minimum(large, NUM_BUCKETS - 1)
    return jnp.where(d < max_exact, d, large)


def _toeplitz(fn, offs, tq, tk):
    ln = tq + tk - 1
    i = jnp.arange(ln + 1)
    shift = jnp.where(i < tk, -i, ln + 1 - i)
    ext = jnp.moveaxis(fn(jnp.asarray(offs)[:, None] + shift[None, :]), -1, 0)
    flat = jnp.tile(ext, (1, 1, tq))[:, :, :tq * ln]
    return flat.reshape(ext.shape[0], len(offs), tq, ln)[:, :, :, :tk]


def _bias_fn(table, window):
    def fn(dist):
        ok = dist >= 0
        if window is not None:
            ok = ok & (dist < window)
        return jnp.where(ok[..., None], table[_t5_bucket(dist)].astype(F32), NEG_INF)
    return fn


def _head_tiles(t):
    return t.reshape(N_KV, GROUP, *t.shape[1:]).transpose(0, 2, 1, 3, 4)


def _in_proj_weight(w_in):
    sizes = (512, 128, 128, 128, 128, 128, 128, 24, 512, 128, 128, MLA_Q_RANK, MLA_KV_RANK,
             MLA_ROPE, 3 * D_MODEL)
    offs = np.concatenate([[0], np.cumsum(sizes)])
    seg = [w_in[:, offs[i]:offs[i + 1]] for i in range(len(sizes))]
    (nq, nkc, nvc, nks, nvs, nkw, nvw, ngate, sq, sk, sv, mq, mkv, mkr, mgate) = seg
    d = w_in.shape[0]
    hd = HEAD_DIM

    def pad(a, left=0):
        return jnp.pad(a, ((0, 0), (left, LANE - left - a.shape[1])))

    def kv_blocks(k, v):
        return [jnp.concatenate([k[:, h * hd:(h + 1) * hd], v[:, h * hd:(h + 1) * hd]], axis=1)
                for h in range(N_KV)]

    mkr3 = jnp.concatenate([mkr, mkr[:, :MLA_ROPE // 2]], axis=1)
    cols = ([mq, pad(ngate), mkv, pad(mkr3, MLA_NOPE), jnp.zeros((d, LANE), w_in.dtype),
             nq * QK_SCALE, nkc, nvc] + kv_blocks(nks, nvs) + kv_blocks(nkw, nvw)
            + [sq * QK_SCALE] + kv_blocks(sk, sv) + [mgate])
    w = jnp.concatenate(cols, axis=1)
    assert w.shape == (d, Z_WIDTH)
    return w.astype(BF16)


def _mla_weights(w_qb, w_kvb, w_c):
    def blocks(a):
        return jnp.pad(a, ((0, 0), (0, 0), (0, LANE - a.shape[2]))).reshape(a.shape[0], -1)

    wq = w_qb.reshape(MLA_Q_RANK, N_HEADS, MLA_QK)
    wq = blocks(jnp.concatenate([wq, wq[:, :, MLA_NOPE:MLA_NOPE + MLA_ROPE // 2]], axis=2))
    wkv = w_kvb.reshape(MLA_KV_RANK, N_HEADS, MLA_NOPE + MLA_V)
    wkt = blocks(wkv[:, :, :MLA_NOPE]).T
    wv = blocks(wkv[:, :, MLA_NOPE:])
    wc = jnp.pad(w_c.reshape(N_HEADS, MLA_V, -1), ((0, 0), (0, LANE - MLA_V), (0, 0)))
    return (wq.astype(BF16), wkt.astype(BF16), wv.astype(BF16),
            wc.reshape(N_HEADS * LANE, -1).astype(BF16))


def _gate_expand():
    e = np.zeros((2 * LANE, 3 * BRANCH_WIDTH), np.float32)
    for n in range(3):
        for h in range(N_HEADS):
            cols = slice(n * BRANCH_WIDTH + h * HEAD_DIM, n * BRANCH_WIDTH + (h + 1) * HEAD_DIM)
            e[h * 3 + n, cols] = 1.0
            e[LANE + h * 3 + n, cols] = 1.0
    return jnp.asarray(e, BF16)


def _rope_tables(pos):
    half = MLA_ROPE // 2
    inv_freq = ROPE_BASE ** (-jnp.arange(half, dtype=F32) / half)
    ang = pos.astype(F32)[:, None] * inv_freq[None, :]
    cos, sin = jnp.cos(ang), jnp.sin(ang)
    n = pos.shape[0]
    ones = jnp.ones((n, MLA_NOPE), F32)
    tail = jnp.zeros((n, LANE - MLA_QK), F32)
    cos_t = jnp.concatenate([ones, cos, cos, tail], axis=1)
    sin_t = jnp.concatenate([0 * ones, -sin, sin, tail], axis=1)
    return cos_t, sin_t


def kernel(x, rel_bias_table, ffn1_norm_pre, ffn1_w_gu, ffn1_w_down, ffn1_norm_post,
           mix_norm_pre, w_in, nsa_pe_k, nsa_w1_k, nsa_w2_k, nsa_pe_v, nsa_w1_v, nsa_w2_v,
           swa_sinks, mla_q_norm, mla_w_qb, mla_kv_norm, mla_w_kvb, w_branch, w_out,
           mix_norm_post, ffn2_norm_pre, ffn2_w_gu, ffn2_w_down, ffn2_norm_post):
    b, s, d = x.shape
    depth = w_in.shape[0]
    m = b * s
    tq = 256
    tq_band = min(256, s // 2)
    tq_sel = min(512, s)
    tk_sel = min(512, s)
    tq_mla = min(512, s)
    ncp = s // CMP_STRIDE
    nsel = s // SEL_LEN
    tm = min(512, m)
    tm_ffn = min(512 * FFN_CHAINS, m)
    tm_big = min(1024, s)

    pos = jnp.arange(s, dtype=jnp.int32)
    tab_a = rel_bias_table[:, :N_HEADS] * LOG2E
    tab_b = rel_bias_table[:, N_HEADS:] * LOG2E
    n_far = -(-(LAST_BUCKET_DIST + tk_sel - 1) // tq_sel)
    n_sel_tiles = min(n_far + 1, s // tq_sel)
    bias_sel = _head_tiles(_toeplitz(_bias_fn(tab_a, None),
                                     [u * tq_sel for u in range(n_sel_tiles)], tq_sel, tk_sel))
    tb = tq_band
    win_offs = sorted({min(i * tb, NSA_WINDOW) for i in range(-(-NSA_WINDOW // tb) + 1)})
    bias_win = _head_tiles(_toeplitz(_bias_fn(tab_a, NSA_WINDOW), win_offs, tb, NSA_WINDOW + tb))
    swa_offs = sorted({min(i * tb, SWA_WINDOW) for i in range(-(-SWA_WINDOW // tb) + 1)})
    bias_swa = _head_tiles(_toeplitz(_bias_fn(tab_b, SWA_WINDOW), swa_offs, tb, SWA_WINDOW + tb))

    def cmp_bias_fn(e):
        r = jnp.arange(CMP_STRIDE)
        dist = e[..., None] * CMP_STRIDE + r - (CMP_LEN - 1)
        return tab_a[_t5_bucket(dist)].astype(F32).reshape(*e.shape, CMP_STRIDE * N_HEADS)

    bias_c = _toeplitz(cmp_bias_fn, [0], ncp, ncp).reshape(CMP_STRIDE, N_HEADS, ncp, ncp)
    bias_c = bias_c.transpose(1, 2, 0, 3).reshape(N_KV, GROUP, s, ncp)
    ci = jnp.arange(ncp)[None, :] * CMP_STRIDE
    sj = jnp.arange(nsel)[:, None] * SEL_LEN
    ovl_t = (jnp.maximum(jnp.minimum(ci + CMP_LEN, sj + SEL_LEN) - jnp.maximum(ci, sj), 0)
             .astype(F32) / CMP_LEN)
    cos_t, sin_t = _rope_tables(pos)
    gexp = _gate_expand()
    assert nsel <= HEAD_DIM
    sel_rows = jnp.where(jnp.arange(HEAD_DIM)[:, None] == (pos[None, :] >> SEL_SHIFT),
                         NEG_INF, 0.0).astype(BF16)
    chunk = CMP_STRIDE * HEAD_DIM

    xf = x.reshape(m, d)
    for l in range(depth):
        xf = _ffn(xf, ffn1_norm_pre[l][None], ffn1_w_gu[l].astype(BF16),
                  ffn1_w_down[l].astype(BF16), ffn1_norm_post[l][None], tm=tm_ffn)

        z = _inproj(xf, mix_norm_pre[l][None], _in_proj_weight(w_in[l]), tm=tm_big, tn=Z_WIDTH // 4)

        def chunks(blk):
            a = z[:, blk * LANE:(blk + 1) * LANE].reshape(b, s, N_KV, HEAD_DIM)
            return a.transpose(0, 2, 1, 3).reshape(b, N_KV, ncp, chunk)

        kc, vc = _cmp_mlp(chunks(ZB_CMP_K), chunks(ZB_CMP_V),
                          nsa_pe_k[l].reshape(2, chunk), nsa_w1_k[l].astype(BF16),
                          nsa_w2_k[l].astype(BF16),
                          nsa_pe_v[l].reshape(2, chunk), nsa_w1_v[l].astype(BF16),
                          nsa_w2_v[l].astype(BF16))
        o_cmp, dropped = _cmp_attn(z, jnp.tile(kc.transpose(0, 1, 3, 2), (1, 1, 2, 1)),
                                   jnp.tile(vc, (1, 1, 1, 2)),
                                   bias_c, ovl_t, tq=tq, seq=s)
        o_sel = _sel_attention(z, dropped, sel_rows, bias_sel, tq=tq_sel, tk=tk_sel, seq=s)
        o_win = _band_attention(z, bias_win, q_block=ZB_NQ, kv_block=ZB_WIN_KV, tq=tq_band, seq=s,
                                back=NSA_WINDOW, name="attn_win")
        o_b = _band_attention(z, bias_swa, q_block=ZB_SQ, kv_block=ZB_SWA_KV, tq=tq_band, seq=s,
                              back=SWA_WINDOW, name="attn_swa",
                              sinks=(swa_sinks[l] * LOG2E).reshape(N_KV, GROUP, 1, 1))
        wq, wkt, wv, wc = _mla_weights(mla_w_qb[l], mla_w_kvb[l], w_branch[l, 2])
        mq, mk, mv1 = _mla_proj(z, cos_t, sin_t, mla_q_norm[l][None], wq,
                                mla_kv_norm[l][None], wkt, wv, tm=tm_big, seq=s)
        o_c = _mla_attention(mq, mk, mv1, tq=tq_mla, seq=s)

        xf = _merge(o_cmp, o_sel, o_win, o_b, o_c, z, xf, gexp, w_branch[l, :2].astype(BF16), wc,
                    w_out[l].astype(BF16), mix_norm_post[l][None], tm=tm)

        xf = _ffn(xf, ffn2_norm_pre[l][None], ffn2_w_gu[l].astype(BF16),
                  ffn2_w_down[l].astype(BF16), ffn2_norm_post[l][None], tm=tm_ffn)
    return xf.reshape(b, s, d)
```

```python
import functools
import math

import jax
import jax.numpy as jnp
import numpy as np
from jax import lax
from jax.experimental import pallas as pl
from jax.experimental.pallas import tpu as pltpu

F32 = jnp.float32
BF16 = jnp.bfloat16

D_MODEL = 1024
D_FF = 2816
HEAD_DIM = 64
NORM_EPS = 1e-6
NUM_BUCKETS = 32
MAX_DISTANCE = 1024
LAST_BUCKET_DIST = 1 + math.ceil(
    (NUM_BUCKETS // 2) * (MAX_DISTANCE / (NUM_BUCKETS // 2))
    ** ((NUM_BUCKETS // 2 - 1) / (NUM_BUCKETS - NUM_BUCKETS // 2)))
N_HEADS = 8
N_KV = 2
GROUP = N_HEADS // N_KV
CMP_LEN = 32
CMP_STRIDE = 16
CMP_HIDDEN = 128
SEL_LEN = 64
SEL_SHIFT = 6
SEL_TOPK = 16
NSA_WINDOW = 512
SWA_WINDOW = 128
MLA_Q_RANK = 384
MLA_KV_RANK = 256
MLA_NOPE = 64
MLA_ROPE = 32
MLA_V = 64
MLA_QK = MLA_NOPE + MLA_ROPE
ROPE_BASE = 10000.0
BRANCH_WIDTH = N_HEADS * HEAD_DIM
NEG_INF = -1e30
SEL_FORCE = 1e9
LOG2E = math.log2(math.e)
QK_SCALE = HEAD_DIM ** -0.5 * LOG2E

LANE = 128
SUBLANES = 8
VMEM_LIMIT = 56 * 1024 * 1024
GROUP_WIDTH = GROUP * HEAD_DIM
FFN_CHAINS = 2
FFN_CHUNK = 256
CMP_PARTS = 4
QK_SPLIT = 2
KV_SPLIT_CHUNK = 512
MLA_CHAINS = 8

ZB_MQ = 0
ZB_NGATE = 3
ZB_MKV = 4
ZB_KR = 6
ZB_NQ = 8
ZB_CMP_K = 12
ZB_CMP_V = 13
ZB_SEL_KV = 14
ZB_WIN_KV = 16
ZB_SQ = 18
ZB_SWA_KV = 22
ZB_MGATE = 24
Z_BLOCKS = 48
Z_WIDTH = Z_BLOCKS * LANE


def _cparams(sem):
    return pltpu.CompilerParams(dimension_semantics=sem, vmem_limit_bytes=VMEM_LIMIT)


def _rms(x, g):
    inv = lax.rsqrt(jnp.mean(x * x, axis=-1, keepdims=True) + NORM_EPS)
    return (x * inv) * g


def _dot(a, b):
    return jnp.dot(a, b, preferred_element_type=F32)


def _dot_nt(a, b, precision=None):
    return lax.dot_general(a, b, (((1,), (1,)), ((), ())), precision=precision,
                           preferred_element_type=F32)


def _ffn_kernel(x_ref, gpre_ref, wg_ref, wu_ref, wd_ref, gpost_ref, o_ref):
    rows = x_ref.shape[0] // FFN_CHAINS
    for c in range(FFN_CHAINS):
        sl = slice(c * rows, (c + 1) * rows)
        x = x_ref[sl, :]
        h = _rms(x, gpre_ref[...]).astype(BF16)
        y = jnp.zeros(x.shape, F32)
        for j in range(D_FF // FFN_CHUNK):
            cols = slice(j * FFN_CHUNK, (j + 1) * FFN_CHUNK)
            g = _dot(h, wg_ref[:, cols])
            u = _dot(h, wu_ref[:, cols])
            act = (g * jax.nn.sigmoid(g)) * u
            y = y + _dot(act.astype(BF16), wd_ref[cols, :])
        o_ref[sl, :] = x + 0.5 * _rms(y, gpost_ref[...])


def _ffn(x, g_pre, w_gu, w_down, g_post, *, tm):
    m, d = x.shape
    once = pl.Buffered(1)
    return pl.pallas_call(
        _ffn_kernel,
        grid=(m // tm,),
        in_specs=[
            pl.BlockSpec((tm, d), lambda i: (i, 0)),
            pl.BlockSpec((1, d), lambda i: (0, 0)),
            pl.BlockSpec((d, D_FF), lambda i: (0, 0), pipeline_mode=once),
            pl.BlockSpec((d, D_FF), lambda i: (0, 1), pipeline_mode=once),
            pl.BlockSpec((D_FF, d), lambda i: (0, 0), pipeline_mode=once),
            pl.BlockSpec((1, d), lambda i: (0, 0)),
        ],
        out_specs=pl.BlockSpec((tm, d), lambda i: (i, 0)),
        out_shape=jax.ShapeDtypeStruct((m, d), F32),
        compiler_params=_cparams(("parallel",)),
        name="ffn",
    )(x, g_pre, w_gu, w_gu, w_down, g_post)


def _inproj_kernel(x_ref, g_ref, w_ref, o_ref, h_sc):
    @pl.when(pl.program_id(1) == 0)
    def _():
        h_sc[...] = _rms(x_ref[...], g_ref[...]).astype(BF16)

    o_ref[...] = _dot(h_sc[...], w_ref[...]).astype(BF16)


def _inproj(x, g, w, *, tm, tn):
    m, d = x.shape
    n = w.shape[1]
    return pl.pallas_call(
        _inproj_kernel,
        grid=(m // tm, n // tn),
        in_specs=[
            pl.BlockSpec((tm, d), lambda i, j: (i, 0)),
            pl.BlockSpec((1, d), lambda i, j: (0, 0)),
            pl.BlockSpec((d, tn), lambda i, j: (0, j)),
        ],
        out_specs=pl.BlockSpec((tm, tn), lambda i, j: (i, j)),
        out_shape=jax.ShapeDtypeStruct((m, n), BF16),
        scratch_shapes=[pltpu.VMEM((tm, d), BF16)],
        compiler_params=_cparams(("parallel", "arbitrary")),
        name="inproj",
    )(x, g, w)


def _cmp_mlp_kernel(k_ref, v_ref, pek_ref, w1k_ref, w2k_ref, pev_ref, w1v_ref, w2v_ref,
                    kc_ref, vc_ref):
    half = CMP_STRIDE * HEAD_DIM

    def one(x_ref, pe_ref, w1_ref, w2_ref, o_ref):
        x = x_ref[0, 0].astype(F32)
        ncp = x.shape[0]
        xa = (x + pe_ref[0:1, :]).astype(BF16)
        xb = (x + pe_ref[1:2, :]).astype(BF16)
        pa = _dot(xa, w1_ref[0:half, :])
        pb = _dot(xb, w1_ref[half:2 * half, :])
        hid = pa + pltpu.roll(pb, ncp - 1, 0)
        out = _dot(jax.nn.gelu(hid).astype(BF16), w2_ref[...])
        row = lax.broadcasted_iota(jnp.int32, out.shape, 0)
        o_ref[0, 0] = jnp.where(row < ncp - 1, out, 0.0).astype(BF16)

    one(k_ref, pek_ref, w1k_ref, w2k_ref, kc_ref)
    one(v_ref, pev_ref, w1v_ref, w2v_ref, vc_ref)


def _cmp_mlp(k4, v4, pek, w1k, w2k, pev, w1v, w2v):
    b, hkv, ncp, width = k4.shape
    blk = pl.BlockSpec((1, 1, ncp, width), lambda i, j: (i, j, 0, 0))
    oblk = pl.BlockSpec((1, 1, ncp, HEAD_DIM), lambda i, j: (i, j, 0, 0))

    def full(a):
        return pl.BlockSpec(a.shape, lambda i, j: (0,) * a.ndim)

    out = jax.ShapeDtypeStruct((b, hkv, ncp, HEAD_DIM), BF16)
    return pl.pallas_call(
        _cmp_mlp_kernel,
        grid=(b, hkv),
        in_specs=[blk, blk, full(pek), full(w1k), full(w2k), full(pev), full(w1v), full(w2v)],
        out_specs=[oblk, oblk],
        out_shape=[out, out],
        compiler_params=_cparams(("parallel", "parallel")),
        name="cmp_mlp",
    )(k4, v4, pek, w1k, w2k, pev, w1v, w2v)


def _stack_heads(q_all, upper):
    q32 = q_all.astype(F32)
    lane = lax.broadcasted_iota(jnp.int32, upper.shape, 1)
    rows = []
    for pair in range(GROUP // 2):
        half = q32[:, pair * LANE:(pair + 1) * LANE]
        rows.append(jnp.where(lane < HEAD_DIM, half, upper))
        rows.append(jnp.where(lane < HEAD_DIM, pltpu.roll(half, HEAD_DIM, 1), upper))
    return jnp.concatenate(rows, axis=0).astype(BF16)


def _split_kv(kv_ref, kt_sc, v_sc, key_rows=None):
    seq = kv_ref.shape[0]
    chunk = min(KV_SPLIT_CHUNK, seq)
    copies = kt_sc.shape[0] // HEAD_DIM - (key_rows is not None)
    lane = lax.broadcasted_iota(jnp.int32, (chunk, LANE), 1)
    for c in range(seq // chunk):
        rows = slice(c * chunk, (c + 1) * chunk)
        x = kv_ref[rows, :].astype(F32)
        kt = x.T[0:HEAD_DIM, :].astype(BF16)
        for r in range(copies):
            kt_sc[r * HEAD_DIM:(r + 1) * HEAD_DIM, rows] = kt
        v_sc[rows, :] = jnp.where(lane < HEAD_DIM, pltpu.roll(x, HEAD_DIM, 1), 1.0).astype(BF16)
    if key_rows is not None:
        kt_sc[copies * HEAD_DIM:(copies + 1) * HEAD_DIM, :] = key_rows


def _group_logits(q_stack, kt, tq):
    per = GROUP // QK_SPLIT
    parts = [_dot(q_stack[p * per * tq:(p + 1) * per * tq], kt) for p in range(QK_SPLIT)]
    return [parts[h // per][(h % per) * tq:(h % per + 1) * tq] for h in range(GROUP)]


def _finish_pair_rolled(acc_even, acc_odd):
    lane = lax.broadcasted_iota(jnp.int32, acc_even.shape, 1)
    even = acc_even / pltpu.roll(acc_even, HEAD_DIM, 1)
    odd = pltpu.roll(acc_odd, HEAD_DIM, 1) / acc_odd
    return jnp.where(lane < HEAD_DIM, even, odd)


def _finish_group(accs):
    return jnp.concatenate([_finish_pair_rolled(accs[0], accs[1]),
                            _finish_pair_rolled(accs[2], accs[3])], axis=1)


def _online_update(s, v, m_ref, acc_ref, idx):
    m_old = m_ref[idx]
    m_new = jnp.maximum(m_old, jnp.max(s, axis=-1, keepdims=True))
    alpha = jnp.exp2(m_old - m_new)
    p = jnp.exp2(s - jnp.concatenate([m_new] * (s.shape[1] // LANE), axis=1))
    acc = acc_ref[idx]
    acc_ref[idx] = (jnp.concatenate([alpha] * (acc.shape[1] // LANE), axis=1) * acc
                    + _dot(p.astype(BF16), v))
    m_ref[idx] = m_new


def _cmp_attn_kernel(q_ref, kc_ref, vc_ref, bias_ref, ovl_ref, o_ref, sel_ref, *, tq, ncp, nsel, nq):
    qi = pl.program_id(1)
    parts = CMP_PARTS if nq % CMP_PARTS == 0 and nsel % (CMP_PARTS * SUBLANES) == 0 else 1
    for k in range(parts):
        ncols = min(ncp, -(-((k + 1) * ncp // parts) // LANE) * LANE)
        nblk = (k + 1) * nsel // parts
        pl.when(qi // (nq // parts) == k)(functools.partial(
            _cmp_attn_body, q_ref, kc_ref, vc_ref, bias_ref, ovl_ref, o_ref, sel_ref,
            tq=tq, ncols=ncols, nblk=nblk))


def _cmp_attn_body(q_ref, kc_ref, vc_ref, bias_ref, ovl_ref, o_ref, sel_ref, *, tq, ncols, nblk):
    qi = pl.program_id(1)
    vc = vc_ref[0, 0, 0:ncols, :]
    lane = lax.broadcasted_iota(jnp.int32, (tq, LANE), 1)
    row = lax.broadcasted_iota(jnp.int32, (tq, ncols), 0) + qi * tq
    col = lax.broadcasted_iota(jnp.int32, (tq, ncols), 1)
    valid = (col * CMP_STRIDE + (CMP_LEN - 1)) <= row
    p_sum = jnp.zeros((tq, ncols), F32)
    outs = []
    q = _stack_heads(q_ref[...], jnp.zeros((tq, LANE), F32))
    logits = _group_logits(q, kc_ref[0, 0, :, 0:ncols], tq)
    for g in range(GROUP):
        s = logits[g] + bias_ref[0, g, :, 0:ncols]
        s = jnp.where(valid, s, NEG_INF)
        e = jnp.where(valid, jnp.exp2(s - jnp.max(s, axis=-1, keepdims=True)), 0.0)
        den = jnp.sum(e, axis=-1, keepdims=True)
        p = e / jnp.where(den > 0, den, 1.0)
        outs.append(_dot(p.astype(BF16), vc))
        p_sum = p_sum + p
    o_ref[...] = jnp.concatenate([jnp.where(lane < HEAD_DIM, outs[0], outs[1]),
                                  jnp.where(lane < HEAD_DIM, outs[2], outs[3])],
                                 axis=1).astype(BF16)

    imp = _dot_nt(ovl_ref[0:nblk, 0:ncols], p_sum, precision=lax.Precision.HIGHEST)
    jj = lax.broadcasted_iota(jnp.int32, (nblk, tq), 0)
    cur = (lax.broadcasted_iota(jnp.int32, (nblk, tq), 1) + qi * tq) >> SEL_SHIFT
    forced = (jj == 0) | (jj == cur) | (jj == cur - 1)
    score = jnp.where(forced, SEL_FORCE, jnp.where(jj <= cur, imp, -SEL_FORCE))
    sub = SUBLANES
    groups = [score[g * sub:(g + 1) * sub, :] for g in range(nblk // sub)]
    ranks = [jnp.zeros((sub, tq), F32) for _ in groups]
    jg = lax.broadcasted_iota(jnp.int32, (sub, tq), 0)
    for i in range(nblk):
        si = score[i:i + 1, :]
        for g, sg in enumerate(groups):
            if g * sub > i:
                ahead = jnp.where(si >= sg, 1.0, 0.0)
            elif (g + 1) * sub <= i:
                ahead = jnp.where(si > sg, 1.0, 0.0)
            else:
                ahead = jnp.where(jg + g * sub > i, jnp.where(si >= sg, 1.0, 0.0),
                                  jnp.where(si > sg, 1.0, 0.0))
            ranks[g] = ranks[g] + ahead
    rank = jnp.concatenate(ranks, axis=0)
    dropped = jnp.where(rank < SEL_TOPK, 0.0, 1.0)
    pieces = [jnp.zeros((HEAD_DIM, tq), F32), dropped]
    if nblk < LANE - HEAD_DIM:
        pieces.append(jnp.zeros((LANE - HEAD_DIM - nblk, tq), F32))
    sel_ref[0, 0] = jnp.concatenate(pieces, axis=0).T.astype(BF16)


def _cmp_attn(z, kc4, vc, bias_c, ovl_t, *, tq, seq):
    b, hkv, _, ncp = kc4.shape
    m = z.shape[0]
    nq = seq // tq
    nsel = seq // SEL_LEN
    kern = functools.partial(_cmp_attn_kernel, tq=tq, ncp=ncp, nsel=nsel, nq=nq)
    qblk = ZB_NQ * LANE // GROUP_WIDTH
    return pl.pallas_call(
        kern,
        grid=(hkv, nq, b),
        in_specs=[
            pl.BlockSpec((tq, GROUP_WIDTH), lambda h, i, bb: (bb * nq + i, qblk + h)),
            pl.BlockSpec((1, 1, LANE, ncp), lambda h, i, bb: (bb, h, 0, 0)),
            pl.BlockSpec((1, 1, ncp, LANE), lambda h, i, bb: (bb, h, 0, 0)),
            pl.BlockSpec((1, GROUP, tq, ncp), lambda h, i, bb: (h, 0, i, 0)),
            pl.BlockSpec((nsel, ncp), lambda h, i, bb: (0, 0)),
        ],
        out_specs=[
            pl.BlockSpec((tq, GROUP_WIDTH), lambda h, i, bb: (bb * nq + i, h)),
            pl.BlockSpec((1, 1, tq, LANE), lambda h, i, bb: (bb, h, i, 0)),
        ],
        out_shape=[
            jax.ShapeDtypeStruct((m, BRANCH_WIDTH), BF16),
            jax.ShapeDtypeStruct((b, hkv, seq, LANE), BF16),
        ],
        compiler_params=_cparams(("parallel", "parallel", "parallel")),
        name="cmp_attn",
    )(z, kc4, vc, bias_c, ovl_t)


def _band_attn_kernel(*refs, tq, w, back, n_off, has_sink):
    if has_sink:
        q_ref, kv_ref, bias_ref, sink_ref, o_ref, kt_sc, v_sc = refs
    else:
        q_ref, kv_ref, bias_ref, o_ref, kt_sc, v_sc = refs
    qi = pl.program_id(2)

    @pl.when(qi == 0)
    def _():
        _split_kv(kv_ref, kt_sc, v_sc)

    off = jnp.minimum(qi * tq, back)
    start = pl.multiple_of(qi * tq - off, LANE)
    q = _stack_heads(q_ref[...], jnp.zeros((tq, LANE), F32))
    logits = _group_logits(q, kt_sc[:, pl.ds(start, w)], tq)
    v1 = v_sc[pl.ds(start, w), :]
    tile = jnp.minimum(qi, n_off - 1)
    sum_lane = lax.broadcasted_iota(jnp.int32, (tq, LANE), 1) >= HEAD_DIM
    accs = []
    for h in range(GROUP):
        s = logits[h] + bias_ref[0, tile, h]
        m = jnp.broadcast_to(jnp.max(s, axis=-1, keepdims=True), (tq, LANE))
        if has_sink:
            sink = sink_ref[0, h]
            m = jnp.maximum(m, sink)
        p = jnp.exp2(s - jnp.concatenate([m] * (w // LANE), axis=1))
        acc = _dot(p.astype(BF16), v1)
        if has_sink:
            acc = acc + jnp.where(sum_lane, jnp.exp2(sink - m), 0.0)
        accs.append(acc)
    o_ref[...] = _finish_group(accs).astype(BF16)


def _sel_attn_kernel(q_ref, kv_ref, drop_ref, rows_ref, bias_ref, o_ref, kt_sc, v_sc, m_sc, acc_sc,
                     *, tq, tk, n_bias):
    qi = pl.program_id(2)

    @pl.when(qi == 0)
    def _():
        _split_kv(kv_ref, kt_sc, v_sc, rows_ref[...])

    q = _stack_heads(q_ref[...], drop_ref[0, 0].astype(F32))
    hi = ((qi + 1) * tq + tk - 1) // tk
    m_sc[...] = jnp.full(m_sc.shape, NEG_INF, F32)
    acc_sc[...] = jnp.zeros(acc_sc.shape, F32)

    def step(kj):
        start = pl.multiple_of(kj * tk, tk)
        logits = _group_logits(q, kt_sc[:, pl.ds(start, tk)], tq)
        v1 = v_sc[pl.ds(start, tk), :]
        u = jnp.minimum((qi * tq - kj * tk) // tq, n_bias - 1)
        for h in range(GROUP):
            s = logits[h] + bias_ref[0, u, h]
            _online_update(s, v1, m_sc, acc_sc, h)

    odd = hi % 2

    @pl.when(odd == 1)
    def _():
        step(hi - 1)

    def body(i, carry):
        kj = hi - odd - 1 - 2 * i
        step(kj)
        step(kj - 1)
        return carry

    lax.fori_loop(0, hi // 2, body, 0)
    o_ref[...] = _finish_group([acc_sc[h] for h in range(GROUP)]).astype(BF16)


def _gqa_specs(z, q_block, kv_block, tq, seq):
    nq = seq // tq
    qblk = q_block * LANE // GROUP_WIDTH
    return [
        pl.BlockSpec((tq, GROUP_WIDTH), lambda h, bb, i: (bb * nq + i, qblk + h)),
        pl.BlockSpec((seq, LANE), lambda h, bb, i: (bb, kv_block + h)),
    ]


def _resident(a):
    return pl.BlockSpec((1,) + a.shape[1:], lambda h, bb, i: (h,) + (0,) * (a.ndim - 1),
                        pipeline_mode=pl.Buffered(1))


def _gqa_out(m, tq, seq):
    nq = seq // tq
    return (pl.BlockSpec((tq, GROUP_WIDTH), lambda h, bb, i: (bb * nq + i, h)),
            jax.ShapeDtypeStruct((m, BRANCH_WIDTH), BF16))


def _kv_scratch(seq, key_rows):
    return [pltpu.VMEM((key_rows, seq), BF16), pltpu.VMEM((seq, LANE), BF16)]


def _band_attention(z, bias, *, q_block, kv_block, tq, seq, back, name, sinks=None):
    m = z.shape[0]
    b = m // seq
    n_off, w = bias.shape[1], bias.shape[4]
    assert w == back + tq and w <= seq and back % LANE == 0
    kern = functools.partial(_band_attn_kernel, tq=tq, w=w, back=back, n_off=n_off,
                             has_sink=sinks is not None)
    in_specs = _gqa_specs(z, q_block, kv_block, tq, seq) + [_resident(bias)]
    args = [z, z, bias]
    if sinks is not None:
        in_specs.append(pl.BlockSpec((1, GROUP, 1, 1), lambda h, bb, i: (h, 0, 0, 0)))
        args.append(sinks)
    out_spec, out_shape = _gqa_out(m, tq, seq)
    return pl.pallas_call(
        kern,
        grid=(N_KV, b, seq // tq),
        in_specs=in_specs,
        out_specs=out_spec,
        out_shape=out_shape,
        scratch_shapes=_kv_scratch(seq, LANE),
        compiler_params=_cparams(("parallel", "parallel", "arbitrary")),
        name=name,
    )(*args)


def _sel_attention(z, dropped, sel_rows, bias, *, tq, tk, seq):
    m = z.shape[0]
    b = m // seq
    kern = functools.partial(_sel_attn_kernel, tq=tq, tk=tk, n_bias=bias.shape[1])
    in_specs = _gqa_specs(z, ZB_NQ, ZB_SEL_KV, tq, seq) + [
        pl.BlockSpec((1, 1, tq, LANE), lambda h, bb, i: (bb, h, i, 0)),
        pl.BlockSpec((HEAD_DIM, seq), lambda h, bb, i: (0, 0), pipeline_mode=pl.Buffered(1)),
        _resident(bias),
    ]
    out_spec, out_shape = _gqa_out(m, tq, seq)
    return pl.pallas_call(
        kern,
        grid=(N_KV, b, seq // tq),
        in_specs=in_specs,
        out_specs=out_spec,
        out_shape=out_shape,
        scratch_shapes=_kv_scratch(seq, LANE) + [pltpu.VMEM((GROUP, tq, LANE), F32),
                                                 pltpu.VMEM((GROUP, tq, LANE), F32)],
        compiler_params=_cparams(("parallel", "parallel", "arbitrary")),
        name="attn_sel",
    )(z, z, dropped, sel_rows, bias)


def _rope_lanes(x, cos_t, sin_t):
    half = MLA_ROPE // 2
    return x * cos_t + pltpu.roll(x, LANE - half, 1) * sin_t


def _mla_proj_kernel(mq_ref, mkv_ref, kr_ref, cos_ref, sin_ref, qn_ref, wq_ref,
                     kvn_ref, wkt_ref, wv_ref, oq_ref, okt_ref, ov_ref):
    cos_t, sin_t = cos_ref[...], sin_ref[...]
    ql = _rms(mq_ref[...].astype(F32), qn_ref[...]).astype(BF16)
    q = _dot(ql, wq_ref[...]) * (MLA_QK ** -0.5 * LOG2E)
    kvl = _rms(mkv_ref[...].astype(F32), kvn_ref[...]).astype(BF16)
    kt = _dot_nt(wkt_ref[...], kvl)
    v = _dot(kvl, wv_ref[...])
    kr_t = _rope_lanes(kr_ref[...].astype(F32), cos_t, sin_t).T
    lane = lax.broadcasted_iota(jnp.int32, (v.shape[0], LANE), 1)
    for h in range(N_HEADS):
        blk = slice(h * LANE, (h + 1) * LANE)
        oq_ref[:, blk] = _rope_lanes(q[:, blk], cos_t, sin_t).astype(BF16)
        okt_ref[0, blk, :] = (kt[blk, :] + kr_t).astype(BF16)
        ov_ref[:, blk] = jnp.where(lane < MLA_V, v[:, blk], 1.0).astype(BF16)


def _mla_proj(z, cos_t, sin_t, q_norm, w_qb, kv_norm, w_kt, w_v, *, tm, seq):
    m = z.shape[0]
    nt = seq // tm
    width = N_HEADS * LANE

    def full(a):
        return pl.BlockSpec(a.shape, lambda i: (0,) * a.ndim)

    tab = pl.BlockSpec((tm, LANE), lambda i: (i % nt, 0))
    out = jax.ShapeDtypeStruct((m, width), BF16)
    oblk = pl.BlockSpec((tm, width), lambda i: (i, 0))
    return pl.pallas_call(
        _mla_proj_kernel,
        grid=(m // tm,),
        in_specs=[
            pl.BlockSpec((tm, MLA_Q_RANK), lambda i: (i, ZB_MQ * LANE // MLA_Q_RANK)),
            pl.BlockSpec((tm, MLA_KV_RANK), lambda i: (i, ZB_MKV * LANE // MLA_KV_RANK)),
            pl.BlockSpec((tm, LANE), lambda i: (i, ZB_KR)),
            tab, tab,
            full(q_norm), full(w_qb), full(kv_norm), full(w_kt), full(w_v),
        ],
        out_specs=[oblk, pl.BlockSpec((1, width, tm), lambda i: (i // nt, 0, i % nt)), oblk],
        out_shape=[out, jax.ShapeDtypeStruct((m // seq, width, seq), BF16), out],
        compiler_params=_cparams(("parallel",)),
        name="mla_proj",
    )(z, z, z, cos_t, sin_t, q_norm, w_qb, kv_norm, w_kt, w_v)


def _mla_attn_kernel(q_ref, k_ref, v_ref, o_ref, m_sc, acc_sc, *, tq):
    qi = pl.program_id(2)
    n_chain = m_sc.shape[0]
    qs = [q_ref[:, c * LANE:(c + 1) * LANE] for c in range(n_chain)]
    m_sc[...] = jnp.full(m_sc.shape, NEG_INF, F32)
    acc_sc[...] = jnp.zeros(acc_sc.shape, F32)

    def step(tile, width, diagonal):
        start = pl.multiple_of(tile * tq, tq)
        for c in range(n_chain):
            blk = slice(c * LANE, (c + 1) * LANE)
            s = _dot(qs[c], k_ref[0, blk, pl.ds(start, width * tq)])
            if diagonal:
                row = lax.broadcasted_iota(jnp.int32, s.shape, 0) + (width - 1) * tq
                col = lax.broadcasted_iota(jnp.int32, s.shape, 1)
                s = jnp.where(col <= row, s, NEG_INF)
            _online_update(s, v_ref[pl.ds(start, width * tq), blk], m_sc, acc_sc, c)

    odd = qi % 2

    @pl.when(odd == 0)
    def _():
        step(qi, 1, True)

    @pl.when(odd == 1)
    def _():
        step(qi - 1, 2, True)

    def body(i, carry):
        step(qi - odd - 2 * (i + 1), 2, False)
        return carry

    lax.fori_loop(0, qi // 2, body, 0)
    for c in range(n_chain):
        acc = acc_sc[c]
        lane = lax.broadcasted_iota(jnp.int32, acc.shape, 1)
        den = jnp.where(lane < MLA_V, pltpu.roll(acc, MLA_V, 1), acc)
        o_ref[:, c * LANE:(c + 1) * LANE] = (acc / den).astype(BF16)


def _mla_attention(q, k, v1, *, tq, seq):
    m = q.shape[0]
    b = m // seq
    nq = seq // tq
    width = MLA_CHAINS * LANE
    slab = pl.BlockSpec((seq, width), lambda h, bb, i: (bb, h))
    slab_t = pl.BlockSpec((1, width, seq), lambda h, bb, i: (bb, h, 0))
    tile = pl.BlockSpec((tq, width), lambda h, bb, i: (bb * nq + i, h))
    return pl.pallas_call(
        functools.partial(_mla_attn_kernel, tq=tq),
        grid=(N_HEADS // MLA_CHAINS, b, nq),
        in_specs=[tile, slab_t, slab],
        out_specs=tile,
        out_shape=jax.ShapeDtypeStruct(q.shape, BF16),
        scratch_shapes=[pltpu.VMEM((MLA_CHAINS, tq, LANE), F32),
                        pltpu.VMEM((MLA_CHAINS, tq, LANE), F32)],
        compiler_params=_cparams(("parallel", "parallel", "arbitrary")),
        name="attn_mla",
    )(q, k, v1)


def _merge_kernel(ocmp_ref, osel_ref, owin_ref, ob_ref, oc_ref, ng_ref, mg_ref, x_ref,
                  gexp_ref, wab_ref, wc_ref, wo_ref, gpost_ref, o_ref):
    ng = jax.nn.sigmoid(ng_ref[...].astype(F32))
    ng_hi = ng.astype(BF16)
    ng_lo = (ng - ng_hi.astype(F32)).astype(BF16)
    gates = _dot(jnp.concatenate([ng_hi, ng_lo], axis=1), gexp_ref[...])
    o_a = jnp.zeros(ocmp_ref.shape, F32)
    for n, ref in enumerate((ocmp_ref, osel_ref, owin_ref)):
        gate = gates[:, n * BRANCH_WIDTH:(n + 1) * BRANCH_WIDTH]
        o_a = o_a + gate * ref[...].astype(F32)
    ys = (_dot(o_a.astype(BF16), wab_ref[0]), _dot(ob_ref[...], wab_ref[1]),
          _dot(oc_ref[...], wc_ref[...]))
    mixed = jnp.zeros(x_ref.shape, F32)
    for n, y in enumerate(ys):
        gate = jax.nn.sigmoid(mg_ref[:, n * D_MODEL:(n + 1) * D_MODEL].astype(F32))
        mixed = mixed + gate * y
    out = _dot(mixed.astype(BF16), wo_ref[...])
    o_ref[...] = x_ref[...] + _rms(out, gpost_ref[...])


def _merge(o_cmp, o_sel, o_win, o_b, o_c, z, x, gexp, w_ab, w_c, w_out, g_post, *, tm):
    m, d = x.shape
    br = pl.BlockSpec((tm, o_cmp.shape[1]), lambda i: (i, 0))

    def full(a):
        return pl.BlockSpec(a.shape, lambda i: (0,) * a.ndim)

    return pl.pallas_call(
        _merge_kernel,
        grid=(m // tm,),
        in_specs=[
            br, br, br, br,
            pl.BlockSpec((tm, o_c.shape[1]), lambda i: (i, 0)),
            pl.BlockSpec((tm, LANE), lambda i: (i, ZB_NGATE)),
            pl.BlockSpec((tm, 3 * d), lambda i: (i, ZB_MGATE * LANE // (3 * d))),
            pl.BlockSpec((tm, d), lambda i: (i, 0)),
            full(gexp), full(w_ab), full(w_c), full(w_out), full(g_post),
        ],
        out_specs=pl.BlockSpec((tm, d), lambda i: (i, 0)),
        out_shape=jax.ShapeDtypeStruct((m, d), F32),
        compiler_params=_cparams(("parallel",)),
        name="merge",
    )(o_cmp, o_sel, o_win, o_b, o_c, z, z, x, gexp, w_ab, w_c, w_out, g_post)


def _t5_bucket(dist):
    max_exact = NUM_BUCKETS // 2
    d = jnp.maximum(dist, 0)
    df = jnp.maximum(d, 1).astype(F32)
    large = max_exact + (jnp.log(df / max_exact) / math.log(MAX_DISTANCE / max_exact)
                         * (NUM_BUCKETS - max_exact)).astype(jnp.int32)
    large = jnp.minimum(large, NUM_BUCKETS - 1)
    return jnp.where(d < max_exact, d, large)


def _toeplitz(fn, offs, tq, tk):
    ln = tq + tk - 1
    i = jnp.arange(ln + 1)
    shift = jnp.where(i < tk, -i, ln + 1 - i)
    ext = jnp.moveaxis(fn(jnp.asarray(offs)[:, None] + shift[None, :]), -1, 0)
    flat = jnp.tile(ext, (1, 1, tq))[:, :, :tq * ln]
    return flat.reshape(ext.shape[0], len(offs), tq, ln)[:, :, :, :tk]


def _bias_fn(table, window):
    def fn(dist):
        ok = dist >= 0
        if window is not None:
            ok = ok & (dist < window)
        return jnp.where(ok[..., None], table[_t5_bucket(dist)].astype(F32), NEG_INF)
    return fn


def _head_tiles(t):
    return t.reshape(N_KV, GROUP, *t.shape[1:]).transpose(0, 2, 1, 3, 4)


def _in_proj_weight(w_in):
    sizes = (512, 128, 128, 128, 128, 128, 128, 24, 512, 128, 128, MLA_Q_RANK, MLA_KV_RANK,
             MLA_ROPE, 3 * D_MODEL)
    offs = np.concatenate([[0], np.cumsum(sizes)])
    seg = [w_in[:, offs[i]:offs[i + 1]] for i in range(len(sizes))]
    (nq, nkc, nvc, nks, nvs, nkw, nvw, ngate, sq, sk, sv, mq, mkv, mkr, mgate) = seg
    d = w_in.shape[0]
    hd = HEAD_DIM

    def pad(a, left=0):
        return jnp.pad(a, ((0, 0), (left, LANE - left - a.shape[1])))

    def kv_blocks(k, v):
        return [jnp.concatenate([k[:, h * hd:(h + 1) * hd], v[:, h * hd:(h + 1) * hd]], axis=1)
                for h in range(N_KV)]

    mkr3 = jnp.concatenate([mkr, mkr[:, :MLA_ROPE // 2]], axis=1)
    cols = ([mq, pad(ngate), mkv, pad(mkr3, MLA_NOPE), jnp.zeros((d, LANE), w_in.dtype),
             nq * QK_SCALE, nkc, nvc] + kv_blocks(nks, nvs) + kv_blocks(nkw, nvw)
            + [sq * QK_SCALE] + kv_blocks(sk, sv) + [mgate])
    w = jnp.concatenate(cols, axis=1)
    assert w.shape == (d, Z_WIDTH)
    return w.astype(BF16)


def _mla_weights(w_qb, w_kvb, w_c):
    def blocks(a):
        return jnp.pad(a, ((0, 0), (0, 0), (0, LANE - a.shape[2]))).reshape(a.shape[0], -1)

    wq = w_qb.reshape(MLA_Q_RANK, N_HEADS, MLA_QK)
    wq = blocks(jnp.concatenate([wq, wq[:, :, MLA_NOPE:MLA_NOPE + MLA_ROPE // 2]], axis=2))
    wkv = w_kvb.reshape(MLA_KV_RANK, N_HEADS, MLA_NOPE + MLA_V)
    wkt = blocks(wkv[:, :, :MLA_NOPE]).T
    wv = blocks(wkv[:, :, MLA_NOPE:])
    wc = jnp.pad(w_c.reshape(N_HEADS, MLA_V, -1), ((0, 0), (0, LANE - MLA_V), (0, 0)))
    return (wq.astype(BF16), wkt.astype(BF16), wv.astype(BF16),
            wc.reshape(N_HEADS * LANE, -1).astype(BF16))


def _gate_expand():
    e = np.zeros((2 * LANE, 3 * BRANCH_WIDTH), np.float32)
    for n in range(3):
        for h in range(N_HEADS):
            cols = slice(n * BRANCH_WIDTH + h * HEAD_DIM, n * BRANCH_WIDTH + (h + 1) * HEAD_DIM)
            e[h * 3 + n, cols] = 1.0
            e[LANE + h * 3 + n, cols] = 1.0
    return jnp.asarray(e, BF16)


def _rope_tables(pos):
    half = MLA_ROPE // 2
    inv_freq = ROPE_BASE ** (-jnp.arange(half, dtype=F32) / half)
    ang = pos.astype(F32)[:, None] * inv_freq[None, :]
    cos, sin = jnp.cos(ang), jnp.sin(ang)
    n = pos.shape[0]
    ones = jnp.ones((n, MLA_NOPE), F32)
    tail = jnp.zeros((n, LANE - MLA_QK), F32)
    cos_t = jnp.concatenate([ones, cos, cos, tail], axis=1)
    sin_t = jnp.concatenate([0 * ones, -sin, sin, tail], axis=1)
    return cos_t, sin_t


def kernel(x, rel_bias_table, ffn1_norm_pre, ffn1_w_gu, ffn1_w_down, ffn1_norm_post,
           mix_norm_pre, w_in, nsa_pe_k, nsa_w1_k, nsa_w2_k, nsa_pe_v, nsa_w1_v, nsa_w2_v,
           swa_sinks, mla_q_norm, mla_w_qb, mla_kv_norm, mla_w_kvb, w_branch, w_out,
           mix_norm_post, ffn2_norm_pre, ffn2_w_gu, ffn2_w_down, ffn2_norm_post):
    b, s, d = x.shape
    depth = w_in.shape[0]
    m = b * s
    tq = 256
    tq_band = min(256, s // 2)
    tq_sel = min(512, s)
    tk_sel = min(512, s)
    tq_mla = min(512, s)
    ncp = s // CMP_STRIDE
    nsel = s // SEL_LEN
    tm = min(512, m)
    tm_ffn = min(512 * FFN_CHAINS, m)
    tm_big = min(1024, s)

    pos = jnp.arange(s, dtype=jnp.int32)
    tab_a = rel_bias_table[:, :N_HEADS] * LOG2E
    tab_b = rel_bias_table[:, N_HEADS:] * LOG2E
    n_far = -(-(LAST_BUCKET_DIST + tk_sel - 1) // tq_sel)
    n_sel_tiles = min(n_far + 1, s // tq_sel)
    bias_sel = _head_tiles(_toeplitz(_bias_fn(tab_a, None),
                                     [u * tq_sel for u in range(n_sel_tiles)], tq_sel, tk_sel))
    tb = tq_band
    win_offs = sorted({min(i * tb, NSA_WINDOW) for i in range(-(-NSA_WINDOW // tb) + 1)})
    bias_win = _head_tiles(_toeplitz(_bias_fn(tab_a, NSA_WINDOW), win_offs, tb, NSA_WINDOW + tb))
    swa_offs = sorted({min(i * tb, SWA_WINDOW) for i in range(-(-SWA_WINDOW // tb) + 1)})
    bias_swa = _head_tiles(_toeplitz(_bias_fn(tab_b, SWA_WINDOW), swa_offs, tb, SWA_WINDOW + tb))

    def cmp_bias_fn(e):
        r = jnp.arange(CMP_STRIDE)
        dist = e[..., None] * CMP_STRIDE + r - (CMP_LEN - 1)
        return tab_a[_t5_bucket(dist)].astype(F32).reshape(*e.shape, CMP_STRIDE * N_HEADS)

    bias_c = _toeplitz(cmp_bias_fn, [0], ncp, ncp).reshape(CMP_STRIDE, N_HEADS, ncp, ncp)
    bias_c = bias_c.transpose(1, 2, 0, 3).reshape(N_KV, GROUP, s, ncp)
    ci = jnp.arange(ncp)[None, :] * CMP_STRIDE
    sj = jnp.arange(nsel)[:, None] * SEL_LEN
    ovl_t = (jnp.maximum(jnp.minimum(ci + CMP_LEN, sj + SEL_LEN) - jnp.maximum(ci, sj), 0)
             .astype(F32) / CMP_LEN)
    cos_t, sin_t = _rope_tables(pos)
    gexp = _gate_expand()
    assert nsel <= HEAD_DIM
    sel_rows = jnp.where(jnp.arange(HEAD_DIM)[:, None] == (pos[None, :] >> SEL_SHIFT),
                         NEG_INF, 0.0).astype(BF16)
    chunk = CMP_STRIDE * HEAD_DIM

    xf = x.reshape(m, d)
    for l in range(depth):
        xf = _ffn(xf, ffn1_norm_pre[l][None], ffn1_w_gu[l].astype(BF16),
                  ffn1_w_down[l].astype(BF16), ffn1_norm_post[l][None], tm=tm_ffn)

        z = _inproj(xf, mix_norm_pre[l][None], _in_proj_weight(w_in[l]), tm=tm_big, tn=Z_WIDTH // 2)

        def chunks(blk):
            a = z[:, blk * LANE:(blk + 1) * LANE].reshape(b, s, N_KV, HEAD_DIM)
            return a.transpose(0, 2, 1, 3).reshape(b, N_KV, ncp, chunk)

        kc, vc = _cmp_mlp(chunks(ZB_CMP_K), chunks(ZB_CMP_V),
                          nsa_pe_k[l].reshape(2, chunk), nsa_w1_k[l].astype(BF16),
                          nsa_w2_k[l].astype(BF16),
                          nsa_pe_v[l].reshape(2, chunk), nsa_w1_v[l].astype(BF16),
                          nsa_w2_v[l].astype(BF16))
        o_cmp, dropped = _cmp_attn(z, jnp.tile(kc.transpose(0, 1, 3, 2), (1, 1, 2, 1)),
                                   jnp.tile(vc, (1, 1, 1, 2)),
                                   bias_c, ovl_t, tq=tq, seq=s)
        o_sel = _sel_attention(z, dropped, sel_rows, bias_sel, tq=tq_sel, tk=tk_sel, seq=s)
        o_win = _band_attention(z, bias_win, q_block=ZB_NQ, kv_block=ZB_WIN_KV, tq=tq_band, seq=s,
                                back=NSA_WINDOW, name="attn_win")
        o_b = _band_attention(z, bias_swa, q_block=ZB_SQ, kv_block=ZB_SWA_KV, tq=tq_band, seq=s,
                              back=SWA_WINDOW, name="attn_swa",
                              sinks=(swa_sinks[l] * LOG2E).reshape(N_KV, GROUP, 1, 1))
        wq, wkt, wv, wc = _mla_weights(mla_w_qb[l], mla_w_kvb[l], w_branch[l, 2])
        mq, mk, mv1 = _mla_proj(z, cos_t, sin_t, mla_q_norm[l][None], wq,
                                mla_kv_norm[l][None], wkt, wv, tm=tm_big, seq=s)
        o_c = _mla_attention(mq, mk, mv1, tq=tq_mla, seq=s)

        xf = _merge(o_cmp, o_sel, o_win, o_b, o_c, z, xf, gexp, w_branch[l, :2].astype(BF16), wc,
                    w_out[l].astype(BF16), mix_norm_post[l][None], tm=tm)

        xf = _ffn(xf, ffn2_norm_pre[l][None], ffn2_w_gu[l].astype(BF16),
                  ffn2_w_down[l].astype(BF16), ffn2_norm_post[l][None], tm=tm_ffn)
    return xf.reshape(b, s, d)
```
